```python
import jax
import jax.numpy as jnp
from jax import lax
import numpy as np


D_MODEL = 1024
BATCH = 32
SEQ = 256
DEPTH = 2
DEC_BATCH = 8
DEC_SEQ = 4096
PAST_LEN = 256

GRID_W = 64
POOL_GROUPS = 4
POOL_GROUP_WIDTH = 128
POOL_WIDTH = POOL_GROUPS * POOL_GROUP_WIDTH
POOL_WINDOWS = (2, 4, 8, 16)
LRU_WIDTH = D_MODEL
LRU_HEADS = 16
LRU_HEAD_DIM = LRU_WIDTH // LRU_HEADS
CONV_WIDTH = 4
CONV_LEFT = 2
LRU_C = 8.0
N_EXPERTS = 16
EXPERT_FF = 2048
CAPACITY_FACTOR = 2
N_MOD = 6
IN_WIDTH = POOL_WIDTH + LRU_WIDTH + 2 * D_MODEL
EPS = 1e-6

kernel_name = 'hybrid_pool_rglru_ecmoe_diffusion_step'


def _rmsnorm(x, g):
    x32 = x.astype(jnp.float32)
    y = x32 * lax.rsqrt(jnp.mean(x32 * x32, axis=-1, keepdims=True) + EPS)
    return (y * g.astype(jnp.float32)).astype(x.dtype)


def _multiscale_pool(p):
    n = p.shape[-2]
    p32 = p.astype(jnp.float32)
    cs = jnp.cumsum(p32, axis=-2)
    cs = jnp.concatenate([jnp.zeros_like(cs[..., :1, :]), cs], axis=-2)
    t = jnp.arange(n)
    outs = []
    for g, w in enumerate(POOL_WINDOWS):
        lo = jnp.clip(t - w // 2, 0, n)
        hi = jnp.clip(t + w // 2, 0, n)
        sl = slice(g * POOL_GROUP_WIDTH, (g + 1) * POOL_GROUP_WIDTH)
        seg = cs[..., sl]
        s = jnp.take(seg, hi, axis=-2) - jnp.take(seg, lo, axis=-2)
        cnt = (hi - lo).astype(jnp.float32)[:, None]
        outs.append(s / cnt - p32[..., sl])
    return jnp.concatenate(outs, axis=-1)


def _pool_mixer(p, w_group, scale, rows):
    bsz, n, ch = p.shape
    if rows is None:
        pooled = _multiscale_pool(p)
    else:
        pooled = _multiscale_pool(p.reshape(bsz, rows, GRID_W, ch)).reshape(bsz, n, ch)
    pg = pooled.reshape(bsz, n, POOL_GROUPS, POOL_GROUP_WIDTH)
    y = jnp.einsum('bngc,gcd->bngd', pg, w_group.astype(jnp.float32)).reshape(bsz, n, ch)
    return (y * scale.astype(jnp.float32)).astype(p.dtype)


def _centred_dwconv(v, w, b):
    n = v.shape[1]
    vp = jnp.pad(v, ((0, 0), (CONV_LEFT, CONV_WIDTH - 1 - CONV_LEFT), (0, 0)))
    out = b
    for k in range(CONV_WIDTH):
        out = out + vp[:, k:k + n] * w[k]
    return out


def _rglru_bidir(v, w_r, b_r, w_i, b_i, lam, h0):
    bsz, n, _ = v.shape
    v32 = v.astype(jnp.float32)
    vh = v32.reshape(bsz, n, LRU_HEADS, LRU_HEAD_DIM)

    def gate(w, bias):
        z = jnp.einsum('bnhd,zhde->zbnhe', vh, w.astype(jnp.float32)).reshape(2, bsz, n, LRU_WIDTH)
        return jax.nn.sigmoid(z + bias.astype(jnp.float32)[:, None, None, :])

    r = gate(w_r, b_r)
    i = gate(w_i, b_i)
    log_a = -LRU_C * r * jax.nn.softplus(-lam.astype(jnp.float32))[:, None, None, :]
    a = jnp.exp(log_a)
    u = jnp.sqrt(-jnp.expm1(2.0 * log_a)) * i * v32[None]
    a = jnp.stack([a[0], a[1][:, ::-1]])
    u = jnp.stack([u[0], u[1][:, ::-1]])

    def step(h, au):
        a_t, u_t = au
        h = a_t * h + u_t
        return h, h

    h_last, hs = lax.scan(step, h0, (jnp.moveaxis(a, 2, 0), jnp.moveaxis(u, 2, 0)))
    y = hs[:, 0] + hs[::-1, 1]
    return jnp.moveaxis(y, 0, 1).astype(v.dtype), h_last


def _token_mixer(h, lp, rows, h0):
    d = h.shape[-1]
    u = h @ lp['w_in']
    p = u[..., :POOL_WIDTH]
    v = u[..., POOL_WIDTH:POOL_WIDTH + LRU_WIDTH]
    gl = u[..., POOL_WIDTH + LRU_WIDTH:]
    y_pool = _pool_mixer(p, lp['pool_w'], lp['pool_scale'], rows)
    v = _centred_dwconv(v, lp['conv_w'], lp['conv_b'])
    y_lru, h_last = _rglru_bidir(v, lp['lru_wr'], lp['lru_br'], lp['lru_wi'], lp['lru_bi'], lp['lru_lambda'], h0)
    g = jax.nn.sigmoid(gl.astype(jnp.float32)).astype(h.dtype)
    merged = g[..., :d] * (y_pool @ lp['w_br_pool']) + g[..., d:] * (y_lru @ lp['w_br_lru'])
    return merged @ lp['w_out'], h_last


def _expert_choice_moe(h, router_w, w1, w3, w2):
    bsz, n, d = h.shape
    cap = CAPACITY_FACTOR * n // N_EXPERTS
    aff = jax.nn.softmax(jnp.einsum('bnd,de->bne', h.astype(jnp.float32), router_w.astype(jnp.float32)), axis=-1)
    gate, idx = lax.top_k(jnp.swapaxes(aff, 1, 2), cap)
    xs = jax.vmap(lambda hb, ib: hb[ib])(h, idx)
    hid = jax.nn.silu(jnp.einsum('becd,edf->becf', xs, w1)) * jnp.einsum('becd,edf->becf', xs, w3)
    out = jnp.einsum('becf,efd->becd', hid, w2) * gate[..., None].astype(h.dtype)
    y = jax.vmap(lambda ib, ob: jnp.zeros((n, d), ob.dtype).at[ib.reshape(-1)].add(ob.reshape(-1, d)))(idx, out)
    return y.astype(h.dtype)


def _layer(x, mod, lp, rows, h0):
    shift1, scale1, gate1, shift2, scale2, gate2 = jnp.split(mod.astype(x.dtype), N_MOD, axis=-1)
    hn = _rmsnorm(x, lp['norm1']) * (1.0 + scale1) + shift1
    mix, h_last = _token_mixer(hn, lp, rows, h0)
    x = x + gate1 * mix
    hn = _rmsnorm(x, lp['norm2']) * (1.0 + scale2) + shift2
    x = x + gate2 * _expert_choice_moe(hn, lp['router_w'], lp['exp_w1'], lp['exp_w3'], lp['exp_w2'])
    return x, h_last


def setup_inputs(seed: int = 0) -> dict:
    key = jax.random.key(seed)
    ks = jax.random.split(key, 32)
    f32 = jnp.float32
    D, P, R, H, Dh, E, F = D_MODEL, POOL_WIDTH, LRU_WIDTH, LRU_HEADS, LRU_HEAD_DIM, N_EXPERTS, EXPERT_FF

    def nrm(k, shape, s):
        return jax.random.normal(k, shape, f32) * s

    a0 = jax.random.uniform(ks[19], (DEPTH, 2, R), f32, 0.9, 0.999)
    return {
        'x_prompt': nrm(ks[0], (BATCH, SEQ, D), 1.0),
        'x_sample': nrm(ks[1], (DEC_BATCH, DEC_SEQ, D), 1.0),
        'state_lru': nrm(ks[2], (DEC_BATCH, DEPTH, 2, R), 0.5),
        'c': nrm(ks[3], (DEC_BATCH, D), 1.0),
        'c_ctx': nrm(ks[4], (D,), 1.0),
        'norm1_g': 1.0 + nrm(ks[5], (DEPTH, D), 0.02),
        'norm2_g': 1.0 + nrm(ks[6], (DEPTH, D), 0.02),
        'final_g': 1.0 + nrm(ks[7], (D,), 0.02),
        'w_mod': nrm(ks[8], (DEPTH, D, N_MOD * D), 0.5 * D ** -0.5),
        'b_mod': nrm(ks[9], (DEPTH, N_MOD * D), 0.02),
        'w_in': nrm(ks[10], (DEPTH, D, IN_WIDTH), D ** -0.5),
        'pool_w': nrm(ks[11], (DEPTH, POOL_GROUPS, POOL_GROUP_WIDTH, POOL_GROUP_WIDTH), POOL_GROUP_WIDTH ** -0.5),
        'pool_scale': 1.0 + nrm(ks[12], (DEPTH, P), 0.02),
        'conv_w': nrm(ks[13], (DEPTH, CONV_WIDTH, R), CONV_WIDTH ** -0.5),
        'conv_b': nrm(ks[14], (DEPTH, R), 0.02),
        'lru_wr': nrm(ks[15], (DEPTH, 2, H, Dh, Dh), Dh ** -0.5),
        'lru_br': nrm(ks[16], (DEPTH, 2, R), 0.02),
        'lru_wi': nrm(ks[17], (DEPTH, 2, H, Dh, Dh), Dh ** -0.5),
        'lru_bi': nrm(ks[18], (DEPTH, 2, R), 0.02),
        'lru_lambda': jnp.log(a0) - jnp.log1p(-a0),
        'w_br_pool': nrm(ks[20], (DEPTH, P, D), P ** -0.5),
        'w_br_lru': nrm(ks[21], (DEPTH, R, D), R ** -0.5),
        'w_out': nrm(ks[22], (DEPTH, D, D), D ** -0.5),
        'router_w': nrm(ks[23], (DEPTH, D, E), D ** -0.5),
        'exp_w1': nrm(ks[24], (DEPTH, E, D, F), D ** -0.5),
        'exp_w3': nrm(ks[25], (DEPTH, E, D, F), D ** -0.5),
        'exp_w2': nrm(ks[26], (DEPTH, E, F, D), F ** -0.5),
    }


def reference(x_prompt, x_sample, state_lru, c, c_ctx, norm1_g, norm2_g, final_g, w_mod, b_mod, w_in,
              pool_w, pool_scale, conv_w, conv_b, lru_wr, lru_br, lru_wi, lru_bi, lru_lambda,
              w_br_pool, w_br_lru, w_out, router_w, exp_w1, exp_w3, exp_w2):
    rows = x_sample.shape[1] // GRID_W
    silu_ctx = jax.nn.silu(c_ctx)
    silu_c = jax.nn.silu(c)
    xp = x_prompt
    xs = x_sample
    ctx_states = []
    for l in range(DEPTH):
        lp = {
            'norm1': norm1_g[l], 'norm2': norm2_g[l], 'w_in': w_in[l],
            'pool_w': pool_w[l], 'pool_scale': pool_scale[l],
            'conv_w': conv_w[l], 'conv_b': conv_b[l],
            'lru_wr': lru_wr[l], 'lru_br': lru_br[l], 'lru_wi': lru_wi[l], 'lru_bi': lru_bi[l],
            'lru_lambda': lru_lambda[l],
            'w_br_pool': w_br_pool[l], 'w_br_lru': w_br_lru[l], 'w_out': w_out[l],
            'router_w': router_w[l], 'exp_w1': exp_w1[l], 'exp_w3': exp_w3[l], 'exp_w2': exp_w2[l],
        }
        mod_ctx = silu_ctx @ w_mod[l] + b_mod[l]
        h0_ctx = jnp.zeros((2, xp.shape[0], LRU_WIDTH), jnp.float32)
        xp, h_last_ctx = _layer(xp, mod_ctx, lp, None, h0_ctx)
        ctx_states.append(jnp.swapaxes(h_last_ctx, 0, 1))
        mod_lat = (silu_c @ w_mod[l] + b_mod[l])[:, None, :]
        h0_lat = jnp.swapaxes(state_lru[:, l], 0, 1).astype(jnp.float32)
        xs, _ = _layer(xs, mod_lat, lp, rows, h0_lat)
    y_prompt = _rmsnorm(xp, final_g)
    y_sample = _rmsnorm(xs, final_g)
    new_state_lru = jnp.stack(ctx_states, axis=1).astype(x_prompt.dtype)
    return (y_prompt, y_sample, new_state_lru)
```

```python
import functools
from typing import NamedTuple

import jax
import jax.numpy as jnp
from jax import lax
from jax.experimental import pallas as pl
from jax.experimental.pallas import tpu as pltpu

F32 = jnp.float32
BF16 = jnp.bfloat16
I32 = jnp.int32
HIGHEST = lax.Precision.HIGHEST

EPS = 1e-6
LRU_C = 8.0
CONV_WIDTH = 4
POOL_HALF_WINDOWS = (1, 2, 4, 8)
N_MOD = 6
CAPACITY_FACTOR = 2
GRID_W = 64
NOT_SELECTED = -(1 << 20)
SUBLANES = 8
BF16_ROWS = 16
LANES = 128
MXU_DIM = 256
VMEM_LIMIT = 56 * 1024 * 1024


class Cfg(NamedTuple):
    d: int
    gw: int
    r: int
    dh: int
    e: int
    f: int
    n_ctx_req: int
    n_ctx: int
    n_lat_req: int
    n_lat: int
    grid_w: int

    @property
    def t(self): return self.n_ctx
    @property
    def p(self): return self.gw * len(POOL_HALF_WINDOWS)
    @property
    def ntc(self): return self.n_ctx_req
    @property
    def tpr(self): return self.n_lat // self.t
    @property
    def nt(self): return self.ntc + self.n_lat_req * self.tpr
    @property
    def n_tok(self): return self.nt * self.t
    @property
    def c_ctx(self): return CAPACITY_FACTOR * self.n_ctx // self.e
    @property
    def c_lat(self): return CAPACITY_FACTOR * self.n_lat // self.e
    @property
    def sblk(self): return self.c_lat
    @property
    def rpb(self): return self.c_lat // self.c_ctx
    @property
    def n_sblk(self): return self.n_ctx_req // self.rpb + self.n_lat_req
    @property
    def w(self): return min(64, self.sblk)
    @property
    def bd(self): return min(MXU_DIM, self.r)
    @property
    def n_tab(self): return 16


def _cparams():
    return pltpu.CompilerParams(dimension_semantics=("arbitrary",), vmem_limit_bytes=VMEM_LIMIT)


def _tab_row(i, cfg):
    return jnp.where(i < cfg.ntc, 0, 1 + (i - cfg.ntc) // cfg.tpr)


def _sigmoid(x):
    return 1.0 / (1.0 + jnp.exp(-x))


def _norm_mod(x, g, scale, shift):
    y = x * lax.rsqrt(jnp.mean(x * x, axis=-1, keepdims=True) + EPS)
    return (y * g) * (1.0 + scale) + shift


def _mod_kernel(c_ref, w_ref, b_ref, o_ref):
    c = c_ref[...]
    s = c * _sigmoid(c)
    o_ref[0] = jnp.dot(s, w_ref[0], precision=HIGHEST, preferred_element_type=F32) + b_ref[0]


def _modulation(cvec, w_mod, b_mod, cfg):
    depth, d, n6 = w_mod.shape
    nc = n6 // 4
    return pl.pallas_call(
        _mod_kernel,
        grid=(depth, n6 // nc),
        in_specs=[pl.BlockSpec((cfg.n_tab, d), lambda l, j: (0, 0)),
                  pl.BlockSpec((1, d, nc), lambda l, j: (l, 0, j)),
                  pl.BlockSpec((1, 1, nc), lambda l, j: (l, 0, j))],
        out_specs=pl.BlockSpec((1, cfg.n_tab, nc), lambda l, j: (l, 0, j)),
        out_shape=jax.ShapeDtypeStruct((depth, cfg.n_tab, n6), F32),
        compiler_params=pltpu.CompilerParams(dimension_semantics=("arbitrary", "arbitrary"),
                                             vmem_limit_bytes=VMEM_LIMIT),
        name="modulation",
    )(cvec, w_mod, b_mod.reshape(depth, 1, n6))


def _inproj_kernel(x_ref, mod_ref, g_ref, wp_ref, wv_ref, wg_ref, p_ref, v_ref, gl_ref, *, d):
    m = mod_ref[0]
    hn = _norm_mod(x_ref[...], g_ref[...], m[:, d:2 * d], m[:, 0:d]).astype(BF16)
    p_ref[...] = jnp.dot(hn, wp_ref[...], preferred_element_type=F32)
    v_ref[...] = jnp.dot(hn, wv_ref[...], preferred_element_type=F32)
    gl_ref[...] = jnp.dot(hn, wg_ref[...], preferred_element_type=F32)


def _inproj(x, mod, g, wp, wv, wg, cfg):
    t, d = cfg.t, cfg.d
    n = x.shape[0]
    tile = lambda width: pl.BlockSpec((t, width), lambda i: (i, 0))
    full = lambda a: pl.BlockSpec(a.shape, lambda i: (0,) * a.ndim)
    return pl.pallas_call(
        functools.partial(_inproj_kernel, d=d),
        grid=(cfg.nt,),
        in_specs=[tile(d),
                  pl.BlockSpec((1, 1, N_MOD * d), lambda i: (_tab_row(i, cfg), 0, 0)),
                  full(g), full(wp), full(wv), full(wg)],
        out_specs=[tile(cfg.p), tile(cfg.r), tile(2 * d)],
        out_shape=[jax.ShapeDtypeStruct((n, cfg.p), F32),
                   jax.ShapeDtypeStruct((n, cfg.r), F32),
                   jax.ShapeDtypeStruct((n, 2 * d), F32)],
        compiler_params=_cparams(),
        name="inproj",
    )(x, mod, g, wp, wv, wg)


def _conv(v, prev, nxt, w, b):
    t = v.shape[0]
    row = lax.broadcasted_iota(I32, v.shape, 0)
    vm1 = jnp.where(row == 0, prev[7:8], pltpu.roll(v, 1, 0))
    vm2 = jnp.where(row == 0, prev[6:7], jnp.where(row == 1, prev[7:8], pltpu.roll(v, 2, 0)))
    vp1 = jnp.where(row == t - 1, nxt[0:1], pltpu.roll(v, t - 1, 0))
    return b + vm2 * w[0:1] + vm1 * w[1:2] + v * w[2:3] + vp1 * w[3:4]


def _gates(vc, wbd_ref, br, bi, lam, cfg):
    bd = cfg.bd
    vcb = vc.astype(BF16)
    zr, zi = [], []
    for j in range(cfg.r // bd):
        z = jnp.dot(vcb[:, j * bd:(j + 1) * bd], wbd_ref[j], preferred_element_type=F32)
        zr.append(z[:, :bd])
        zi.append(z[:, bd:])
    rg = _sigmoid(jnp.concatenate(zr, axis=1) + br)
    ig = _sigmoid(jnp.concatenate(zi, axis=1) + bi)
    nl = -lam
    softplus = jnp.maximum(nl, 0.0) + jnp.log1p(jnp.exp(-jnp.abs(nl)))
    log_a = -LRU_C * rg * softplus
    a = jnp.exp(log_a)
    u = jnp.sqrt(-jnp.tanh(log_a) * (a * a + 1.0)) * ig * vc
    return a, u


def _local_scan(a, u, reverse):
    t = a.shape[0]
    row8 = lax.broadcasted_iota(I32, a.shape, 0) % SUBLANES
    for s in (1, 2, 4):
        ok = (row8 < SUBLANES - s) if reverse else (row8 >= s)
        shift = t - s if reverse else s
        a_sh = jnp.where(ok, pltpu.roll(a, shift, 0), 1.0)
        u_sh = jnp.where(ok, pltpu.roll(u, shift, 0), 0.0)
        u = u + a * u_sh
        a = a * a_sh
    return a, u


def _carry_scan(a_scr, u_scr, out_ref, h_in, reverse):
    groups = a_scr.shape[0] // SUBLANES

    def body(j, h):
        g = groups - 1 - j if reverse else j
        r0 = pl.multiple_of(g * SUBLANES, SUBLANES)
        hg = a_scr[pl.ds(r0, SUBLANES), :] * h + u_scr[pl.ds(r0, SUBLANES), :]
        out_ref[pl.ds(r0, SUBLANES), :] = hg
        return hg[0:1] if reverse else hg[SUBLANES - 1:SUBLANES]

    return lax.fori_loop(0, groups, body, h_in)


def _tile_flags(i, cfg):
    is_ctx = i < cfg.ntc
    k = jnp.maximum(i - cfg.ntc, 0) % cfg.tpr
    first = jnp.logical_or(is_ctx, k == 0)
    last = jnp.logical_or(is_ctx, k == cfg.tpr - 1)
    return is_ctx, first, last


def _scan_bwd_kernel(v_ref, vprev_ref, vnext_ref, cw_ref, cb_ref, wbd_ref, br_ref, bi_ref, lam_ref, h0_ref,
                     vc_ref, hb_ref, hlast_ref, a_scr, u_scr, carry_scr, *, cfg):
    i = cfg.nt - 1 - pl.program_id(0)
    is_ctx, first, last = _tile_flags(i, cfg)
    prev = jnp.where(first, 0.0, vprev_ref[...])
    nxt = jnp.where(last, 0.0, vnext_ref[...])
    vc = _conv(v_ref[...], prev, nxt, cw_ref[...], cb_ref[...])
    vc_ref[...] = vc
    a, u = _gates(vc, wbd_ref, br_ref[...], bi_ref[...], lam_ref[...], cfg)
    a, u = _local_scan(a, u, True)
    a_scr[...] = a
    u_scr[...] = u

    @pl.when(last)
    def _():
        carry_scr[...] = h0_ref[0]

    h = _carry_scan(a_scr, u_scr, hb_ref, carry_scr[...], True)
    carry_scr[...] = h

    @pl.when(is_ctx)
    def _():
        hlast_ref[0] = h


def _scan_bwd(v, cw, cb, wbd, br, bi, lam, h0, cfg):
    t, r, nt = cfg.t, cfg.r, cfg.nt
    n = v.shape[0]
    tb = t // SUBLANES
    rev = lambda s: nt - 1 - s
    full = lambda a: pl.BlockSpec(a.shape, lambda s: (0,) * a.ndim)
    tile = pl.BlockSpec((t, r), lambda s: (rev(s), 0))
    return pl.pallas_call(
        functools.partial(_scan_bwd_kernel, cfg=cfg),
        grid=(nt,),
        in_specs=[tile,
                  pl.BlockSpec((SUBLANES, r), lambda s: (jnp.maximum(rev(s) * tb - 1, 0), 0)),
                  pl.BlockSpec((SUBLANES, r), lambda s: (jnp.minimum((rev(s) + 1) * tb, n // SUBLANES - 1), 0)),
                  full(cw), full(cb), full(wbd), full(br), full(bi), full(lam),
                  pl.BlockSpec((1, 1, r), lambda s: (_tab_row(rev(s), cfg), 0, 0))],
        out_specs=[tile, tile,
                   pl.BlockSpec((1, 1, r), lambda s: (jnp.minimum(rev(s), cfg.ntc - 1), 0, 0))],
        out_shape=[jax.ShapeDtypeStruct((n, r), F32),
                   jax.ShapeDtypeStruct((n, r), F32),
                   jax.ShapeDtypeStruct((cfg.n_ctx_req, 1, r), F32)],
        scratch_shapes=[pltpu.VMEM((t, r), F32), pltpu.VMEM((t, r), F32), pltpu.VMEM((1, r), F32)],
        compiler_params=_cparams(),
        name="scan_bwd",
    )(v, v, v, cw, cb, wbd, br, bi, lam, h0)


def _pool(pv, is_ctx, cfg):
    t = pv.shape[0]
    gw = cfg.gw
    row = lax.broadcasted_iota(I32, (t, gw), 0)
    pos = jnp.where(is_ctx, row % cfg.n_ctx, row % cfg.grid_w)
    length = jnp.where(is_ctx, cfg.n_ctx, cfg.grid_w)

    def shifted(x, dlt):
        ok = jnp.logical_and(pos + dlt >= 0, pos + dlt < length)
        return jnp.where(ok, pltpu.roll(x, (-dlt) % t, 0), 0.0)

    outs = []
    for g, m in enumerate(POOL_HALF_WINDOWS):
        x = pv[:, g * gw:(g + 1) * gw]
        back, fwd, k = x, x, 1
        while k < m:
            back = back + shifted(back, -k)
            fwd = fwd + shifted(fwd, k)
            k *= 2
        s = shifted(back, -1) + fwd
        cnt = (jnp.minimum(pos + m, length) - jnp.maximum(pos - m, 0)).astype(F32)
        outs.append(s / cnt - x)
    return outs


def _scan_fwd_kernel(x_ref, p_ref, vc_ref, gl_ref, hb_ref, mod_ref, h0_ref, wbd_ref, br_ref, bi_ref, lam_ref,
                     pw_ref, ps_ref, wbp_ref, wbl_ref, wo_ref, g2_ref, rw_ref,
                     x1_ref, hn2_ref, lg_ref, hlast_ref, a_scr, u_scr, hf_scr, carry_scr, *, cfg):
    i = pl.program_id(0)
    d, gw = cfg.d, cfg.gw
    is_ctx, first, _ = _tile_flags(i, cfg)
    vc = vc_ref[...]
    a, u = _gates(vc, wbd_ref, br_ref[...], bi_ref[...], lam_ref[...], cfg)
    a, u = _local_scan(a, u, False)
    a_scr[...] = a
    u_scr[...] = u

    @pl.when(first)
    def _():
        carry_scr[...] = h0_ref[0]

    h = _carry_scan(a_scr, u_scr, hf_scr, carry_scr[...], False)
    carry_scr[...] = h

    @pl.when(is_ctx)
    def _():
        hlast_ref[0] = h

    y_lru = (hf_scr[...] + hb_ref[...]).astype(BF16)

    pooled = _pool(p_ref[...], is_ctx, cfg)
    ps = ps_ref[...]
    y_pool = jnp.concatenate(
        [jnp.dot(pooled[g].astype(BF16), pw_ref[g], preferred_element_type=F32) * ps[:, g * gw:(g + 1) * gw]
         for g in range(len(POOL_HALF_WINDOWS))], axis=1).astype(BF16)

    gate = _sigmoid(gl_ref[...])
    merged = (gate[:, :d] * jnp.dot(y_pool, wbp_ref[...], preferred_element_type=F32)
              + gate[:, d:] * jnp.dot(y_lru, wbl_ref[...], preferred_element_type=F32))
    mix = jnp.dot(merged.astype(BF16), wo_ref[...], preferred_element_type=F32)
    m = mod_ref[0]
    x1 = x_ref[...] + m[:, 2 * d:3 * d] * mix
    x1_ref[...] = x1
    hn2 = _norm_mod(x1, g2_ref[...], m[:, 4 * d:5 * d], m[:, 3 * d:4 * d])
    hn2_ref[...] = hn2.astype(BF16)
    lg_ref[...] = jnp.dot(hn2, rw_ref[...], precision=HIGHEST, preferred_element_type=F32)


def _scan_fwd(x, pv, vc, gl, hb, mod, h0, wbd, br, bi, lam, pw, ps, wbp, wbl, wo, g2, rw, cfg):
    t, d, r = cfg.t, cfg.d, cfg.r
    n = x.shape[0]
    tile = lambda width: pl.BlockSpec((t, width), lambda i: (i, 0))
    full = lambda a: pl.BlockSpec(a.shape, lambda i: (0,) * a.ndim)
    tab = lambda width: pl.BlockSpec((1, 1, width), lambda i: (_tab_row(i, cfg), 0, 0))
    return pl.pallas_call(
        functools.partial(_scan_fwd_kernel, cfg=cfg),
        grid=(cfg.nt,),
        in_specs=[tile(d), tile(cfg.p), tile(r), tile(2 * d), tile(r), tab(N_MOD * d), tab(r),
                  full(wbd), full(br), full(bi), full(lam), full(pw), full(ps), full(wbp), full(wbl),
                  full(wo), full(g2), full(rw)],
        out_specs=[tile(d), tile(d), tile(cfg.e),
                   pl.BlockSpec((1, 1, r), lambda i: (jnp.minimum(i, cfg.ntc - 1), 0, 0))],
        out_shape=[jax.ShapeDtypeStruct((n, d), F32),
                   jax.ShapeDtypeStruct((n, d), BF16),
                   jax.ShapeDtypeStruct((n, cfg.e), F32),
                   jax.ShapeDtypeStruct((cfg.n_ctx_req, 1, r), F32)],
        scratch_shapes=[pltpu.VMEM((t, r), F32), pltpu.VMEM((t, r), F32), pltpu.VMEM((t, r), F32),
                        pltpu.VMEM((1, r), F32)],
        compiler_params=_cparams(),
        name="scan_fwd",
    )(x, pv, vc, gl, hb, mod, h0, wbd, br, bi, lam, pw, ps, wbp, wbl, wo, g2, rw)


def _route_kernel(l_ref, selpos_ref, gate_ref, offs_ref, cnts_ref, *, cap, t):
    lg = l_ref[...]
    e, n = lg.shape
    nch = n // t
    ex = jnp.exp(lg - jnp.max(lg, axis=0, keepdims=True))
    aff = ex / jnp.sum(ex, axis=0, keepdims=True)
    gate_ref[...] = aff
    keys = pltpu.bitcast(aff, I32)

    def count(mask):
        return jnp.sum(mask.astype(F32), axis=1, keepdims=True)

    thr = jnp.zeros((e, 1), I32)
    for bit in range(30, -1, -1):
        cand = thr | (1 << bit)
        thr = jnp.where(count(keys >= cand) >= cap, cand, thr)
    gt = keys > thr
    eq = keys == thr
    need = cap - count(gt)

    tri = (lax.broadcasted_iota(I32, (t, t), 0) < lax.broadcasted_iota(I32, (t, t), 1)).astype(BF16)
    tok = lax.broadcasted_iota(I32, (n, LANES), 0)
    chunk = lax.broadcasted_iota(I32, (n, LANES), 1)
    before = (tok < chunk * t).astype(BF16)
    inside = (tok // t == chunk).astype(BF16)

    def prefix(mask):
        mb = mask.astype(BF16)
        offs = jnp.dot(mb, before, preferred_element_type=F32)
        pre = [jnp.dot(mb[:, k * t:(k + 1) * t], tri, preferred_element_type=F32) + offs[:, k:k + 1]
               for k in range(nch)]
        return jnp.concatenate(pre, axis=1) if nch > 1 else pre[0], offs, mb

    eq_rank, _, _ = prefix(eq)
    sel = jnp.logical_or(gt, jnp.logical_and(eq, eq_rank < need))
    pos, offs, selb = prefix(sel)
    selpos_ref[...] = jnp.where(sel, pos.astype(I32), NOT_SELECTED)
    offs_ref[0] = offs.astype(I32)
    cnts_ref[0] = jnp.dot(selb, inside, preferred_element_type=F32).astype(I32)


def _route(logits_t, n_req, n, col0, cap, cfg):
    e = cfg.e
    return pl.pallas_call(
        functools.partial(_route_kernel, cap=cap, t=cfg.t),
        grid=(n_req,),
        in_specs=[pl.BlockSpec((e, n), lambda b: (0, col0 + b))],
        out_specs=[pl.BlockSpec((e, n), lambda b: (0, b)),
                   pl.BlockSpec((e, n), lambda b: (0, b)),
                   pl.BlockSpec((1, e, LANES), lambda b: (b, 0, 0)),
                   pl.BlockSpec((1, e, LANES), lambda b: (b, 0, 0))],
        out_shape=[jax.ShapeDtypeStruct((e, n_req * n), I32),
                   jax.ShapeDtypeStruct((e, n_req * n), F32),
                   jax.ShapeDtypeStruct((n_req, e, LANES), I32),
                   jax.ShapeDtypeStruct((n_req, e, LANES), I32)],
        compiler_params=_cparams(),
        name="route",
    )(logits_t)


def _slot_block(i, cfg):
    return jnp.where(i < cfg.ntc, i // cfg.rpb, cfg.ntc // cfg.rpb + (i - cfg.ntc) // cfg.tpr)


def _windows(i, toff_ref, tcnt_ref, cfg):
    is_ctx = i < cfg.ntc
    base = jnp.where(is_ctx, (i % cfg.rpb) * cfg.c_ctx, 0)
    starts, npass = [], 0
    for e in range(cfg.e):
        off = base + toff_ref[i * cfg.e + e]
        start = (off // BF16_ROWS) * BF16_ROWS
        starts.append(start)
        npass = jnp.maximum(npass, (off + tcnt_ref[i * cfg.e + e] - start + cfg.w - 1) // cfg.w)
    return base, starts, npass


def _window(start, j, cfg):
    want = start + j * cfg.w
    a = pl.multiple_of(jnp.minimum(want, cfg.sblk - cfg.w), BF16_ROWS)
    return a, want - a


def _dispatch_kernel(toff_ref, tcnt_ref, x_ref, sp_ref, xs_ref, p_scr, *, cfg):
    i = pl.program_id(0)
    w = cfg.w
    is_ctx = i < cfg.ntc
    new_block = jnp.where(is_ctx, i % cfg.rpb == 0, jnp.maximum(i - cfg.ntc, 0) % cfg.tpr == 0)

    @pl.when(new_block)
    def _():
        xs_ref[...] = jnp.zeros(xs_ref.shape, xs_ref.dtype)

    base, starts, npass = _windows(i, toff_ref, tcnt_ref, cfg)
    slot = lax.broadcasted_iota(I32, (w, cfg.t), 0)

    def one_pass(j, carry):
        firsts = []
        for e in range(cfg.e):
            a, owned = _window(starts[e], j, cfg)
            firsts.append(a)
            rel = sp_ref[e:e + 1, :] - (a - base)
            p_scr[e * w:(e + 1) * w, :] = jnp.logical_and(rel == slot, slot >= owned).astype(BF16)
        rows = jnp.dot(p_scr[...], x_ref[...], preferred_element_type=F32)
        for e in range(cfg.e):
            cur = xs_ref[e, pl.ds(firsts[e], w), :].astype(F32)
            xs_ref[e, pl.ds(firsts[e], w), :] = (cur + rows[e * w:(e + 1) * w]).astype(BF16)
        return carry

    lax.fori_loop(0, npass, one_pass, 0)


def _dispatch(toff, tcnt, hn2, selpos, cfg):
    t, d, e = cfg.t, cfg.d, cfg.e
    return pl.pallas_call(
        functools.partial(_dispatch_kernel, cfg=cfg),
        grid_spec=pltpu.PrefetchScalarGridSpec(
            num_scalar_prefetch=2,
            grid=(cfg.nt,),
            in_specs=[pl.BlockSpec((t, d), lambda i, *_: (i, 0)),
                      pl.BlockSpec((e, t), lambda i, *_: (0, i))],
            out_specs=pl.BlockSpec((e, cfg.sblk, d), lambda i, *_: (0, _slot_block(i, cfg), 0)),
            scratch_shapes=[pltpu.VMEM((e * cfg.w, t), BF16)]),
        out_shape=jax.ShapeDtypeStruct((e, cfg.n_sblk * cfg.sblk, d), BF16),
        compiler_params=_cparams(),
        name="dispatch",
    )(toff, tcnt, hn2, selpos)


def _ffn_kernel(x_ref, w1_ref, w3_ref, w2_ref, o_ref, *, fc):
    x = x_ref[0]
    acc = jnp.zeros((x.shape[0], o_ref.shape[2]), F32)
    for c in range(w1_ref.shape[2] // fc):
        sl = slice(c * fc, (c + 1) * fc)
        h1 = jnp.dot(x, w1_ref[0, :, sl], preferred_element_type=F32)
        h3 = jnp.dot(x, w3_ref[0, :, sl], preferred_element_type=F32)
        hid = (h1 * _sigmoid(h1)) * h3
        acc = acc + jnp.dot(hid.astype(BF16), w2_ref[0, sl, :], preferred_element_type=F32)
    o_ref[0] = acc.astype(BF16)


def _ffn(xs, w1, w3, w2, cfg):
    e, s, d = xs.shape
    f = w1.shape[2]
    tm = cfg.sblk
    return pl.pallas_call(
        functools.partial(_ffn_kernel, fc=min(512, f)),
        grid=(e, s // tm),
        in_specs=[pl.BlockSpec((1, tm, d), lambda k, m: (k, m, 0)),
                  pl.BlockSpec((1, d, f), lambda k, m: (k, 0, 0)),
                  pl.BlockSpec((1, d, f), lambda k, m: (k, 0, 0)),
                  pl.BlockSpec((1, f, d), lambda k, m: (k, 0, 0))],
        out_specs=pl.BlockSpec((1, tm, d), lambda k, m: (k, m, 0)),
        out_shape=jax.ShapeDtypeStruct((e, s, d), BF16),
        compiler_params=pltpu.CompilerParams(dimension_semantics=("arbitrary", "arbitrary"),
                                             vmem_limit_bytes=VMEM_LIMIT),
        name="ffn",
    )(xs, w1, w3, w2)


def _combine_kernel(toff_ref, tcnt_ref, x1_ref, mod_ref, spt_ref, gt_ref, o_ref, fg_ref, out_ref, ow_scr,
                    *, cfg, final):
    i = pl.program_id(0)
    w, d = cfg.w, cfg.d
    base, starts, npass = _windows(i, toff_ref, tcnt_ref, cfg)
    lane = lax.broadcasted_iota(I32, (cfg.t, w), 1)
    spt = spt_ref[...]
    gt = gt_ref[...]

    def one_pass(j, y):
        hi, lo = [], []
        for e in range(cfg.e):
            a, owned = _window(starts[e], j, cfg)
            ow_scr[e * w:(e + 1) * w, :] = o_ref[e, pl.ds(a, w), :]
            rel = spt[:, e:e + 1] - (a - base)
            hit = jnp.logical_and(rel == lane, lane >= owned)
            g = jnp.where(hit, gt[:, e:e + 1], 0.0)
            g_hi = g.astype(BF16)
            hi.append(g_hi)
            lo.append((g - g_hi.astype(F32)).astype(BF16))
        ow = ow_scr[...]
        return (y + jnp.dot(jnp.concatenate(hi, axis=1), ow, preferred_element_type=F32)
                + jnp.dot(jnp.concatenate(lo, axis=1), ow, preferred_element_type=F32))

    y = lax.fori_loop(0, npass, one_pass, jnp.zeros((cfg.t, d), F32))
    x2 = x1_ref[...] + mod_ref[0][:, 5 * d:6 * d] * y
    if final:
        x2 = (x2 * lax.rsqrt(jnp.mean(x2 * x2, axis=-1, keepdims=True) + EPS)) * fg_ref[...]
    out_ref[...] = x2


def _combine(toff, tcnt, x1, mod, selpos_t, gate_t, o, fg, cfg, final):
    t, d, e = cfg.t, cfg.d, cfg.e
    return pl.pallas_call(
        functools.partial(_combine_kernel, cfg=cfg, final=final),
        grid_spec=pltpu.PrefetchScalarGridSpec(
            num_scalar_prefetch=2,
            grid=(cfg.nt,),
            in_specs=[pl.BlockSpec((t, d), lambda i, *_: (i, 0)),
                      pl.BlockSpec((1, 1, N_MOD * d), lambda i, *_: (_tab_row(i, cfg), 0, 0)),
                      pl.BlockSpec((t, e), lambda i, *_: (i, 0)),
                      pl.BlockSpec((t, e), lambda i, *_: (i, 0)),
                      pl.BlockSpec((e, cfg.sblk, d), lambda i, *_: (0, _slot_block(i, cfg), 0)),
                      pl.BlockSpec((1, d), lambda i, *_: (0, 0))],
            out_specs=pl.BlockSpec((t, d), lambda i, *_: (i, 0)),
            scratch_shapes=[pltpu.VMEM((e * cfg.w, d), BF16)]),
        out_shape=jax.ShapeDtypeStruct(x1.shape, F32),
        compiler_params=_cparams(),
        name="combine",
    )(toff, tcnt, x1, mod, selpos_t, gate_t, o, fg)


def _block_diag_gates(wr, wi, cfg):
    hpb = cfg.bd // cfg.dh
    nblk = cfg.r // cfg.bd
    eye = jnp.eye(hpb, dtype=wr.dtype)

    def blocks(w):
        w4 = w.reshape(nblk, hpb, cfg.dh, cfg.dh)
        return jnp.einsum('jhde,hk->jhdke', w4, eye).reshape(nblk, cfg.bd, cfg.bd)

    return jnp.concatenate([blocks(wr), blocks(wi)], axis=2).astype(BF16)


def _tile_tables(offs_ctx, offs_lat, cfg):
    ctx = offs_ctx[:, :, 0]
    lat = jnp.swapaxes(offs_lat[:, :, :cfg.tpr], 1, 2).reshape(-1, cfg.e)
    return jnp.concatenate([ctx, lat], axis=0).reshape(-1)


def _forward(cfg, x_prompt, x_sample, state_lru, c, c_ctx, norm1_g, norm2_g, final_g, w_mod, b_mod, w_in,
             pool_w, pool_scale, conv_w, conv_b, lru_wr, lru_br, lru_wi, lru_bi, lru_lambda,
             w_br_pool, w_br_lru, w_out, router_w, exp_w1, exp_w3, exp_w2):
    d, r, p = cfg.d, cfg.r, cfg.p
    depth = w_in.shape[0]
    assert cfg.n_lat % cfg.t == 0 and cfg.t % cfg.grid_w == 0 and cfg.n_ctx_req % cfg.rpb == 0
    assert (cfg.n_ctx_req * cfg.n_ctx) % cfg.n_lat == 0 and cfg.w % BF16_ROWS == 0

    n_ctx_tok = cfg.n_ctx_req * cfg.n_ctx
    x = jnp.concatenate([x_prompt.reshape(n_ctx_tok, d), x_sample.reshape(-1, d)], axis=0)

    cvec = jnp.zeros((cfg.n_tab, d), F32).at[0].set(c_ctx).at[1:1 + cfg.n_lat_req].set(c)
    mod = _modulation(cvec, w_mod, b_mod, cfg)

    states = []
    for l in range(depth):
        mod_l = mod[l].reshape(cfg.n_tab, 1, N_MOD * d)
        row = lambda a: a.reshape(1, -1)
        wp = w_in[l][:, :p].astype(BF16)
        wv = w_in[l][:, p:p + r].astype(BF16)
        wg = w_in[l][:, p + r:].astype(BF16)
        pv, v, gl = _inproj(x, mod_l, row(norm1_g[l]), wp, wv, wg, cfg)

        def h0(direction):
            tab = jnp.zeros((cfg.n_tab, 1, r), F32)
            return tab.at[1:1 + cfg.n_lat_req, 0].set(state_lru[:, l, direction].astype(F32))

        wbd = [_block_diag_gates(lru_wr[l, z], lru_wi[l, z], cfg) for z in range(2)]
        vc, hb, hb_last = _scan_bwd(v, conv_w[l], row(conv_b[l]), wbd[1], row(lru_br[l, 1]), row(lru_bi[l, 1]),
                                    row(lru_lambda[l, 1]), h0(1), cfg)
        x1, hn2, logits, hf_last = _scan_fwd(
            x, pv, vc, gl, hb, mod_l, h0(0), wbd[0], row(lru_br[l, 0]), row(lru_bi[l, 0]), row(lru_lambda[l, 0]),
            pool_w[l].astype(BF16), row(pool_scale[l]), w_br_pool[l].astype(BF16), w_br_lru[l].astype(BF16),
            w_out[l].astype(BF16), row(norm2_g[l]), router_w[l], cfg)
        states.append(jnp.stack([hf_last[:, 0], hb_last[:, 0]], axis=1))

        logits_t = logits.T
        sp_c, g_c, off_c, cnt_c = _route(logits_t, cfg.n_ctx_req, cfg.n_ctx, 0, cfg.c_ctx, cfg)
        sp_l, g_l, off_l, cnt_l = _route(logits_t, cfg.n_lat_req, cfg.n_lat, n_ctx_tok // cfg.n_lat,
                                         cfg.c_lat, cfg)
        selpos = jnp.concatenate([sp_c, sp_l], axis=1)
        gate = jnp.concatenate([g_c, g_l], axis=1)
        toff = _tile_tables(off_c, off_l, cfg)
        tcnt = _tile_tables(cnt_c, cnt_l, cfg)

        xs = _dispatch(toff, tcnt, hn2, selpos, cfg)
        o = _ffn(xs, exp_w1[l].astype(BF16), exp_w3[l].astype(BF16), exp_w2[l].astype(BF16), cfg)
        x = _combine(toff, tcnt, x1, mod_l, selpos.T, gate.T, o, row(final_g), cfg, l == depth - 1)

    y_prompt = x[:n_ctx_tok].reshape(x_prompt.shape)
    y_sample = x[n_ctx_tok:].reshape(x_sample.shape)
    new_state = jnp.stack(states, axis=1).astype(x_prompt.dtype)
    return y_prompt, y_sample, new_state


def kernel(x_prompt, x_sample, state_lru, c, c_ctx, norm1_g, norm2_g, final_g, w_mod, b_mod, w_in, pool_w,
           pool_scale, conv_w, conv_b, lru_wr, lru_br, lru_wi, lru_bi, lru_lambda, w_br_pool, w_br_lru, w_out,
           router_w, exp_w1, exp_w3, exp_w2):
    cfg = Cfg(d=x_prompt.shape[2], gw=pool_w.shape[2], r=lru_lambda.shape[2], dh=lru_wr.shape[3],
              e=router_w.shape[2], f=exp_w1.shape[3], n_ctx_req=x_prompt.shape[0], n_ctx=x_prompt.shape[1],
              n_lat_req=x_sample.shape[0], n_lat=x_sample.shape[1], grid_w=GRID_W)
    return _forward(cfg, x_prompt, x_sample, state_lru, c, c_ctx, norm1_g, norm2_g, final_g, w_mod, b_mod, w_in,
                    pool_w, pool_scale, conv_w, conv_b, lru_wr, lru_br, lru_wi, lru_bi, lru_lambda,
                    w_br_pool, w_br_lru, w_out, router_w, exp_w1, exp_w3, exp_w2)
```

```python
import functools
from typing import NamedTuple

import jax
import jax.numpy as jnp
from jax import lax
from jax.experimental import pallas as pl
from jax.experimental.pallas import tpu as pltpu

F32 = jnp.float32
BF16 = jnp.bfloat16
I32 = jnp.int32
HIGHEST = lax.Precision.HIGHEST

EPS = 1e-6
LRU_C = 8.0
CONV_WIDTH = 4
POOL_HALF_WINDOWS = (1, 2, 4, 8)
N_MOD = 6
CAPACITY_FACTOR = 2
GRID_W = 64
NOT_SELECTED = -(1 << 20)
SUBLANES = 8
BF16_ROWS = 16
LANES = 128
MXU_DIM = 256
VMEM_LIMIT = 56 * 1024 * 1024


class Cfg(NamedTuple):
    d: int
    gw: int
    r: int
    dh: int
    e: int
    f: int
    n_ctx_req: int
    n_ctx: int
    n_lat_req: int
    n_lat: int
    grid_w: int

    @property
    def t(self): return self.n_ctx
    @property
    def p(self): return self.gw * len(POOL_HALF_WINDOWS)
    @property
    def ntc(self): return self.n_ctx_req
    @property
    def tpr(self): return self.n_lat // self.t
    @property
    def nt(self): return self.ntc + self.n_lat_req * self.tpr
    @property
    def n_tok(self): return self.nt * self.t
    @property
    def c_ctx(self): return CAPACITY_FACTOR * self.n_ctx // self.e
    @property
    def c_lat(self): return CAPACITY_FACTOR * self.n_lat // self.e
    @property
    def sblk(self): return self.c_lat
    @property
    def rpb(self): return self.c_lat // self.c_ctx
    @property
    def n_sblk(self): return self.n_ctx_req // self.rpb + self.n_lat_req
    @property
    def w(self): return min(64, self.sblk)
    @property
    def bd(self): return min(MXU_DIM, self.r)
    @property
    def n_tab(self): return 16


def _cparams():
    return pltpu.CompilerParams(dimension_semantics=("arbitrary",), vmem_limit_bytes=VMEM_LIMIT)


def _tab_row(i, cfg):
    return jnp.where(i < cfg.ntc, 0, 1 + (i - cfg.ntc) // cfg.tpr)


def _sigmoid(x):
    return 0.5 * jnp.tanh(0.5 * x) + 0.5


def _lat_pos(i, cfg):
    j = jnp.maximum(i - cfg.ntc, 0)
    return j // cfg.tpr, j % cfg.tpr


def _x_specs(cfg, tile_of):
    def ctx_map(s, *_):
        return (jnp.minimum(tile_of(s), cfg.ntc - 1), 0, 0)

    def lat_map(s, *_):
        b, k = _lat_pos(tile_of(s), cfg)
        return (b, k, 0)

    return [pl.BlockSpec((1, cfg.t, cfg.d), ctx_map), pl.BlockSpec((1, cfg.t, cfg.d), lat_map)]


def _norm_mod(x, g, scale, shift):
    y = x * lax.rsqrt(jnp.mean(x * x, axis=-1, keepdims=True) + EPS)
    return (y * g) * (1.0 + scale) + shift


def _mod_kernel(c_ref, w_ref, b_ref, o_ref):
    c = c_ref[...]
    s = c * _sigmoid(c)
    o_ref[0] = jnp.dot(s, w_ref[0], precision=HIGHEST, preferred_element_type=F32) + b_ref[0]


def _modulation(cvec, w_mod, b_mod, cfg):
    depth, d, n6 = w_mod.shape
    nc = n6 // 4
    return pl.pallas_call(
        _mod_kernel,
        grid=(depth, n6 // nc),
        in_specs=[pl.BlockSpec((cfg.n_tab, d), lambda l, j: (0, 0)),
                  pl.BlockSpec((1, d, nc), lambda l, j: (l, 0, j)),
                  pl.BlockSpec((1, 1, nc), lambda l, j: (l, 0, j))],
        out_specs=pl.BlockSpec((1, cfg.n_tab, nc), lambda l, j: (l, 0, j)),
        out_shape=jax.ShapeDtypeStruct((depth, cfg.n_tab, n6), F32),
        compiler_params=pltpu.CompilerParams(dimension_semantics=("arbitrary", "arbitrary"),
                                             vmem_limit_bytes=VMEM_LIMIT),
        name="modulation",
    )(cvec, w_mod, b_mod.reshape(depth, 1, n6))


def _conv(v, prev, nxt, w, b):
    t = v.shape[0]
    row = lax.broadcasted_iota(I32, v.shape, 0)
    vm1 = jnp.where(row == 0, prev[7:8], pltpu.roll(v, 1, 0))
    vm2 = jnp.where(row == 0, prev[6:7], jnp.where(row == 1, prev[7:8], pltpu.roll(v, 2, 0)))
    vp1 = jnp.where(row == t - 1, nxt[0:1], pltpu.roll(v, t - 1, 0))
    return b + vm2 * w[0:1] + vm1 * w[1:2] + v * w[2:3] + vp1 * w[3:4]


def _gates(vc, wbd_ref, br, bi, lam, cfg):
    bd = cfg.bd
    vcb = vc.astype(BF16)
    zr, zi = [], []
    for j in range(cfg.r // bd):
        z = jnp.dot(vcb[:, j * bd:(j + 1) * bd], wbd_ref[j], preferred_element_type=F32)
        zr.append(z[:, :bd])
        zi.append(z[:, bd:])
    rg = _sigmoid(jnp.concatenate(zr, axis=1) + br)
    ig = _sigmoid(jnp.concatenate(zi, axis=1) + bi)
    nl = -lam
    softplus = jnp.maximum(nl, 0.0) + jnp.log1p(jnp.exp(-jnp.abs(nl)))
    log_a = -LRU_C * rg * softplus
    a = jnp.exp(log_a)
    u = jnp.sqrt(-jnp.tanh(log_a) * (a * a + 1.0)) * ig * vc
    return a, u


def _scan(a_scr, u_scr, out_ref, h_in, reverse):
    groups = a_scr.shape[0] // SUBLANES
    row = lax.broadcasted_iota(I32, (SUBLANES, a_scr.shape[1]), 0)
    steps = [((row < SUBLANES - s) if reverse else (row >= s), SUBLANES - s if reverse else s)
             for s in (1, 2, 4)]

    def body(j, h):
        g = groups - 1 - j if reverse else j
        r0 = pl.multiple_of(g * SUBLANES, SUBLANES)
        a = a_scr[pl.ds(r0, SUBLANES), :]
        u = u_scr[pl.ds(r0, SUBLANES), :]
        for ok, shift in steps:
            u = u + a * jnp.where(ok, pltpu.roll(u, shift, 0), 0.0)
            a = a * jnp.where(ok, pltpu.roll(a, shift, 0), 1.0)
        hg = a * h + u
        out_ref[pl.ds(r0, SUBLANES), :] = hg
        return hg[0:1] if reverse else hg[SUBLANES - 1:SUBLANES]

    return lax.fori_loop(0, groups, body, h_in, unroll=4)


def _tile_flags(i, cfg):
    is_ctx = i < cfg.ntc
    k = jnp.maximum(i - cfg.ntc, 0) % cfg.tpr
    first = jnp.logical_or(is_ctx, k == 0)
    last = jnp.logical_or(is_ctx, k == cfg.tpr - 1)
    return is_ctx, first, last


def _scan_bwd_kernel(xc_ref, xl_ref, xprev_ref, xnext_ref, mod_ref, g1_ref, wv_ref, cw_ref, cb_ref, wbd_ref,
                     br_ref, bi_ref, lam_ref, h0_ref, vc_ref, hb_ref, hlast_ref, a_scr, u_scr, carry_scr,
                     *, cfg):
    i = cfg.nt - 1 - pl.program_id(0)
    t, d = cfg.t, cfg.d
    is_ctx, first, last = _tile_flags(i, cfg)
    x_ext = jnp.concatenate([xprev_ref[0], jnp.where(is_ctx, xc_ref[0], xl_ref[0]), xnext_ref[0]], axis=0)
    m = mod_ref[0]
    hn = _norm_mod(x_ext, g1_ref[...], m[:, d:2 * d], m[:, 0:d]).astype(BF16)
    v_ext = jnp.dot(hn, wv_ref[...], preferred_element_type=F32)
    prev = jnp.where(first, 0.0, v_ext[0:SUBLANES])
    nxt = jnp.where(last, 0.0, v_ext[t + SUBLANES:t + 2 * SUBLANES])
    vc = _conv(v_ext[SUBLANES:t + SUBLANES], prev, nxt, cw_ref[...], cb_ref[...])
    vc_ref[...] = vc
    a, u = _gates(vc, wbd_ref, br_ref[...], bi_ref[...], lam_ref[...], cfg)
    a_scr[...] = a
    u_scr[...] = u

    @pl.when(last)
    def _():
        carry_scr[...] = h0_ref[0]

    h = _scan(a_scr, u_scr, hb_ref, carry_scr[...], True)
    carry_scr[...] = h

    @pl.when(is_ctx)
    def _():
        hlast_ref[0] = h


def _scan_bwd(xc, xl, mod, g1, wv, cw, cb, wbd, br, bi, lam, h0, cfg):
    t, d, r, nt = cfg.t, cfg.d, cfg.r, cfg.nt
    n = cfg.n_tok
    tb = t // SUBLANES
    rev = lambda s: nt - 1 - s
    full = lambda a: pl.BlockSpec(a.shape, lambda s: (0,) * a.ndim)
    tile = pl.BlockSpec((t, r), lambda s: (rev(s), 0))
    tab = lambda width: pl.BlockSpec((1, 1, width), lambda s: (_tab_row(rev(s), cfg), 0, 0))

    def prev_map(s):
        b, k = _lat_pos(rev(s), cfg)
        return (b, jnp.maximum(k * tb - 1, 0), 0)

    def next_map(s):
        b, k = _lat_pos(rev(s), cfg)
        return (b, jnp.minimum((k + 1) * tb, cfg.n_lat // SUBLANES - 1), 0)

    return pl.pallas_call(
        functools.partial(_scan_bwd_kernel, cfg=cfg),
        grid=(nt,),
        in_specs=_x_specs(cfg, rev) + [
            pl.BlockSpec((1, SUBLANES, d), prev_map), pl.BlockSpec((1, SUBLANES, d), next_map),
            tab(N_MOD * d), full(g1), full(wv), full(cw), full(cb), full(wbd), full(br), full(bi), full(lam),
            tab(r)],
        out_specs=[tile, tile,
                   pl.BlockSpec((1, 1, r), lambda s: (jnp.minimum(rev(s), cfg.ntc - 1), 0, 0))],
        out_shape=[jax.ShapeDtypeStruct((n, r), F32),
                   jax.ShapeDtypeStruct((n, r), F32),
                   jax.ShapeDtypeStruct((cfg.n_ctx_req, 1, r), F32)],
        scratch_shapes=[pltpu.VMEM((t, r), F32), pltpu.VMEM((t, r), F32), pltpu.VMEM((1, r), F32)],
        compiler_params=_cparams(),
        name="scan_bwd",
    )(xc, xl, xl, xl, mod, g1, wv, cw, cb, wbd, br, bi, lam, h0)


def _pool(pv, is_ctx, cfg):
    t = pv.shape[0]
    gw = cfg.gw
    row = lax.broadcasted_iota(I32, (t, gw), 0)
    pos = jnp.where(is_ctx, row % cfg.n_ctx, row % cfg.grid_w)
    length = jnp.where(is_ctx, cfg.n_ctx, cfg.grid_w)

    def shifted(x, dlt):
        ok = jnp.logical_and(pos + dlt >= 0, pos + dlt < length)
        return jnp.where(ok, pltpu.roll(x, (-dlt) % t, 0), 0.0)

    outs = []
    for g, m in enumerate(POOL_HALF_WINDOWS):
        x = pv[:, g * gw:(g + 1) * gw]
        back, fwd, k = x, x, 1
        while k < m:
            back = back + shifted(back, -k)
            fwd = fwd + shifted(fwd, k)
            k *= 2
        s = shifted(back, -1) + fwd
        cnt = (jnp.minimum(pos + m, length) - jnp.maximum(pos - m, 0)).astype(F32)
        outs.append(s / cnt - x)
    return outs


def _scan_fwd_kernel(xc_ref, xl_ref, vc_ref, hb_ref, mod_ref, h0_ref, g1_ref, wp_ref, wg_ref, wbd_ref, br_ref,
                     bi_ref, lam_ref, pw_ref, ps_ref, wbp_ref, wbl_ref, wo_ref, g2_ref, rwa_ref, rwb_ref,
                     x1_ref, hn2_ref, lg_ref, hlast_ref, a_scr, u_scr, hf_scr, carry_scr, *, cfg):
    i = pl.program_id(0)
    d, gw = cfg.d, cfg.gw
    is_ctx, first, _ = _tile_flags(i, cfg)
    a, u = _gates(vc_ref[...], wbd_ref, br_ref[...], bi_ref[...], lam_ref[...], cfg)
    a_scr[...] = a
    u_scr[...] = u

    @pl.when(first)
    def _():
        carry_scr[...] = h0_ref[0]

    h = _scan(a_scr, u_scr, hf_scr, carry_scr[...], False)
    carry_scr[...] = h

    @pl.when(is_ctx)
    def _():
        hlast_ref[0] = h

    y_lru = (hf_scr[...] + hb_ref[...]).astype(BF16)

    m = mod_ref[0]
    x = jnp.where(is_ctx, xc_ref[0], xl_ref[0])
    hn = _norm_mod(x, g1_ref[...], m[:, d:2 * d], m[:, 0:d]).astype(BF16)
    pooled = _pool(jnp.dot(hn, wp_ref[...], preferred_element_type=F32), is_ctx, cfg)
    ps = ps_ref[...]
    y_pool = jnp.concatenate(
        [jnp.dot(pooled[g].astype(BF16), pw_ref[g], preferred_element_type=F32) * ps[:, g * gw:(g + 1) * gw]
         for g in range(len(POOL_HALF_WINDOWS))], axis=1).astype(BF16)

    merged = (_sigmoid(jnp.dot(hn, wg_ref[:, :d], preferred_element_type=F32))
              * jnp.dot(y_pool, wbp_ref[...], preferred_element_type=F32)
              + _sigmoid(jnp.dot(hn, wg_ref[:, d:], preferred_element_type=F32))
              * jnp.dot(y_lru, wbl_ref[...], preferred_element_type=F32))
    mix = jnp.dot(merged.astype(BF16), wo_ref[...], preferred_element_type=F32)
    x1 = x + m[:, 2 * d:3 * d] * mix
    x1_ref[...] = x1
    hn2 = _norm_mod(x1, g2_ref[...], m[:, 4 * d:5 * d], m[:, 3 * d:4 * d])
    hi = hn2.astype(BF16)
    hn2_ref[...] = hi
    lo = (hn2 - hi.astype(F32)).astype(BF16)
    lg_ref[...] = (jnp.dot(hi, rwa_ref[...], preferred_element_type=F32)
                   + jnp.dot(hi, rwb_ref[...], preferred_element_type=F32)
                   + jnp.dot(lo, rwa_ref[...], preferred_element_type=F32))


def _scan_fwd(xc, xl, vc, hb, mod, h0, g1, wp, wg, wbd, br, bi, lam, pw, ps, wbp, wbl, wo, g2, rwa, rwb, cfg):
    t, d, r = cfg.t, cfg.d, cfg.r
    n = cfg.n_tok
    tile = lambda width: pl.BlockSpec((t, width), lambda i: (i, 0))
    full = lambda a: pl.BlockSpec(a.shape, lambda i: (0,) * a.ndim)
    tab = lambda width: pl.BlockSpec((1, 1, width), lambda i: (_tab_row(i, cfg), 0, 0))
    return pl.pallas_call(
        functools.partial(_scan_fwd_kernel, cfg=cfg),
        grid=(cfg.nt,),
        in_specs=_x_specs(cfg, lambda i: i) + [
            tile(r), tile(r), tab(N_MOD * d), tab(r), full(g1), full(wp), full(wg),
            full(wbd), full(br), full(bi), full(lam), full(pw), full(ps), full(wbp), full(wbl),
            full(wo), full(g2), full(rwa), full(rwb)],
        out_specs=[tile(d), tile(d), tile(cfg.e),
                   pl.BlockSpec((1, 1, r), lambda i: (jnp.minimum(i, cfg.ntc - 1), 0, 0))],
        out_shape=[jax.ShapeDtypeStruct((n, d), F32),
                   jax.ShapeDtypeStruct((n, d), BF16),
                   jax.ShapeDtypeStruct((n, cfg.e), F32),
                   jax.ShapeDtypeStruct((cfg.n_ctx_req, 1, r), F32)],
        scratch_shapes=[pltpu.VMEM((t, r), F32), pltpu.VMEM((t, r), F32), pltpu.VMEM((t, r), F32),
                        pltpu.VMEM((1, r), F32)],
        compiler_params=_cparams(),
        name="scan_fwd",
    )(xc, xl, vc, hb, mod, h0, g1, wp, wg, wbd, br, bi, lam, pw, ps, wbp, wbl, wo, g2, rwa, rwb)


def _route_kernel(l_ref, selpos_ref, gate_ref, offs_ref, cnts_ref, *, cap, t):
    lg = l_ref[...]
    e, n = lg.shape
    nch = n // t
    ex = jnp.exp(lg - jnp.max(lg, axis=0, keepdims=True))
    aff = ex / jnp.sum(ex, axis=0, keepdims=True)
    gate_ref[...] = aff
    keys = pltpu.bitcast(aff, I32)

    def count(mask):
        return jnp.sum(mask.astype(F32), axis=1, keepdims=True)

    thr = jnp.zeros((e, 1), I32)
    for bit in range(30, -1, -1):
        cand = thr | (1 << bit)
        thr = jnp.where(count(keys >= cand) >= cap, cand, thr)
    gt = keys > thr
    eq = keys == thr
    need = cap - count(gt)

    tri = (lax.broadcasted_iota(I32, (t, t), 0) < lax.broadcasted_iota(I32, (t, t), 1)).astype(BF16)
    tok = lax.broadcasted_iota(I32, (n, LANES), 0)
    chunk = lax.broadcasted_iota(I32, (n, LANES), 1)
    before = (tok < chunk * t).astype(BF16)
    inside = (tok // t == chunk).astype(BF16)

    def prefix(mask):
        mb = mask.astype(BF16)
        offs = jnp.dot(mb, before, preferred_element_type=F32)
        pre = [jnp.dot(mb[:, k * t:(k + 1) * t], tri, preferred_element_type=F32) + offs[:, k:k + 1]
               for k in range(nch)]
        return jnp.concatenate(pre, axis=1) if nch > 1 else pre[0], offs, mb

    eq_rank, _, _ = prefix(eq)
    sel = jnp.logical_or(gt, jnp.logical_and(eq, eq_rank < need))
    pos, offs, selb = prefix(sel)
    selpos_ref[...] = jnp.where(sel, pos.astype(I32), NOT_SELECTED)
    offs_ref[0] = offs.astype(I32)
    cnts_ref[0] = jnp.dot(selb, inside, preferred_element_type=F32).astype(I32)


def _route(logits_t, n_req, n, col0, cap, cfg):
    e = cfg.e
    return pl.pallas_call(
        functools.partial(_route_kernel, cap=cap, t=cfg.t),
        grid=(n_req,),
        in_specs=[pl.BlockSpec((e, n), lambda b: (0, col0 + b))],
        out_specs=[pl.BlockSpec((e, n), lambda b: (0, b)),
                   pl.BlockSpec((e, n), lambda b: (0, b)),
                   pl.BlockSpec((1, e, LANES), lambda b: (b, 0, 0)),
                   pl.BlockSpec((1, e, LANES), lambda b: (b, 0, 0))],
        out_shape=[jax.ShapeDtypeStruct((e, n_req * n), I32),
                   jax.ShapeDtypeStruct((e, n_req * n), F32),
                   jax.ShapeDtypeStruct((n_req, e, LANES), I32),
                   jax.ShapeDtypeStruct((n_req, e, LANES), I32)],
        compiler_params=_cparams(),
        name="route",
    )(logits_t)


def _slot_block(i, cfg):
    return jnp.where(i < cfg.ntc, i // cfg.rpb, cfg.ntc // cfg.rpb + (i - cfg.ntc) // cfg.tpr)


def _windows(i, toff_ref, tcnt_ref, cfg):
    is_ctx = i < cfg.ntc
    base = jnp.where(is_ctx, (i % cfg.rpb) * cfg.c_ctx, 0)
    starts, npass = [], 0
    for e in range(cfg.e):
        off = base + toff_ref[i * cfg.e + e]
        start = (off // BF16_ROWS) * BF16_ROWS
        starts.append(start)
        npass = jnp.maximum(npass, (off + tcnt_ref[i * cfg.e + e] - start + cfg.w - 1) // cfg.w)
    return base, starts, npass


def _window(start, j, cfg):
    want = start + j * cfg.w
    a = pl.multiple_of(jnp.minimum(want, cfg.sblk - cfg.w), BF16_ROWS)
    return a, want - a


def _dispatch_kernel(toff_ref, tcnt_ref, x_ref, sp_ref, xs_ref, p_scr, *, cfg):
    i = pl.program_id(0)
    w = cfg.w
    is_ctx = i < cfg.ntc
    new_block = jnp.where(is_ctx, i % cfg.rpb == 0, jnp.maximum(i - cfg.ntc, 0) % cfg.tpr == 0)

    @pl.when(new_block)
    def _():
        xs_ref[...] = jnp.zeros(xs_ref.shape, xs_ref.dtype)

    base, starts, npass = _windows(i, toff_ref, tcnt_ref, cfg)
    slot = lax.broadcasted_iota(I32, (w, cfg.t), 0)

    def one_pass(j, carry):
        firsts = []
        for e in range(cfg.e):
            a, owned = _window(starts[e], j, cfg)
            firsts.append(a)
            rel = sp_ref[e:e + 1, :] - (a - base)
            p_scr[e * w:(e + 1) * w, :] = jnp.logical_and(rel == slot, slot >= owned).astype(BF16)
        rows = jnp.dot(p_scr[...], x_ref[...], preferred_element_type=F32)
        for e in range(cfg.e):
            cur = xs_ref[e, pl.ds(firsts[e], w), :].astype(F32)
            xs_ref[e, pl.ds(firsts[e], w), :] = (cur + rows[e * w:(e + 1) * w]).astype(BF16)
        return carry

    lax.fori_loop(0, npass, one_pass, 0)


def _dispatch(toff, tcnt, hn2, selpos, cfg):
    t, d, e = cfg.t, cfg.d, cfg.e
    return pl.pallas_call(
        functools.partial(_dispatch_kernel, cfg=cfg),
        grid_spec=pltpu.PrefetchScalarGridSpec(
            num_scalar_prefetch=2,
            grid=(cfg.nt,),
            in_specs=[pl.BlockSpec((t, d), lambda i, *_: (i, 0)),
                      pl.BlockSpec((e, t), lambda i, *_: (0, i))],
            out_specs=pl.BlockSpec((e, cfg.sblk, d), lambda i, *_: (0, _slot_block(i, cfg), 0)),
            scratch_shapes=[pltpu.VMEM((e * cfg.w, t), BF16)]),
        out_shape=jax.ShapeDtypeStruct((e, cfg.n_sblk * cfg.sblk, d), BF16),
        compiler_params=_cparams(),
        name="dispatch",
    )(toff, tcnt, hn2, selpos)


def _ffn_kernel(x_ref, w1_ref, w3_ref, w2_ref, o_ref, *, fc):
    x = x_ref[0]
    acc = jnp.zeros((x.shape[0], o_ref.shape[2]), F32)
    for c in range(w1_ref.shape[2] // fc):
        sl = slice(c * fc, (c + 1) * fc)
        h1 = jnp.dot(x, w1_ref[0, :, sl], preferred_element_type=F32)
        h3 = jnp.dot(x, w3_ref[0, :, sl], preferred_element_type=F32)
        hid = (h1 * _sigmoid(h1)) * h3
        acc = acc + jnp.dot(hid.astype(BF16), w2_ref[0, sl, :], preferred_element_type=F32)
    o_ref[0] = acc.astype(BF16)


def _ffn(xs, w1, w3, w2, cfg):
    e, s, d = xs.shape
    f = w1.shape[2]
    tm = cfg.sblk
    return pl.pallas_call(
        functools.partial(_ffn_kernel, fc=min(512, f)),
        grid=(e, s // tm),
        in_specs=[pl.BlockSpec((1, tm, d), lambda k, m: (k, m, 0)),
                  pl.BlockSpec((1, d, f), lambda k, m: (k, 0, 0)),
                  pl.BlockSpec((1, d, f), lambda k, m: (k, 0, 0)),
                  pl.BlockSpec((1, f, d), lambda k, m: (k, 0, 0))],
        out_specs=pl.BlockSpec((1, tm, d), lambda k, m: (k, m, 0)),
        out_shape=jax.ShapeDtypeStruct((e, s, d), BF16),
        compiler_params=pltpu.CompilerParams(dimension_semantics=("arbitrary", "arbitrary"),
                                             vmem_limit_bytes=VMEM_LIMIT),
        name="ffn",
    )(xs, w1, w3, w2)


def _combine_kernel(toff_ref, tcnt_ref, x1_ref, mod_ref, spt_ref, gt_ref, o_ref, fg_ref, oc_ref, ol_ref, ow_scr,
                    *, cfg, final):
    i = pl.program_id(0)
    w, d = cfg.w, cfg.d
    base, starts, npass = _windows(i, toff_ref, tcnt_ref, cfg)
    lane = lax.broadcasted_iota(I32, (cfg.t, w), 1)
    spt = spt_ref[...]
    gt = gt_ref[...]

    def one_pass(j, y):
        hi, lo = [], []
        for e in range(cfg.e):
            a, owned = _window(starts[e], j, cfg)
            ow_scr[e * w:(e + 1) * w, :] = o_ref[e, pl.ds(a, w), :]
            rel = spt[:, e:e + 1] - (a - base)
            hit = jnp.logical_and(rel == lane, lane >= owned)
            g = jnp.where(hit, gt[:, e:e + 1], 0.0)
            g_hi = g.astype(BF16)
            hi.append(g_hi)
            lo.append((g - g_hi.astype(F32)).astype(BF16))
        ow = ow_scr[...]
        return (y + jnp.dot(jnp.concatenate(hi, axis=1), ow, preferred_element_type=F32)
                + jnp.dot(jnp.concatenate(lo, axis=1), ow, preferred_element_type=F32))

    y = lax.fori_loop(0, npass, one_pass, jnp.zeros((cfg.t, d), F32))
    x2 = x1_ref[...] + mod_ref[0][:, 5 * d:6 * d] * y
    if final:
        x2 = (x2 * lax.rsqrt(jnp.mean(x2 * x2, axis=-1, keepdims=True) + EPS)) * fg_ref[...]
    is_ctx = i < cfg.ntc

    @pl.when(is_ctx)
    def _():
        oc_ref[0] = x2

    @pl.when(jnp.logical_not(is_ctx))
    def _():
        ol_ref[0] = x2


def _combine(toff, tcnt, x1, mod, selpos_t, gate_t, o, fg, cfg, final):
    t, d, e = cfg.t, cfg.d, cfg.e
    return pl.pallas_call(
        functools.partial(_combine_kernel, cfg=cfg, final=final),
        grid_spec=pltpu.PrefetchScalarGridSpec(
            num_scalar_prefetch=2,
            grid=(cfg.nt,),
            in_specs=[pl.BlockSpec((t, d), lambda i, *_: (i, 0)),
                      pl.BlockSpec((1, 1, N_MOD * d), lambda i, *_: (_tab_row(i, cfg), 0, 0)),
                      pl.BlockSpec((t, e), lambda i, *_: (i, 0)),
                      pl.BlockSpec((t, e), lambda i, *_: (i, 0)),
                      pl.BlockSpec((e, cfg.sblk, d), lambda i, *_: (0, _slot_block(i, cfg), 0)),
                      pl.BlockSpec((1, d), lambda i, *_: (0, 0))],
            out_specs=_x_specs(cfg, lambda i: i),
            scratch_shapes=[pltpu.VMEM((e * cfg.w, d), BF16)]),
        out_shape=[jax.ShapeDtypeStruct((cfg.n_ctx_req, cfg.n_ctx, d), F32),
                   jax.ShapeDtypeStruct((cfg.n_lat_req, cfg.n_lat, d), F32)],
        compiler_params=_cparams(),
        name="combine",
    )(toff, tcnt, x1, mod, selpos_t, gate_t, o, fg)


def _block_diag_gates(wr, wi, cfg):
    hpb = cfg.bd // cfg.dh
    nblk = cfg.r // cfg.bd
    eye = jnp.eye(hpb, dtype=wr.dtype)

    def blocks(w):
        w4 = w.reshape(nblk, hpb, cfg.dh, cfg.dh)
        return jnp.einsum('jhde,hk->jhdke', w4, eye).reshape(nblk, cfg.bd, cfg.bd)

    return jnp.concatenate([blocks(wr), blocks(wi)], axis=2).astype(BF16)


def _tile_tables(offs_ctx, offs_lat, cfg):
    ctx = offs_ctx[:, :, 0]
    lat = jnp.swapaxes(offs_lat[:, :, :cfg.tpr], 1, 2).reshape(-1, cfg.e)
    return jnp.concatenate([ctx, lat], axis=0).reshape(-1)


def _forward(cfg, x_prompt, x_sample, state_lru, c, c_ctx, norm1_g, norm2_g, final_g, w_mod, b_mod, w_in,
             pool_w, pool_scale, conv_w, conv_b, lru_wr, lru_br, lru_wi, lru_bi, lru_lambda,
             w_br_pool, w_br_lru, w_out, router_w, exp_w1, exp_w3, exp_w2):
    d, r, p = cfg.d, cfg.r, cfg.p
    depth = w_in.shape[0]
    assert cfg.n_lat % cfg.t == 0 and cfg.t % cfg.grid_w == 0 and cfg.n_ctx_req % cfg.rpb == 0
    assert (cfg.n_ctx_req * cfg.n_ctx) % cfg.n_lat == 0 and cfg.w % BF16_ROWS == 0

    n_ctx_tok = cfg.n_ctx_req * cfg.n_ctx
    xc, xl = x_prompt, x_sample

    cvec = jnp.zeros((cfg.n_tab, d), F32).at[0].set(c_ctx).at[1:1 + cfg.n_lat_req].set(c)
    mod = _modulation(cvec, w_mod, b_mod, cfg)

    states = []
    for l in range(depth):
        mod_l = mod[l].reshape(cfg.n_tab, 1, N_MOD * d)
        row = lambda a: a.reshape(1, -1)
        wp = w_in[l][:, :p].astype(BF16)
        wv = w_in[l][:, p:p + r].astype(BF16)
        wg = w_in[l][:, p + r:].astype(BF16)
        rwa = router_w[l].astype(BF16)
        rwb = (router_w[l] - rwa.astype(F32)).astype(BF16)

        def h0(direction):
            tab = jnp.zeros((cfg.n_tab, 1, r), F32)
            return tab.at[1:1 + cfg.n_lat_req, 0].set(state_lru[:, l, direction].astype(F32))

        wbd = [_block_diag_gates(lru_wr[l, z], lru_wi[l, z], cfg) for z in range(2)]
        vc, hb, hb_last = _scan_bwd(xc, xl, mod_l, row(norm1_g[l]), wv, conv_w[l], row(conv_b[l]), wbd[1],
                                    row(lru_br[l, 1]), row(lru_bi[l, 1]), row(lru_lambda[l, 1]), h0(1), cfg)
        x1, hn2, logits, hf_last = _scan_fwd(
            xc, xl, vc, hb, mod_l, h0(0), row(norm1_g[l]), wp, wg, wbd[0], row(lru_br[l, 0]), row(lru_bi[l, 0]),
            row(lru_lambda[l, 0]), pool_w[l].astype(BF16), row(pool_scale[l]), w_br_pool[l].astype(BF16),
            w_br_lru[l].astype(BF16), w_out[l].astype(BF16), row(norm2_g[l]), rwa, rwb, cfg)
        states.append(jnp.stack([hf_last[:, 0], hb_last[:, 0]], axis=1))

        logits_t = logits.T
        sp_c, g_c, off_c, cnt_c = _route(logits_t, cfg.n_ctx_req, cfg.n_ctx, 0, cfg.c_ctx, cfg)
        sp_l, g_l, off_l, cnt_l = _route(logits_t, cfg.n_lat_req, cfg.n_lat, n_ctx_tok // cfg.n_lat,
                                         cfg.c_lat, cfg)
        selpos = jnp.concatenate([sp_c, sp_l], axis=1)
        gate = jnp.concatenate([g_c, g_l], axis=1)
        toff = _tile_tables(off_c, off_l, cfg)
        tcnt = _tile_tables(cnt_c, cnt_l, cfg)

        xs = _dispatch(toff, tcnt, hn2, selpos, cfg)
        o = _ffn(xs, exp_w1[l].astype(BF16), exp_w3[l].astype(BF16), exp_w2[l].astype(BF16), cfg)
        xc, xl = _combine(toff, tcnt, x1, mod_l, selpos.T, gate.T, o, row(final_g), cfg, l == depth - 1)

    new_state = jnp.stack(states, axis=1).astype(x_prompt.dtype)
    return xc, xl, new_state


def kernel(x_prompt, x_sample, state_lru, c, c_ctx, norm1_g, norm2_g, final_g, w_mod, b_mod, w_in, pool_w,
           pool_scale, conv_w, conv_b, lru_wr, lru_br, lru_wi, lru_bi, lru_lambda, w_br_pool, w_br_lru, w_out,
           router_w, exp_w1, exp_w3, exp_w2):
    cfg = Cfg(d=x_prompt.shape[2], gw=pool_w.shape[2], r=lru_lambda.shape[2], dh=lru_wr.shape[3],
              e=router_w.shape[2], f=exp_w1.shape[3], n_ctx_req=x_prompt.shape[0], n_ctx=x_prompt.shape[1],
              n_lat_req=x_sample.shape[0], n_lat=x_sample.shape[1], grid_w=GRID_W)
    return _forward(cfg, x_prompt, x_sample, state_lru, c, c_ctx, norm1_g, norm2_g, final_g, w_mod, b_mod, w_in,
                    pool_w, pool_scale, conv_w, conv_b, lru_wr, lru_br, lru_wi, lru_bi, lru_lambda,
                    w_br_pool, w_br_lru, w_out, router_w, exp_w1, exp_w3, exp_w2)
```

```python
import functools
from typing import NamedTuple

import jax
import jax.numpy as jnp
from jax import lax
from jax.experimental import pallas as pl
from jax.experimental.pallas import tpu as pltpu

F32 = jnp.float32
BF16 = jnp.bfloat16
I32 = jnp.int32
HIGHEST = lax.Precision.HIGHEST

EPS = 1e-6
LRU_C = 8.0
CONV_WIDTH = 4
POOL_HALF_WINDOWS = (1, 2, 4, 8)
N_MOD = 6
CAPACITY_FACTOR = 2
GRID_W = 64
NOT_SELECTED = -(1 << 20)
SUBLANES = 8
BF16_ROWS = 16
LANES = 128
MXU_DIM = 256
VMEM_LIMIT = 56 * 1024 * 1024


class Cfg(NamedTuple):
    d: int
    gw: int
    r: int
    dh: int
    e: int
    f: int
    n_ctx_req: int
    n_ctx: int
    n_lat_req: int
    n_lat: int
    grid_w: int

    @property
    def t(self): return self.n_ctx
    @property
    def p(self): return self.gw * len(POOL_HALF_WINDOWS)
    @property
    def ntc(self): return self.n_ctx_req
    @property
    def tpr(self): return self.n_lat // self.t
    @property
    def nt(self): return self.ntc + self.n_lat_req * self.tpr
    @property
    def n_tok(self): return self.nt * self.t
    @property
    def c_ctx(self): return CAPACITY_FACTOR * self.n_ctx // self.e
    @property
    def c_lat(self): return CAPACITY_FACTOR * self.n_lat // self.e
    @property
    def sblk(self): return self.c_lat
    @property
    def rpb(self): return self.c_lat // self.c_ctx
    @property
    def n_sblk(self): return self.n_ctx_req // self.rpb + self.n_lat_req
    @property
    def w(self): return min(64, self.sblk)
    @property
    def bd(self): return min(MXU_DIM, self.r)
    @property
    def n_tab(self): return 16


def _cparams():
    return pltpu.CompilerParams(dimension_semantics=("arbitrary",), vmem_limit_bytes=VMEM_LIMIT)


def _tab_row(i, cfg):
    return jnp.where(i < cfg.ntc, 0, 1 + (i - cfg.ntc) // cfg.tpr)


def _sigmoid(x):
    return 0.5 * jnp.tanh(0.5 * x) + 0.5


def _lat_pos(i, cfg):
    j = jnp.maximum(i - cfg.ntc, 0)
    return j // cfg.tpr, j % cfg.tpr


def _x_specs(cfg, tile_of):
    def ctx_map(s, *_):
        return (jnp.minimum(tile_of(s), cfg.ntc - 1), 0, 0)

    def lat_map(s, *_):
        b, k = _lat_pos(tile_of(s), cfg)
        return (b, k, 0)

    return [pl.BlockSpec((1, cfg.t, cfg.d), ctx_map), pl.BlockSpec((1, cfg.t, cfg.d), lat_map)]


def _norm_mod(x, g, scale, shift):
    y = x * lax.rsqrt(jnp.mean(x * x, axis=-1, keepdims=True) + EPS)
    return (y * g) * (1.0 + scale) + shift


def _mod_kernel(c_ref, w_ref, b_ref, o_ref):
    c = c_ref[...]
    s = c * _sigmoid(c)
    o_ref[0] = jnp.dot(s, w_ref[0], precision=HIGHEST, preferred_element_type=F32) + b_ref[0]


def _modulation(cvec, w_mod, b_mod, cfg):
    depth, d, n6 = w_mod.shape
    nc = n6 // 4
    return pl.pallas_call(
        _mod_kernel,
        grid=(depth, n6 // nc),
        in_specs=[pl.BlockSpec((cfg.n_tab, d), lambda l, j: (0, 0)),
                  pl.BlockSpec((1, d, nc), lambda l, j: (l, 0, j)),
                  pl.BlockSpec((1, 1, nc), lambda l, j: (l, 0, j))],
        out_specs=pl.BlockSpec((1, cfg.n_tab, nc), lambda l, j: (l, 0, j)),
        out_shape=jax.ShapeDtypeStruct((depth, cfg.n_tab, n6), F32),
        compiler_params=pltpu.CompilerParams(dimension_semantics=("arbitrary", "arbitrary"),
                                             vmem_limit_bytes=VMEM_LIMIT),
        name="modulation",
    )(cvec, w_mod, b_mod.reshape(depth, 1, n6))


def _conv(v, prev, nxt, w, b):
    t = v.shape[0]
    row = lax.broadcasted_iota(I32, v.shape, 0)
    vm1 = jnp.where(row == 0, prev[7:8], pltpu.roll(v, 1, 0))
    vm2 = jnp.where(row == 0, prev[6:7], jnp.where(row == 1, prev[7:8], pltpu.roll(v, 2, 0)))
    vp1 = jnp.where(row == t - 1, nxt[0:1], pltpu.roll(v, t - 1, 0))
    return b + vm2 * w[0:1] + vm1 * w[1:2] + v * w[2:3] + vp1 * w[3:4]


def _gates(vc, wbd_ref, br, bi, lam, cfg):
    bd = cfg.bd
    vcb = vc.astype(BF16)
    zr, zi = [], []
    for j in range(cfg.r // bd):
        z = jnp.dot(vcb[:, j * bd:(j + 1) * bd], wbd_ref[j], preferred_element_type=F32)
        zr.append(z[:, :bd])
        zi.append(z[:, bd:])
    rg = _sigmoid(jnp.concatenate(zr, axis=1) + br)
    ig = _sigmoid(jnp.concatenate(zi, axis=1) + bi)
    nl = -lam
    softplus = jnp.maximum(nl, 0.0) + jnp.log1p(jnp.exp(-jnp.abs(nl)))
    log_a = -LRU_C * rg * softplus
    a = jnp.exp(log_a)
    u = jnp.sqrt(-jnp.tanh(log_a) * (a * a + 1.0)) * ig * vc
    return a, u


def _scan(a_scr, u_scr, out_ref, h_in, reverse):
    groups = a_scr.shape[0] // SUBLANES
    row = lax.broadcasted_iota(I32, (SUBLANES, a_scr.shape[1]), 0)
    steps = [((row < SUBLANES - s) if reverse else (row >= s), SUBLANES - s if reverse else s)
             for s in (1, 2, 4)]

    def body(j, h):
        g = groups - 1 - j if reverse else j
        r0 = pl.multiple_of(g * SUBLANES, SUBLANES)
        a = a_scr[pl.ds(r0, SUBLANES), :]
        u = u_scr[pl.ds(r0, SUBLANES), :]
        for ok, shift in steps:
            u = u + a * jnp.where(ok, pltpu.roll(u, shift, 0), 0.0)
            a = a * jnp.where(ok, pltpu.roll(a, shift, 0), 1.0)
        hg = a * h + u
        out_ref[pl.ds(r0, SUBLANES), :] = hg
        return hg[0:1] if reverse else hg[SUBLANES - 1:SUBLANES]

    return lax.fori_loop(0, groups, body, h_in, unroll=4)


def _tile_flags(i, cfg):
    is_ctx = i < cfg.ntc
    k = jnp.maximum(i - cfg.ntc, 0) % cfg.tpr
    first = jnp.logical_or(is_ctx, k == 0)
    last = jnp.logical_or(is_ctx, k == cfg.tpr - 1)
    return is_ctx, first, last


def _scan_bwd_kernel(xc_ref, xl_ref, xprev_ref, xnext_ref, mod_ref, g1_ref, wv_ref, cw_ref, cb_ref, wbd_ref,
                     br_ref, bi_ref, lam_ref, h0_ref, vc_ref, hb_ref, hlast_ref, a_scr, u_scr, carry_scr,
                     *, cfg):
    i = cfg.nt - 1 - pl.program_id(0)
    t, d = cfg.t, cfg.d
    is_ctx, first, last = _tile_flags(i, cfg)
    x_ext = jnp.concatenate([xprev_ref[0], jnp.where(is_ctx, xc_ref[0], xl_ref[0]), xnext_ref[0]], axis=0)
    m = mod_ref[0]
    hn = _norm_mod(x_ext, g1_ref[...], m[:, d:2 * d], m[:, 0:d]).astype(BF16)
    v_ext = jnp.dot(hn, wv_ref[...], preferred_element_type=F32)
    prev = jnp.where(first, 0.0, v_ext[0:SUBLANES])
    nxt = jnp.where(last, 0.0, v_ext[t + SUBLANES:t + 2 * SUBLANES])
    vc = _conv(v_ext[SUBLANES:t + SUBLANES], prev, nxt, cw_ref[...], cb_ref[...])
    vc_ref[...] = vc
    a, u = _gates(vc, wbd_ref, br_ref[...], bi_ref[...], lam_ref[...], cfg)
    a_scr[...] = a
    u_scr[...] = u

    @pl.when(last)
    def _():
        carry_scr[...] = h0_ref[0]

    h = _scan(a_scr, u_scr, hb_ref, carry_scr[...], True)
    carry_scr[...] = h

    @pl.when(is_ctx)
    def _():
        hlast_ref[0] = h


def _scan_bwd(xc, xl, mod, g1, wv, cw, cb, wbd, br, bi, lam, h0, cfg):
    t, d, r, nt = cfg.t, cfg.d, cfg.r, cfg.nt
    n = cfg.n_tok
    tb = t // SUBLANES
    rev = lambda s: nt - 1 - s
    full = lambda a: pl.BlockSpec(a.shape, lambda s: (0,) * a.ndim)
    tile = pl.BlockSpec((t, r), lambda s: (rev(s), 0))
    tab = lambda width: pl.BlockSpec((1, 1, width), lambda s: (_tab_row(rev(s), cfg), 0, 0))

    def prev_map(s):
        b, k = _lat_pos(rev(s), cfg)
        return (b, jnp.maximum(k * tb - 1, 0), 0)

    def next_map(s):
        b, k = _lat_pos(rev(s), cfg)
        return (b, jnp.minimum((k + 1) * tb, cfg.n_lat // SUBLANES - 1), 0)

    return pl.pallas_call(
        functools.partial(_scan_bwd_kernel, cfg=cfg),
        grid=(nt,),
        in_specs=_x_specs(cfg, rev) + [
            pl.BlockSpec((1, SUBLANES, d), prev_map), pl.BlockSpec((1, SUBLANES, d), next_map),
            tab(N_MOD * d), full(g1), full(wv), full(cw), full(cb), full(wbd), full(br), full(bi), full(lam),
            tab(r)],
        out_specs=[tile, tile,
                   pl.BlockSpec((1, 1, r), lambda s: (jnp.minimum(rev(s), cfg.ntc - 1), 0, 0))],
        out_shape=[jax.ShapeDtypeStruct((n, r), F32),
                   jax.ShapeDtypeStruct((n, r), F32),
                   jax.ShapeDtypeStruct((cfg.n_ctx_req, 1, r), F32)],
        scratch_shapes=[pltpu.VMEM((t, r), F32), pltpu.VMEM((t, r), F32), pltpu.VMEM((1, r), F32)],
        compiler_params=_cparams(),
        name="scan_bwd",
    )(xc, xl, xl, xl, mod, g1, wv, cw, cb, wbd, br, bi, lam, h0)


def _pool(pv, is_ctx, cfg):
    t = pv.shape[0]
    gw = cfg.gw
    row = lax.broadcasted_iota(I32, (t, gw), 0)
    pos = jnp.where(is_ctx, row % cfg.n_ctx, row % cfg.grid_w)
    length = jnp.where(is_ctx, cfg.n_ctx, cfg.grid_w)

    def shifted(x, dlt):
        ok = jnp.logical_and(pos + dlt >= 0, pos + dlt < length)
        return jnp.where(ok, pltpu.roll(x, (-dlt) % t, 0), 0.0)

    outs = []
    for g, m in enumerate(POOL_HALF_WINDOWS):
        x = pv[:, g * gw:(g + 1) * gw]
        back, fwd, k = x, x, 1
        while k < m:
            back = back + shifted(back, -k)
            fwd = fwd + shifted(fwd, k)
            k *= 2
        s = shifted(back, -1) + fwd
        cnt = (jnp.minimum(pos + m, length) - jnp.maximum(pos - m, 0)).astype(F32)
        outs.append(s / cnt - x)
    return outs


def _scan_fwd_kernel(xc_ref, xl_ref, vc_ref, hb_ref, mod_ref, h0_ref, g1_ref, wp_ref, wg_ref, wbd_ref, br_ref,
                     bi_ref, lam_ref, pw_ref, ps_ref, wbp_ref, wbl_ref, wo_ref, g2_ref, rwa_ref, rwb_ref,
                     x1_ref, hn2_ref, lg_ref, hlast_ref, a_scr, u_scr, hf_scr, carry_scr, *, cfg):
    i = pl.program_id(0)
    d, gw = cfg.d, cfg.gw
    is_ctx, first, _ = _tile_flags(i, cfg)
    a, u = _gates(vc_ref[...], wbd_ref, br_ref[...], bi_ref[...], lam_ref[...], cfg)
    a_scr[...] = a
    u_scr[...] = u

    @pl.when(first)
    def _():
        carry_scr[...] = h0_ref[0]

    h = _scan(a_scr, u_scr, hf_scr, carry_scr[...], False)
    carry_scr[...] = h

    @pl.when(is_ctx)
    def _():
        hlast_ref[0] = h

    y_lru = (hf_scr[...] + hb_ref[...]).astype(BF16)

    m = mod_ref[0]
    x = jnp.where(is_ctx, xc_ref[0], xl_ref[0])
    hn = _norm_mod(x, g1_ref[...], m[:, d:2 * d], m[:, 0:d]).astype(BF16)
    pooled = _pool(jnp.dot(hn, wp_ref[...], preferred_element_type=F32), is_ctx, cfg)
    ps = ps_ref[...]
    y_pool = jnp.concatenate(
        [jnp.dot(pooled[g].astype(BF16), pw_ref[g], preferred_element_type=F32) * ps[:, g * gw:(g + 1) * gw]
         for g in range(len(POOL_HALF_WINDOWS))], axis=1).astype(BF16)

    merged = (_sigmoid(jnp.dot(hn, wg_ref[:, :d], preferred_element_type=F32))
              * jnp.dot(y_pool, wbp_ref[...], preferred_element_type=F32)
              + _sigmoid(jnp.dot(hn, wg_ref[:, d:], preferred_element_type=F32))
              * jnp.dot(y_lru, wbl_ref[...], preferred_element_type=F32))
    mix = jnp.dot(merged.astype(BF16), wo_ref[...], preferred_element_type=F32)
    x1 = x + m[:, 2 * d:3 * d] * mix
    x1_ref[...] = x1
    hn2 = _norm_mod(x1, g2_ref[...], m[:, 4 * d:5 * d], m[:, 3 * d:4 * d])
    hi = hn2.astype(BF16)
    hn2_ref[...] = hi
    lo = (hn2 - hi.astype(F32)).astype(BF16)
    lg_ref[...] = (jnp.dot(hi, rwa_ref[...], preferred_element_type=F32)
                   + jnp.dot(hi, rwb_ref[...], preferred_element_type=F32)
                   + jnp.dot(lo, rwa_ref[...], preferred_element_type=F32))


def _scan_fwd(xc, xl, vc, hb, mod, h0, g1, wp, wg, wbd, br, bi, lam, pw, ps, wbp, wbl, wo, g2, rwa, rwb, cfg):
    t, d, r = cfg.t, cfg.d, cfg.r
    n = cfg.n_tok
    tile = lambda width: pl.BlockSpec((t, width), lambda i: (i, 0))
    full = lambda a: pl.BlockSpec(a.shape, lambda i: (0,) * a.ndim)
    tab = lambda width: pl.BlockSpec((1, 1, width), lambda i: (_tab_row(i, cfg), 0, 0))
    return pl.pallas_call(
        functools.partial(_scan_fwd_kernel, cfg=cfg),
        grid=(cfg.nt,),
        in_specs=_x_specs(cfg, lambda i: i) + [
            tile(r), tile(r), tab(N_MOD * d), tab(r), full(g1), full(wp), full(wg),
            full(wbd), full(br), full(bi), full(lam), full(pw), full(ps), full(wbp), full(wbl),
            full(wo), full(g2), full(rwa), full(rwb)],
        out_specs=[tile(d), tile(d), tile(cfg.e),
                   pl.BlockSpec((1, 1, r), lambda i: (jnp.minimum(i, cfg.ntc - 1), 0, 0))],
        out_shape=[jax.ShapeDtypeStruct((n, d), F32),
                   jax.ShapeDtypeStruct((n, d), BF16),
                   jax.ShapeDtypeStruct((n, cfg.e), F32),
                   jax.ShapeDtypeStruct((cfg.n_ctx_req, 1, r), F32)],
        scratch_shapes=[pltpu.VMEM((t, r), F32), pltpu.VMEM((t, r), F32), pltpu.VMEM((t, r), F32),
                        pltpu.VMEM((1, r), F32)],
        compiler_params=_cparams(),
        name="scan_fwd",
    )(xc, xl, vc, hb, mod, h0, g1, wp, wg, wbd, br, bi, lam, pw, ps, wbp, wbl, wo, g2, rwa, rwb)


def _route_kernel(l_ref, selpos_ref, gate_ref, offs_ref, cnts_ref, *, cap, t):
    lg = l_ref[...]
    e, n = lg.shape
    nch = n // t
    ex = jnp.exp(lg - jnp.max(lg, axis=0, keepdims=True))
    aff = ex / jnp.sum(ex, axis=0, keepdims=True)
    gate_ref[...] = aff
    keys = pltpu.bitcast(aff, I32)

    def count(mask):
        return jnp.sum(mask.astype(F32), axis=1, keepdims=True)

    thr = jnp.zeros((e, 1), I32)
    for bit in range(30, -1, -1):
        cand = thr | (1 << bit)
        thr = jnp.where(count(keys >= cand) >= cap, cand, thr)
    gt = keys > thr
    eq = keys == thr
    need = cap - count(gt)

    tri = (lax.broadcasted_iota(I32, (t, t), 0) < lax.broadcasted_iota(I32, (t, t), 1)).astype(BF16)
    tok = lax.broadcasted_iota(I32, (n, LANES), 0)
    chunk = lax.broadcasted_iota(I32, (n, LANES), 1)
    before = (tok < chunk * t).astype(BF16)
    inside = (tok // t == chunk).astype(BF16)

    def prefix(mask):
        mb = mask.astype(BF16)
        offs = jnp.dot(mb, before, preferred_element_type=F32)
        pre = [jnp.dot(mb[:, k * t:(k + 1) * t], tri, preferred_element_type=F32) + offs[:, k:k + 1]
               for k in range(nch)]
        return jnp.concatenate(pre, axis=1) if nch > 1 else pre[0], offs, mb

    eq_rank, _, _ = prefix(eq)
    sel = jnp.logical_or(gt, jnp.logical_and(eq, eq_rank < need))
    pos, offs, selb = prefix(sel)
    selpos_ref[...] = jnp.where(sel, pos.astype(I32), NOT_SELECTED)
    offs_ref[0] = offs.astype(I32)
    cnts_ref[0] = jnp.dot(selb, inside, preferred_element_type=F32).astype(I32)


def _route(logits_t, n_req, n, col0, cap, cfg):
    e = cfg.e
    return pl.pallas_call(
        functools.partial(_route_kernel, cap=cap, t=cfg.t),
        grid=(n_req,),
        in_specs=[pl.BlockSpec((e, n), lambda b: (0, col0 + b))],
        out_specs=[pl.BlockSpec((e, n), lambda b: (0, b)),
                   pl.BlockSpec((e, n), lambda b: (0, b)),
                   pl.BlockSpec((1, e, LANES), lambda b: (b, 0, 0)),
                   pl.BlockSpec((1, e, LANES), lambda b: (b, 0, 0))],
        out_shape=[jax.ShapeDtypeStruct((e, n_req * n), I32),
                   jax.ShapeDtypeStruct((e, n_req * n), F32),
                   jax.ShapeDtypeStruct((n_req, e, LANES), I32),
                   jax.ShapeDtypeStruct((n_req, e, LANES), I32)],
        compiler_params=_cparams(),
        name="route",
    )(logits_t)


def _slot_block(i, cfg):
    return jnp.where(i < cfg.ntc, i // cfg.rpb, cfg.ntc // cfg.rpb + (i - cfg.ntc) // cfg.tpr)


def _windows(i, toff_ref, tcnt_ref, cfg):
    is_ctx = i < cfg.ntc
    base = jnp.where(is_ctx, (i % cfg.rpb) * cfg.c_ctx, 0)
    starts, npass = [], 0
    for e in range(cfg.e):
        off = base + toff_ref[i * cfg.e + e]
        start = (off // BF16_ROWS) * BF16_ROWS
        starts.append(start)
        npass = jnp.maximum(npass, (off + tcnt_ref[i * cfg.e + e] - start + cfg.w - 1) // cfg.w)
    return base, starts, npass


def _window(start, j, cfg):
    want = start + j * cfg.w
    a = pl.multiple_of(jnp.minimum(want, cfg.sblk - cfg.w), BF16_ROWS)
    return a, want - a


def _dispatch_kernel(toff_ref, tcnt_ref, x_ref, sp_ref, g_ref, xs_ref, gs_ref, p_scr, *, cfg):
    i = pl.program_id(0)
    w = cfg.w
    is_ctx = i < cfg.ntc
    new_block = jnp.where(is_ctx, i % cfg.rpb == 0, jnp.maximum(i - cfg.ntc, 0) % cfg.tpr == 0)

    @pl.when(new_block)
    def _():
        xs_ref[...] = jnp.zeros(xs_ref.shape, xs_ref.dtype)
        gs_ref[...] = jnp.zeros(gs_ref.shape, gs_ref.dtype)

    base, starts, npass = _windows(i, toff_ref, tcnt_ref, cfg)
    slot = lax.broadcasted_iota(I32, (w, cfg.t), 0)

    def one_pass(j, carry):
        firsts = []
        for e in range(cfg.e):
            a, owned = _window(starts[e], j, cfg)
            firsts.append(a)
            rel = sp_ref[e:e + 1, :] - (a - base)
            hit = jnp.logical_and(rel == slot, slot >= owned)
            p_scr[e * w:(e + 1) * w, :] = hit.astype(BF16)
            gs_ref[e, pl.ds(a, w), :] += jnp.sum(jnp.where(hit, g_ref[e:e + 1, :], 0.0), axis=1, keepdims=True)
        rows = jnp.dot(p_scr[...], x_ref[...], preferred_element_type=F32)
        for e in range(cfg.e):
            cur = xs_ref[e, pl.ds(firsts[e], w), :].astype(F32)
            xs_ref[e, pl.ds(firsts[e], w), :] = (cur + rows[e * w:(e + 1) * w]).astype(BF16)
        return carry

    lax.fori_loop(0, npass, one_pass, 0)


def _dispatch(toff, tcnt, hn2, selpos, gate, cfg):
    t, d, e = cfg.t, cfg.d, cfg.e
    s = cfg.n_sblk * cfg.sblk
    slots = lambda width: pl.BlockSpec((e, cfg.sblk, width), lambda i, *_: (0, _slot_block(i, cfg), 0))
    return pl.pallas_call(
        functools.partial(_dispatch_kernel, cfg=cfg),
        grid_spec=pltpu.PrefetchScalarGridSpec(
            num_scalar_prefetch=2,
            grid=(cfg.nt,),
            in_specs=[pl.BlockSpec((t, d), lambda i, *_: (i, 0)),
                      pl.BlockSpec((e, t), lambda i, *_: (0, i)),
                      pl.BlockSpec((e, t), lambda i, *_: (0, i))],
            out_specs=[slots(d), slots(1)],
            scratch_shapes=[pltpu.VMEM((e * cfg.w, t), BF16)]),
        out_shape=[jax.ShapeDtypeStruct((e, s, d), BF16), jax.ShapeDtypeStruct((e, s, 1), F32)],
        compiler_params=_cparams(),
        name="dispatch",
    )(toff, tcnt, hn2, selpos, gate)


def _ffn_kernel(x_ref, gs_ref, w1_ref, w3_ref, w2_ref, o_ref, *, fc):
    x = x_ref[0]
    acc = jnp.zeros((x.shape[0], o_ref.shape[2]), F32)
    for c in range(w1_ref.shape[3] // fc):
        sl = slice(c * fc, (c + 1) * fc)
        h1 = jnp.dot(x, w1_ref[0, 0, :, sl], preferred_element_type=F32)
        h3 = jnp.dot(x, w3_ref[0, 0, :, sl], preferred_element_type=F32)
        hid = (h1 * _sigmoid(h1)) * h3
        acc = acc + jnp.dot(hid.astype(BF16), w2_ref[0, 0, sl, :], preferred_element_type=F32)
    o_ref[0] = (acc * gs_ref[0]).astype(BF16)


def _ffn(xs, gs, w1, w3, w2, layer, cfg):
    e, s, d = xs.shape
    f = w1.shape[3]
    tm = cfg.sblk
    return pl.pallas_call(
        functools.partial(_ffn_kernel, fc=min(512, f)),
        grid=(e, s // tm),
        in_specs=[pl.BlockSpec((1, tm, d), lambda k, m: (k, m, 0)),
                  pl.BlockSpec((1, tm, 1), lambda k, m: (k, m, 0)),
                  pl.BlockSpec((1, 1, d, f), lambda k, m: (layer, k, 0, 0)),
                  pl.BlockSpec((1, 1, d, f), lambda k, m: (layer, k, 0, 0)),
                  pl.BlockSpec((1, 1, f, d), lambda k, m: (layer, k, 0, 0))],
        out_specs=pl.BlockSpec((1, tm, d), lambda k, m: (k, m, 0)),
        out_shape=jax.ShapeDtypeStruct((e, s, d), BF16),
        compiler_params=pltpu.CompilerParams(dimension_semantics=("arbitrary", "arbitrary"),
                                             vmem_limit_bytes=VMEM_LIMIT),
        name="ffn",
    )(xs, gs, w1, w3, w2)


def _combine_kernel(toff_ref, tcnt_ref, x1_ref, mod_ref, spt_ref, o_ref, fg_ref, oc_ref, ol_ref, ow_scr,
                    *, cfg, final):
    i = pl.program_id(0)
    w, d = cfg.w, cfg.d
    base, starts, npass = _windows(i, toff_ref, tcnt_ref, cfg)
    lane = lax.broadcasted_iota(I32, (cfg.t, w), 1)
    spt = spt_ref[...]

    def one_pass(j, y):
        hits = []
        for e in range(cfg.e):
            a, owned = _window(starts[e], j, cfg)
            ow_scr[e * w:(e + 1) * w, :] = o_ref[e, pl.ds(a, w), :]
            rel = spt[:, e:e + 1] - (a - base)
            hits.append(jnp.logical_and(rel == lane, lane >= owned).astype(BF16))
        return y + jnp.dot(jnp.concatenate(hits, axis=1), ow_scr[...], preferred_element_type=F32)

    y = lax.fori_loop(0, npass, one_pass, jnp.zeros((cfg.t, d), F32))
    x2 = x1_ref[...] + mod_ref[0][:, 5 * d:6 * d] * y
    if final:
        x2 = (x2 * lax.rsqrt(jnp.mean(x2 * x2, axis=-1, keepdims=True) + EPS)) * fg_ref[...]
    is_ctx = i < cfg.ntc

    @pl.when(is_ctx)
    def _():
        oc_ref[0] = x2

    @pl.when(jnp.logical_not(is_ctx))
    def _():
        ol_ref[0] = x2


def _combine(toff, tcnt, x1, mod, selpos_t, o, fg, cfg, final):
    t, d, e = cfg.t, cfg.d, cfg.e
    return pl.pallas_call(
        functools.partial(_combine_kernel, cfg=cfg, final=final),
        grid_spec=pltpu.PrefetchScalarGridSpec(
            num_scalar_prefetch=2,
            grid=(cfg.nt,),
            in_specs=[pl.BlockSpec((t, d), lambda i, *_: (i, 0)),
                      pl.BlockSpec((1, 1, N_MOD * d), lambda i, *_: (_tab_row(i, cfg), 0, 0)),
                      pl.BlockSpec((t, e), lambda i, *_: (i, 0)),
                      pl.BlockSpec((e, cfg.sblk, d), lambda i, *_: (0, _slot_block(i, cfg), 0)),
                      pl.BlockSpec((1, d), lambda i, *_: (0, 0))],
            out_specs=_x_specs(cfg, lambda i: i),
            scratch_shapes=[pltpu.VMEM((e * cfg.w, d), BF16)]),
        out_shape=[jax.ShapeDtypeStruct((cfg.n_ctx_req, cfg.n_ctx, d), F32),
                   jax.ShapeDtypeStruct((cfg.n_lat_req, cfg.n_lat, d), F32)],
        compiler_params=_cparams(),
        name="combine",
    )(toff, tcnt, x1, mod, selpos_t, o, fg)


def _block_diag_gates(wr, wi, cfg):
    hpb = cfg.bd // cfg.dh
    nblk = cfg.r // cfg.bd
    eye = jnp.eye(hpb, dtype=wr.dtype)

    def blocks(w):
        w4 = w.reshape(nblk, hpb, cfg.dh, cfg.dh)
        return jnp.einsum('jhde,hk->jhdke', w4, eye).reshape(nblk, cfg.bd, cfg.bd)

    return jnp.concatenate([blocks(wr), blocks(wi)], axis=2).astype(BF16)


def _tile_tables(offs_ctx, offs_lat, cfg):
    ctx = offs_ctx[:, :, 0]
    lat = jnp.swapaxes(offs_lat[:, :, :cfg.tpr], 1, 2).reshape(-1, cfg.e)
    return jnp.concatenate([ctx, lat], axis=0).reshape(-1)


def _forward(cfg, x_prompt, x_sample, state_lru, c, c_ctx, norm1_g, norm2_g, final_g, w_mod, b_mod, w_in,
             pool_w, pool_scale, conv_w, conv_b, lru_wr, lru_br, lru_wi, lru_bi, lru_lambda,
             w_br_pool, w_br_lru, w_out, router_w, exp_w1, exp_w3, exp_w2):
    d, r, p = cfg.d, cfg.r, cfg.p
    depth = w_in.shape[0]
    assert cfg.n_lat % cfg.t == 0 and cfg.t % cfg.grid_w == 0 and cfg.n_ctx_req % cfg.rpb == 0
    assert (cfg.n_ctx_req * cfg.n_ctx) % cfg.n_lat == 0 and cfg.w % BF16_ROWS == 0

    n_ctx_tok = cfg.n_ctx_req * cfg.n_ctx
    xc, xl = x_prompt, x_sample

    cvec = jnp.zeros((cfg.n_tab, d), F32).at[0].set(c_ctx).at[1:1 + cfg.n_lat_req].set(c)
    mod = _modulation(cvec, w_mod, b_mod, cfg)
    w1b, w3b, w2b = exp_w1.astype(BF16), exp_w3.astype(BF16), exp_w2.astype(BF16)

    states = []
    for l in range(depth):
        mod_l = mod[l].reshape(cfg.n_tab, 1, N_MOD * d)
        row = lambda a: a.reshape(1, -1)
        wp = w_in[l][:, :p].astype(BF16)
        wv = w_in[l][:, p:p + r].astype(BF16)
        wg = w_in[l][:, p + r:].astype(BF16)
        rwa = router_w[l].astype(BF16)
        rwb = (router_w[l] - rwa.astype(F32)).astype(BF16)

        def h0(direction):
            tab = jnp.zeros((cfg.n_tab, 1, r), F32)
            return tab.at[1:1 + cfg.n_lat_req, 0].set(state_lru[:, l, direction].astype(F32))

        wbd = [_block_diag_gates(lru_wr[l, z], lru_wi[l, z], cfg) for z in range(2)]
        vc, hb, hb_last = _scan_bwd(xc, xl, mod_l, row(norm1_g[l]), wv, conv_w[l], row(conv_b[l]), wbd[1],
                                    row(lru_br[l, 1]), row(lru_bi[l, 1]), row(lru_lambda[l, 1]), h0(1), cfg)
        x1, hn2, logits, hf_last = _scan_fwd(
            xc, xl, vc, hb, mod_l, h0(0), row(norm1_g[l]), wp, wg, wbd[0], row(lru_br[l, 0]), row(lru_bi[l, 0]),
            row(lru_lambda[l, 0]), pool_w[l].astype(BF16), row(pool_scale[l]), w_br_pool[l].astype(BF16),
            w_br_lru[l].astype(BF16), w_out[l].astype(BF16), row(norm2_g[l]), rwa, rwb, cfg)
        states.append(jnp.stack([hf_last[:, 0], hb_last[:, 0]], axis=1))

        logits_t = logits.T
        sp_c, g_c, off_c, cnt_c = _route(logits_t, cfg.n_ctx_req, cfg.n_ctx, 0, cfg.c_ctx, cfg)
        sp_l, g_l, off_l, cnt_l = _route(logits_t, cfg.n_lat_req, cfg.n_lat, n_ctx_tok // cfg.n_lat,
                                         cfg.c_lat, cfg)
        selpos = jnp.concatenate([sp_c, sp_l], axis=1)
        gate = jnp.concatenate([g_c, g_l], axis=1)
        toff = _tile_tables(off_c, off_l, cfg)
        tcnt = _tile_tables(cnt_c, cnt_l, cfg)

        xs, gs = _dispatch(toff, tcnt, hn2, selpos, gate, cfg)
        o = _ffn(xs, gs, w1b, w3b, w2b, l, cfg)
        xc, xl = _combine(toff, tcnt, x1, mod_l, selpos.T, o, row(final_g), cfg, l == depth - 1)

    new_state = jnp.stack(states, axis=1).astype(x_prompt.dtype)
    return xc, xl, new_state


def kernel(x_prompt, x_sample, state_lru, c, c_ctx, norm1_g, norm2_g, final_g, w_mod, b_mod, w_in, pool_w,
           pool_scale, conv_w, conv_b, lru_wr, lru_br, lru_wi, lru_bi, lru_lambda, w_br_pool, w_br_lru, w_out,
           router_w, exp_w1, exp_w3, exp_w2):
    cfg = Cfg(d=x_prompt.shape[2], gw=pool_w.shape[2], r=lru_lambda.shape[2], dh=lru_wr.shape[3],
              e=router_w.shape[2], f=exp_w1.shape[3], n_ctx_req=x_prompt.shape[0], n_ctx=x_prompt.shape[1],
              n_lat_req=x_sample.shape[0], n_lat=x_sample.shape[1], grid_w=GRID_W)
    return _forward(cfg, x_prompt, x_sample, state_lru, c, c_ctx, norm1_g, norm2_g, final_g, w_mod, b_mod, w_in,
                    pool_w, pool_scale, conv_w, conv_b, lru_wr, lru_br, lru_wi, lru_bi, lru_lambda,
                    w_br_pool, w_br_lru, w_out, router_w, exp_w1, exp_w3, exp_w2)
```

```python
import functools
from typing import NamedTuple

import jax
import jax.numpy as jnp
from jax import lax
from jax.experimental import pallas as pl
from jax.experimental.pallas import tpu as pltpu

F32 = jnp.float32
BF16 = jnp.bfloat16
I32 = jnp.int32
HIGHEST = lax.Precision.HIGHEST

EPS = 1e-6
LRU_C = 8.0
CONV_WIDTH = 4
POOL_HALF_WINDOWS = (1, 2, 4, 8)
N_MOD = 6
CAPACITY_FACTOR = 2
GRID_W = 64
NOT_SELECTED = -(1 << 20)
SUBLANES = 8
BF16_ROWS = 16
LANES = 128
MXU_DIM = 256
VMEM_LIMIT = 56 * 1024 * 1024


class Cfg(NamedTuple):
    d: int
    gw: int
    r: int
    dh: int
    e: int
    f: int
    n_ctx_req: int
    n_ctx: int
    n_lat_req: int
    n_lat: int
    grid_w: int

    @property
    def t(self): return self.n_ctx
    @property
    def p(self): return self.gw * len(POOL_HALF_WINDOWS)
    @property
    def ntc(self): return self.n_ctx_req
    @property
    def tpr(self): return self.n_lat // self.t
    @property
    def nt(self): return self.ntc + self.n_lat_req * self.tpr
    @property
    def n_tok(self): return self.nt * self.t
    @property
    def c_ctx(self): return CAPACITY_FACTOR * self.n_ctx // self.e
    @property
    def c_lat(self): return CAPACITY_FACTOR * self.n_lat // self.e
    @property
    def sblk(self): return self.c_lat
    @property
    def rpb(self): return self.c_lat // self.c_ctx
    @property
    def n_sblk(self): return self.n_ctx_req // self.rpb + self.n_lat_req
    @property
    def w(self): return min(64, self.sblk)
    @property
    def bd(self): return min(MXU_DIM, self.r)
    @property
    def n_tab(self): return 16


def _cparams():
    return pltpu.CompilerParams(dimension_semantics=("arbitrary",), vmem_limit_bytes=VMEM_LIMIT)


def _tab_row(i, cfg):
    return jnp.where(i < cfg.ntc, 0, 1 + (i - cfg.ntc) // cfg.tpr)


def _sigmoid(x):
    return 0.5 * jnp.tanh(0.5 * x) + 0.5


def _lat_pos(i, cfg):
    j = jnp.maximum(i - cfg.ntc, 0)
    return j // cfg.tpr, j % cfg.tpr


def _x_specs(cfg, tile_of):
    def ctx_map(s, *_):
        return (jnp.minimum(tile_of(s), cfg.ntc - 1), 0, 0)

    def lat_map(s, *_):
        b, k = _lat_pos(tile_of(s), cfg)
        return (b, k, 0)

    return [pl.BlockSpec((1, cfg.t, cfg.d), ctx_map), pl.BlockSpec((1, cfg.t, cfg.d), lat_map)]


def _norm_mod(x, g, scale, shift):
    y = x * lax.rsqrt(jnp.mean(x * x, axis=-1, keepdims=True) + EPS)
    return (y * g) * (1.0 + scale) + shift


def _mod_kernel(c_ref, w_ref, b_ref, o_ref):
    c = c_ref[...]
    s = c * _sigmoid(c)
    o_ref[0] = jnp.dot(s, w_ref[0], precision=HIGHEST, preferred_element_type=F32) + b_ref[0]


def _modulation(cvec, w_mod, b_mod, cfg):
    depth, d, n6 = w_mod.shape
    nc = n6 // 4
    return pl.pallas_call(
        _mod_kernel,
        grid=(depth, n6 // nc),
        in_specs=[pl.BlockSpec((cfg.n_tab, d), lambda l, j: (0, 0)),
                  pl.BlockSpec((1, d, nc), lambda l, j: (l, 0, j)),
                  pl.BlockSpec((1, 1, nc), lambda l, j: (l, 0, j))],
        out_specs=pl.BlockSpec((1, cfg.n_tab, nc), lambda l, j: (l, 0, j)),
        out_shape=jax.ShapeDtypeStruct((depth, cfg.n_tab, n6), F32),
        compiler_params=pltpu.CompilerParams(dimension_semantics=("arbitrary", "arbitrary"),
                                             vmem_limit_bytes=VMEM_LIMIT),
        name="modulation",
    )(cvec, w_mod, b_mod.reshape(depth, 1, n6))


def _conv(v, prev, nxt, w, b):
    t = v.shape[0]
    row = lax.broadcasted_iota(I32, v.shape, 0)
    vm1 = jnp.where(row == 0, prev[7:8], pltpu.roll(v, 1, 0))
    vm2 = jnp.where(row == 0, prev[6:7], jnp.where(row == 1, prev[7:8], pltpu.roll(v, 2, 0)))
    vp1 = jnp.where(row == t - 1, nxt[0:1], pltpu.roll(v, t - 1, 0))
    return b + vm2 * w[0:1] + vm1 * w[1:2] + v * w[2:3] + vp1 * w[3:4]


def _gates(vc, wbd_ref, br, bi, lam, cfg):
    bd = cfg.bd
    vcb = vc.astype(BF16)
    zr, zi = [], []
    for j in range(cfg.r // bd):
        z = jnp.dot(vcb[:, j * bd:(j + 1) * bd], wbd_ref[j], preferred_element_type=F32)
        zr.append(z[:, :bd])
        zi.append(z[:, bd:])
    rg = _sigmoid(jnp.concatenate(zr, axis=1) + br)
    ig = _sigmoid(jnp.concatenate(zi, axis=1) + bi)
    nl = -lam
    softplus = jnp.maximum(nl, 0.0) + jnp.log1p(jnp.exp(-jnp.abs(nl)))
    log_a = -LRU_C * rg * softplus
    a = jnp.exp(log_a)
    u = jnp.sqrt(-jnp.tanh(log_a) * (a * a + 1.0)) * ig * vc
    return a, u


def _scan(a_scr, u_scr, out_ref, h_in, reverse):
    groups = a_scr.shape[0] // SUBLANES
    row = lax.broadcasted_iota(I32, (SUBLANES, a_scr.shape[1]), 0)
    steps = [((row < SUBLANES - s) if reverse else (row >= s), SUBLANES - s if reverse else s)
             for s in (1, 2, 4)]

    def body(j, h):
        g = groups - 1 - j if reverse else j
        r0 = pl.multiple_of(g * SUBLANES, SUBLANES)
        a = a_scr[pl.ds(r0, SUBLANES), :]
        u = u_scr[pl.ds(r0, SUBLANES), :]
        for ok, shift in steps:
            u = u + a * jnp.where(ok, pltpu.roll(u, shift, 0), 0.0)
            a = a * jnp.where(ok, pltpu.roll(a, shift, 0), 1.0)
        hg = a * h + u
        out_ref[pl.ds(r0, SUBLANES), :] = hg
        return hg[0:1] if reverse else hg[SUBLANES - 1:SUBLANES]

    return lax.fori_loop(0, groups, body, h_in, unroll=4)


def _tile_flags(i, cfg):
    is_ctx = i < cfg.ntc
    k = jnp.maximum(i - cfg.ntc, 0) % cfg.tpr
    first = jnp.logical_or(is_ctx, k == 0)
    last = jnp.logical_or(is_ctx, k == cfg.tpr - 1)
    return is_ctx, first, last


def _scan_bwd_kernel(xc_ref, xl_ref, xprev_ref, xnext_ref, mod_ref, g1_ref, wv_ref, cw_ref, cb_ref, wbd_ref,
                     br_ref, bi_ref, lam_ref, h0_ref, vc_ref, hb_ref, hlast_ref, a_scr, u_scr, carry_scr,
                     *, cfg):
    i = cfg.nt - 1 - pl.program_id(0)
    t, d = cfg.t, cfg.d
    is_ctx, first, last = _tile_flags(i, cfg)
    x_ext = jnp.concatenate([xprev_ref[0], jnp.where(is_ctx, xc_ref[0], xl_ref[0]), xnext_ref[0]], axis=0)
    m = mod_ref[0]
    hn = _norm_mod(x_ext, g1_ref[...], m[:, d:2 * d], m[:, 0:d]).astype(BF16)
    v_ext = jnp.dot(hn, wv_ref[...], preferred_element_type=F32)
    prev = jnp.where(first, 0.0, v_ext[0:SUBLANES])
    nxt = jnp.where(last, 0.0, v_ext[t + SUBLANES:t + 2 * SUBLANES])
    vc = _conv(v_ext[SUBLANES:t + SUBLANES], prev, nxt, cw_ref[...], cb_ref[...])
    vc_ref[...] = vc
    a, u = _gates(vc, wbd_ref, br_ref[...], bi_ref[...], lam_ref[...], cfg)
    a_scr[...] = a
    u_scr[...] = u

    @pl.when(last)
    def _():
        carry_scr[...] = h0_ref[0]

    h = _scan(a_scr, u_scr, hb_ref, carry_scr[...], True)
    carry_scr[...] = h

    @pl.when(is_ctx)
    def _():
        hlast_ref[0] = h


def _scan_bwd(xc, xl, mod, g1, wv, cw, cb, wbd, br, bi, lam, h0, cfg):
    t, d, r, nt = cfg.t, cfg.d, cfg.r, cfg.nt
    n = cfg.n_tok
    tb = t // SUBLANES
    rev = lambda s: nt - 1 - s
    full = lambda a: pl.BlockSpec(a.shape, lambda s: (0,) * a.ndim)
    tile = pl.BlockSpec((t, r), lambda s: (rev(s), 0))
    tab = lambda width: pl.BlockSpec((1, 1, width), lambda s: (_tab_row(rev(s), cfg), 0, 0))

    def prev_map(s):
        b, k = _lat_pos(rev(s), cfg)
        return (b, jnp.maximum(k * tb - 1, 0), 0)

    def next_map(s):
        b, k = _lat_pos(rev(s), cfg)
        return (b, jnp.minimum((k + 1) * tb, cfg.n_lat // SUBLANES - 1), 0)

    return pl.pallas_call(
        functools.partial(_scan_bwd_kernel, cfg=cfg),
        grid=(nt,),
        in_specs=_x_specs(cfg, rev) + [
            pl.BlockSpec((1, SUBLANES, d), prev_map), pl.BlockSpec((1, SUBLANES, d), next_map),
            tab(N_MOD * d), full(g1), full(wv), full(cw), full(cb), full(wbd), full(br), full(bi), full(lam),
            tab(r)],
        out_specs=[tile, tile,
                   pl.BlockSpec((1, 1, r), lambda s: (jnp.minimum(rev(s), cfg.ntc - 1), 0, 0))],
        out_shape=[jax.ShapeDtypeStruct((n, r), F32),
                   jax.ShapeDtypeStruct((n, r), F32),
                   jax.ShapeDtypeStruct((cfg.n_ctx_req, 1, r), F32)],
        scratch_shapes=[pltpu.VMEM((t, r), F32), pltpu.VMEM((t, r), F32), pltpu.VMEM((1, r), F32)],
        compiler_params=_cparams(),
        name="scan_bwd",
    )(xc, xl, xl, xl, mod, g1, wv, cw, cb, wbd, br, bi, lam, h0)


def _pool(pv, is_ctx, cfg):
    t = pv.shape[0]
    gw = cfg.gw
    row = lax.broadcasted_iota(I32, (t, gw), 0)
    pos = jnp.where(is_ctx, row % cfg.n_ctx, row % cfg.grid_w)
    length = jnp.where(is_ctx, cfg.n_ctx, cfg.grid_w)

    def shifted(x, dlt):
        ok = jnp.logical_and(pos + dlt >= 0, pos + dlt < length)
        return jnp.where(ok, pltpu.roll(x, (-dlt) % t, 0), 0.0)

    outs = []
    for g, m in enumerate(POOL_HALF_WINDOWS):
        x = pv[:, g * gw:(g + 1) * gw]
        back, fwd, k = x, x, 1
        while k < m:
            back = back + shifted(back, -k)
            fwd = fwd + shifted(fwd, k)
            k *= 2
        s = shifted(back, -1) + fwd
        cnt = (jnp.minimum(pos + m, length) - jnp.maximum(pos - m, 0)).astype(F32)
        outs.append(s / cnt - x)
    return outs


def _scan_fwd_kernel(xc_ref, xl_ref, vc_ref, hb_ref, mod_ref, h0_ref, g1_ref, wp_ref, wg_ref, wbd_ref, br_ref,
                     bi_ref, lam_ref, pw_ref, ps_ref, wbp_ref, wbl_ref, wo_ref, g2_ref, rwa_ref, rwb_ref,
                     ew1_ref, ew3_ref, ew2_ref,
                     x1_ref, hn2_ref, lg_ref, hlast_ref, ew1b_ref, ew3b_ref, ew2b_ref,
                     a_scr, u_scr, hf_scr, carry_scr, *, cfg):
    i = pl.program_id(0)
    d, gw = cfg.d, cfg.gw
    is_ctx, first, _ = _tile_flags(i, cfg)

    ew1b_ref[0] = ew1_ref[0, 0].astype(BF16)
    ew3b_ref[0] = ew3_ref[0, 0].astype(BF16)
    ew2b_ref[0] = ew2_ref[0, 0].astype(BF16)

    a, u = _gates(vc_ref[...], wbd_ref, br_ref[...], bi_ref[...], lam_ref[...], cfg)
    a_scr[...] = a
    u_scr[...] = u

    @pl.when(first)
    def _():
        carry_scr[...] = h0_ref[0]

    h = _scan(a_scr, u_scr, hf_scr, carry_scr[...], False)
    carry_scr[...] = h

    @pl.when(is_ctx)
    def _():
        hlast_ref[0] = h

    y_lru = (hf_scr[...] + hb_ref[...]).astype(BF16)

    m = mod_ref[0]
    x = jnp.where(is_ctx, xc_ref[0], xl_ref[0])
    hn = _norm_mod(x, g1_ref[...], m[:, d:2 * d], m[:, 0:d]).astype(BF16)
    pooled = _pool(jnp.dot(hn, wp_ref[...], preferred_element_type=F32), is_ctx, cfg)
    ps = ps_ref[...]
    y_pool = jnp.concatenate(
        [jnp.dot(pooled[g].astype(BF16), pw_ref[g], preferred_element_type=F32) * ps[:, g * gw:(g + 1) * gw]
         for g in range(len(POOL_HALF_WINDOWS))], axis=1).astype(BF16)

    merged = (_sigmoid(jnp.dot(hn, wg_ref[:, :d], preferred_element_type=F32))
              * jnp.dot(y_pool, wbp_ref[...], preferred_element_type=F32)
              + _sigmoid(jnp.dot(hn, wg_ref[:, d:], preferred_element_type=F32))
              * jnp.dot(y_lru, wbl_ref[...], preferred_element_type=F32))
    mix = jnp.dot(merged.astype(BF16), wo_ref[...], preferred_element_type=F32)
    x1 = x + m[:, 2 * d:3 * d] * mix
    x1_ref[...] = x1
    hn2 = _norm_mod(x1, g2_ref[...], m[:, 4 * d:5 * d], m[:, 3 * d:4 * d])
    hi = hn2.astype(BF16)
    hn2_ref[...] = hi
    lo = (hn2 - hi.astype(F32)).astype(BF16)
    lg_ref[...] = (jnp.dot(hi, rwa_ref[...], preferred_element_type=F32)
                   + jnp.dot(hi, rwb_ref[...], preferred_element_type=F32)
                   + jnp.dot(lo, rwa_ref[...], preferred_element_type=F32))


def _scan_fwd(xc, xl, vc, hb, mod, h0, g1, wp, wg, wbd, br, bi, lam, pw, ps, wbp, wbl, wo, g2, rwa, rwb,
              ew1, ew3, ew2, layer, cfg):
    t, d, r = cfg.t, cfg.d, cfg.r
    n = cfg.n_tok
    nlt = cfg.nt - cfg.ntc
    tile = lambda width: pl.BlockSpec((t, width), lambda i: (i, 0))
    full = lambda a: pl.BlockSpec(a.shape, lambda i: (0,) * a.ndim)
    tab = lambda width: pl.BlockSpec((1, 1, width), lambda i: (_tab_row(i, cfg), 0, 0))
    chunk_of = lambda i: jnp.maximum(i - cfg.ntc, 0)

    def chunked(w):
        depth, e, rows, cols = w.shape
        assert (e * rows) % (nlt * BF16_ROWS) == 0
        return w.reshape(depth, nlt, e * rows // nlt, cols)

    ews = [chunked(w) for w in (ew1, ew3, ew2)]
    outs = pl.pallas_call(
        functools.partial(_scan_fwd_kernel, cfg=cfg),
        grid=(cfg.nt,),
        in_specs=_x_specs(cfg, lambda i: i) + [
            tile(r), tile(r), tab(N_MOD * d), tab(r), full(g1), full(wp), full(wg),
            full(wbd), full(br), full(bi), full(lam), full(pw), full(ps), full(wbp), full(wbl),
            full(wo), full(g2), full(rwa), full(rwb)] + [
            pl.BlockSpec((1, 1) + w.shape[2:], lambda i: (layer, chunk_of(i), 0, 0)) for w in ews],
        out_specs=[tile(d), tile(d), tile(cfg.e),
                   pl.BlockSpec((1, 1, r), lambda i: (jnp.minimum(i, cfg.ntc - 1), 0, 0))] + [
            pl.BlockSpec((1,) + w.shape[2:], lambda i: (chunk_of(i), 0, 0)) for w in ews],
        out_shape=[jax.ShapeDtypeStruct((n, d), F32),
                   jax.ShapeDtypeStruct((n, d), BF16),
                   jax.ShapeDtypeStruct((n, cfg.e), F32),
                   jax.ShapeDtypeStruct((cfg.n_ctx_req, 1, r), F32)] + [
            jax.ShapeDtypeStruct(w.shape[1:], BF16) for w in ews],
        scratch_shapes=[pltpu.VMEM((t, r), F32), pltpu.VMEM((t, r), F32), pltpu.VMEM((t, r), F32),
                        pltpu.VMEM((1, r), F32)],
        compiler_params=_cparams(),
        name="scan_fwd",
    )(xc, xl, vc, hb, mod, h0, g1, wp, wg, wbd, br, bi, lam, pw, ps, wbp, wbl, wo, g2, rwa, rwb, *ews)
    return list(outs[:4]) + [o.reshape(w.shape[1:]) for o, w in zip(outs[4:], (ew1, ew3, ew2))]


def _route_kernel(l_ref, selpos_ref, gate_ref, offs_ref, cnts_ref, *, cap, t):
    lg = l_ref[...]
    e, n = lg.shape
    nch = n // t
    ex = jnp.exp(lg - jnp.max(lg, axis=0, keepdims=True))
    aff = ex / jnp.sum(ex, axis=0, keepdims=True)
    gate_ref[...] = aff
    keys = pltpu.bitcast(aff, I32)

    def count(mask):
        return jnp.sum(mask.astype(F32), axis=1, keepdims=True)

    thr = jnp.zeros((e, 1), I32)
    for bit in range(30, -1, -1):
        cand = thr | (1 << bit)
        thr = jnp.where(count(keys >= cand) >= cap, cand, thr)
    gt = keys > thr
    eq = keys == thr
    need = cap - count(gt)

    tri = (lax.broadcasted_iota(I32, (t, t), 0) < lax.broadcasted_iota(I32, (t, t), 1)).astype(BF16)
    tok = lax.broadcasted_iota(I32, (n, LANES), 0)
    chunk = lax.broadcasted_iota(I32, (n, LANES), 1)
    before = (tok < chunk * t).astype(BF16)
    inside = (tok // t == chunk).astype(BF16)

    def prefix(mask):
        mb = mask.astype(BF16)
        offs = jnp.dot(mb, before, preferred_element_type=F32)
        pre = [jnp.dot(mb[:, k * t:(k + 1) * t], tri, preferred_element_type=F32) + offs[:, k:k + 1]
               for k in range(nch)]
        return jnp.concatenate(pre, axis=1) if nch > 1 else pre[0], offs, mb

    eq_rank, _, _ = prefix(eq)
    sel = jnp.logical_or(gt, jnp.logical_and(eq, eq_rank < need))
    pos, offs, selb = prefix(sel)
    selpos_ref[...] = jnp.where(sel, pos.astype(I32), NOT_SELECTED)
    offs_ref[0] = offs.astype(I32)
    cnts_ref[0] = jnp.dot(selb, inside, preferred_element_type=F32).astype(I32)


def _route(logits_t, n_req, n, col0, cap, cfg):
    e = cfg.e
    return pl.pallas_call(
        functools.partial(_route_kernel, cap=cap, t=cfg.t),
        grid=(n_req,),
        in_specs=[pl.BlockSpec((e, n), lambda b: (0, col0 + b))],
        out_specs=[pl.BlockSpec((e, n), lambda b: (0, b)),
                   pl.BlockSpec((e, n), lambda b: (0, b)),
                   pl.BlockSpec((1, e, LANES), lambda b: (b, 0, 0)),
                   pl.BlockSpec((1, e, LANES), lambda b: (b, 0, 0))],
        out_shape=[jax.ShapeDtypeStruct((e, n_req * n), I32),
                   jax.ShapeDtypeStruct((e, n_req * n), F32),
                   jax.ShapeDtypeStruct((n_req, e, LANES), I32),
                   jax.ShapeDtypeStruct((n_req, e, LANES), I32)],
        compiler_params=_cparams(),
        name="route",
    )(logits_t)


def _slot_block(i, cfg):
    return jnp.where(i < cfg.ntc, i // cfg.rpb, cfg.ntc // cfg.rpb + (i - cfg.ntc) // cfg.tpr)


def _windows(i, toff_ref, tcnt_ref, cfg):
    is_ctx = i < cfg.ntc
    base = jnp.where(is_ctx, (i % cfg.rpb) * cfg.c_ctx, 0)
    starts, npass = [], 0
    for e in range(cfg.e):
        off = base + toff_ref[i * cfg.e + e]
        start = (off // BF16_ROWS) * BF16_ROWS
        starts.append(start)
        npass = jnp.maximum(npass, (off + tcnt_ref[i * cfg.e + e] - start + cfg.w - 1) // cfg.w)
    return base, starts, npass


def _window(start, j, cfg):
    want = start + j * cfg.w
    a = pl.multiple_of(jnp.minimum(want, cfg.sblk - cfg.w), BF16_ROWS)
    return a, want - a


def _dispatch_kernel(toff_ref, tcnt_ref, x_ref, sp_ref, g_ref, xs_ref, gs_ref, p_scr, *, cfg):
    i = pl.program_id(0)
    w = cfg.w
    is_ctx = i < cfg.ntc
    new_block = jnp.where(is_ctx, i % cfg.rpb == 0, jnp.maximum(i - cfg.ntc, 0) % cfg.tpr == 0)

    @pl.when(new_block)
    def _():
        xs_ref[...] = jnp.zeros(xs_ref.shape, xs_ref.dtype)
        gs_ref[...] = jnp.zeros(gs_ref.shape, gs_ref.dtype)

    base, starts, npass = _windows(i, toff_ref, tcnt_ref, cfg)
    slot = lax.broadcasted_iota(I32, (w, cfg.t), 0)

    def one_pass(j, carry):
        firsts = []
        for e in range(cfg.e):
            a, owned = _window(starts[e], j, cfg)
            firsts.append(a)
            rel = sp_ref[e:e + 1, :] - (a - base)
            hit = jnp.logical_and(rel == slot, slot >= owned)
            p_scr[e * w:(e + 1) * w, :] = hit.astype(BF16)
            gs_ref[e, pl.ds(a, w), :] += jnp.sum(jnp.where(hit, g_ref[e:e + 1, :], 0.0), axis=1, keepdims=True)
        rows = jnp.dot(p_scr[...], x_ref[...], preferred_element_type=F32)
        for e in range(cfg.e):
            cur = xs_ref[e, pl.ds(firsts[e], w), :].astype(F32)
            xs_ref[e, pl.ds(firsts[e], w), :] = (cur + rows[e * w:(e + 1) * w]).astype(BF16)
        return carry

    lax.fori_loop(0, npass, one_pass, 0)


def _dispatch(toff, tcnt, hn2, selpos, gate, cfg):
    t, d, e = cfg.t, cfg.d, cfg.e
    s = cfg.n_sblk * cfg.sblk
    slots = lambda width: pl.BlockSpec((e, cfg.sblk, width), lambda i, *_: (0, _slot_block(i, cfg), 0))
    return pl.pallas_call(
        functools.partial(_dispatch_kernel, cfg=cfg),
        grid_spec=pltpu.PrefetchScalarGridSpec(
            num_scalar_prefetch=2,
            grid=(cfg.nt,),
            in_specs=[pl.BlockSpec((t, d), lambda i, *_: (i, 0)),
                      pl.BlockSpec((e, t), lambda i, *_: (0, i)),
                      pl.BlockSpec((e, t), lambda i, *_: (0, i))],
            out_specs=[slots(d), slots(1)],
            scratch_shapes=[pltpu.VMEM((e * cfg.w, t), BF16)]),
        out_shape=[jax.ShapeDtypeStruct((e, s, d), BF16), jax.ShapeDtypeStruct((e, s, 1), F32)],
        compiler_params=_cparams(),
        name="dispatch",
    )(toff, tcnt, hn2, selpos, gate)


def _ffn_kernel(x_ref, gs_ref, w1_ref, w3_ref, w2_ref, o_ref, *, fc):
    x = x_ref[0]
    acc = jnp.zeros((x.shape[0], o_ref.shape[2]), F32)
    for c in range(w1_ref.shape[2] // fc):
        sl = slice(c * fc, (c + 1) * fc)
        h1 = jnp.dot(x, w1_ref[0, :, sl], preferred_element_type=F32)
        h3 = jnp.dot(x, w3_ref[0, :, sl], preferred_element_type=F32)
        hid = (h1 * _sigmoid(h1)) * h3
        acc = acc + jnp.dot(hid.astype(BF16), w2_ref[0, sl, :], preferred_element_type=F32)
    o_ref[0] = (acc * gs_ref[0]).astype(BF16)


def _ffn(xs, gs, w1, w3, w2, cfg):
    e, s, d = xs.shape
    f = w1.shape[2]
    tm = cfg.sblk
    return pl.pallas_call(
        functools.partial(_ffn_kernel, fc=min(512, f)),
        grid=(e, s // tm),
        in_specs=[pl.BlockSpec((1, tm, d), lambda k, m: (k, m, 0)),
                  pl.BlockSpec((1, tm, 1), lambda k, m: (k, m, 0)),
                  pl.BlockSpec((1, d, f), lambda k, m: (k, 0, 0)),
                  pl.BlockSpec((1, d, f), lambda k, m: (k, 0, 0)),
                  pl.BlockSpec((1, f, d), lambda k, m: (k, 0, 0))],
        out_specs=pl.BlockSpec((1, tm, d), lambda k, m: (k, m, 0)),
        out_shape=jax.ShapeDtypeStruct((e, s, d), BF16),
        compiler_params=pltpu.CompilerParams(dimension_semantics=("arbitrary", "arbitrary"),
                                             vmem_limit_bytes=VMEM_LIMIT),
        name="ffn",
    )(xs, gs, w1, w3, w2)


def _combine_kernel(toff_ref, tcnt_ref, x1_ref, mod_ref, spt_ref, o_ref, fg_ref, oc_ref, ol_ref, ow_scr,
                    *, cfg, final):
    i = pl.program_id(0)
    w, d = cfg.w, cfg.d
    base, starts, npass = _windows(i, toff_ref, tcnt_ref, cfg)
    ne = cfg.e
    expert = lax.broadcasted_iota(I32, (1, ne), 1)
    expand = (lax.broadcasted_iota(I32, (ne, ne * w), 1) // w
              == lax.broadcasted_iota(I32, (ne, ne * w), 0)).astype(BF16)
    row_in_window = (lax.broadcasted_iota(I32, (cfg.t, ne * w), 1) % w).astype(F32)
    spt = spt_ref[...]

    def one_pass(j, y):
        shift = jnp.zeros((1, ne), I32)
        owned = jnp.zeros((1, ne), I32)
        for e in range(ne):
            a, own = _window(starts[e], j, cfg)
            ow_scr[e * w:(e + 1) * w, :] = o_ref[e, pl.ds(a, w), :]
            shift = jnp.where(expert == e, a - base, shift)
            owned = jnp.where(expert == e, own, owned)
        rel = spt - shift
        rel = jnp.where(jnp.logical_and(rel >= owned, rel < w), rel, -1)
        wide = jnp.dot(rel.astype(F32).astype(BF16), expand, preferred_element_type=F32)
        hits = (wide == row_in_window).astype(BF16)
        return y + jnp.dot(hits, ow_scr[...], preferred_element_type=F32)

    y = lax.fori_loop(0, npass, one_pass, jnp.zeros((cfg.t, d), F32))
    x2 = x1_ref[...] + mod_ref[0][:, 5 * d:6 * d] * y
    if final:
        x2 = (x2 * lax.rsqrt(jnp.mean(x2 * x2, axis=-1, keepdims=True) + EPS)) * fg_ref[...]
    is_ctx = i < cfg.ntc

    @pl.when(is_ctx)
    def _():
        oc_ref[0] = x2

    @pl.when(jnp.logical_not(is_ctx))
    def _():
        ol_ref[0] = x2


def _combine(toff, tcnt, x1, mod, selpos_t, o, fg, cfg, final):
    t, d, e = cfg.t, cfg.d, cfg.e
    return pl.pallas_call(
        functools.partial(_combine_kernel, cfg=cfg, final=final),
        grid_spec=pltpu.PrefetchScalarGridSpec(
            num_scalar_prefetch=2,
            grid=(cfg.nt,),
            in_specs=[pl.BlockSpec((t, d), lambda i, *_: (i, 0)),
                      pl.BlockSpec((1, 1, N_MOD * d), lambda i, *_: (_tab_row(i, cfg), 0, 0)),
                      pl.BlockSpec((t, e), lambda i, *_: (i, 0)),
                      pl.BlockSpec((e, cfg.sblk, d), lambda i, *_: (0, _slot_block(i, cfg), 0)),
                      pl.BlockSpec((1, d), lambda i, *_: (0, 0))],
            out_specs=_x_specs(cfg, lambda i: i),
            scratch_shapes=[pltpu.VMEM((e * cfg.w, d), BF16)]),
        out_shape=[jax.ShapeDtypeStruct((cfg.n_ctx_req, cfg.n_ctx, d), F32),
                   jax.ShapeDtypeStruct((cfg.n_lat_req, cfg.n_lat, d), F32)],
        compiler_params=_cparams(),
        name="combine",
    )(toff, tcnt, x1, mod, selpos_t, o, fg)


def _block_diag_gates(wr, wi, cfg):
    hpb = cfg.bd // cfg.dh
    nblk = cfg.r // cfg.bd
    eye = jnp.eye(hpb, dtype=wr.dtype)

    def blocks(w):
        w4 = w.reshape(nblk, hpb, cfg.dh, cfg.dh)
        return jnp.einsum('jhde,hk->jhdke', w4, eye).reshape(nblk, cfg.bd, cfg.bd)

    return jnp.concatenate([blocks(wr), blocks(wi)], axis=2).astype(BF16)


def _tile_tables(offs_ctx, offs_lat, cfg):
    ctx = offs_ctx[:, :, 0]
    lat = jnp.swapaxes(offs_lat[:, :, :cfg.tpr], 1, 2).reshape(-1, cfg.e)
    return jnp.concatenate([ctx, lat], axis=0).reshape(-1)


def _forward(cfg, x_prompt, x_sample, state_lru, c, c_ctx, norm1_g, norm2_g, final_g, w_mod, b_mod, w_in,
             pool_w, pool_scale, conv_w, conv_b, lru_wr, lru_br, lru_wi, lru_bi, lru_lambda,
             w_br_pool, w_br_lru, w_out, router_w, exp_w1, exp_w3, exp_w2):
    d, r, p = cfg.d, cfg.r, cfg.p
    depth = w_in.shape[0]
    assert cfg.n_lat % cfg.t == 0 and cfg.t % cfg.grid_w == 0 and cfg.n_ctx_req % cfg.rpb == 0
    assert (cfg.n_ctx_req * cfg.n_ctx) % cfg.n_lat == 0 and cfg.w % BF16_ROWS == 0

    n_ctx_tok = cfg.n_ctx_req * cfg.n_ctx
    xc, xl = x_prompt, x_sample

    cvec = jnp.zeros((cfg.n_tab, d), F32).at[0].set(c_ctx).at[1:1 + cfg.n_lat_req].set(c)
    mod = _modulation(cvec, w_mod, b_mod, cfg)

    states = []
    for l in range(depth):
        mod_l = mod[l].reshape(cfg.n_tab, 1, N_MOD * d)
        row = lambda a: a.reshape(1, -1)
        wp = w_in[l][:, :p].astype(BF16)
        wv = w_in[l][:, p:p + r].astype(BF16)
        wg = w_in[l][:, p + r:].astype(BF16)
        rwa = router_w[l].astype(BF16)
        rwb = (router_w[l] - rwa.astype(F32)).astype(BF16)

        def h0(direction):
            tab = jnp.zeros((cfg.n_tab, 1, r), F32)
            return tab.at[1:1 + cfg.n_lat_req, 0].set(state_lru[:, l, direction].astype(F32))

        wbd = [_block_diag_gates(lru_wr[l, z], lru_wi[l, z], cfg) for z in range(2)]
        vc, hb, hb_last = _scan_bwd(xc, xl, mod_l, row(norm1_g[l]), wv, conv_w[l], row(conv_b[l]), wbd[1],
                                    row(lru_br[l, 1]), row(lru_bi[l, 1]), row(lru_lambda[l, 1]), h0(1), cfg)
        x1, hn2, logits, hf_last, w1b, w3b, w2b = _scan_fwd(
            xc, xl, vc, hb, mod_l, h0(0), row(norm1_g[l]), wp, wg, wbd[0], row(lru_br[l, 0]), row(lru_bi[l, 0]),
            row(lru_lambda[l, 0]), pool_w[l].astype(BF16), row(pool_scale[l]), w_br_pool[l].astype(BF16),
            w_br_lru[l].astype(BF16), w_out[l].astype(BF16), row(norm2_g[l]), rwa, rwb,
            exp_w1, exp_w3, exp_w2, l, cfg)
        states.append(jnp.stack([hf_last[:, 0], hb_last[:, 0]], axis=1))

        logits_t = logits.T
        sp_c, g_c, off_c, cnt_c = _route(logits_t, cfg.n_ctx_req, cfg.n_ctx, 0, cfg.c_ctx, cfg)
        sp_l, g_l, off_l, cnt_l = _route(logits_t, cfg.n_lat_req, cfg.n_lat, n_ctx_tok // cfg.n_lat,
                                         cfg.c_lat, cfg)
        selpos = jnp.concatenate([sp_c, sp_l], axis=1)
        gate = jnp.concatenate([g_c, g_l], axis=1)
        toff = _tile_tables(off_c, off_l, cfg)
        tcnt = _tile_tables(cnt_c, cnt_l, cfg)

        xs, gs = _dispatch(toff, tcnt, hn2, selpos, gate, cfg)
        o = _ffn(xs, gs, w1b, w3b, w2b, cfg)
        xc, xl = _combine(toff, tcnt, x1, mod_l, selpos.T, o, row(final_g), cfg, l == depth - 1)

    new_state = jnp.stack(states, axis=1).astype(x_prompt.dtype)
    return xc, xl, new_state


def kernel(x_prompt, x_sample, state_lru, c, c_ctx, norm1_g, norm2_g, final_g, w_mod, b_mod, w_in, pool_w,
           pool_scale, conv_w, conv_b, lru_wr, lru_br, lru_wi, lru_bi, lru_lambda, w_br_pool, w_br_lru, w_out,
           router_w, exp_w1, exp_w3, exp_w2):
    cfg = Cfg(d=x_prompt.shape[2], gw=pool_w.shape[2], r=lru_lambda.shape[2], dh=lru_wr.shape[3],
              e=router_w.shape[2], f=exp_w1.shape[3], n_ctx_req=x_prompt.shape[0], n_ctx=x_prompt.shape[1],
              n_lat_req=x_sample.shape[0], n_lat=x_sample.shape[1], grid_w=GRID_W)
    return _forward(cfg, x_prompt, x_sample, state_lru, c, c_ctx, norm1_g, norm2_g, final_g, w_mod, b_mod, w_in,
                    pool_w, pool_scale, conv_w, conv_b, lru_wr, lru_br, lru_wi, lru_bi, lru_lambda,
                    w_br_pool, w_br_lru, w_out, router_w, exp_w1, exp_w3, exp_w2)
```

```python
import functools
from typing import NamedTuple

import jax
import jax.numpy as jnp
from jax import lax
from jax.experimental import pallas as pl
from jax.experimental.pallas import tpu as pltpu

F32 = jnp.float32
BF16 = jnp.bfloat16
I32 = jnp.int32
HIGHEST = lax.Precision.HIGHEST

EPS = 1e-6
LRU_C = 8.0
CONV_WIDTH = 4
POOL_HALF_WINDOWS = (1, 2, 4, 8)
N_MOD = 6
CAPACITY_FACTOR = 2
GRID_W = 64
NOT_SELECTED = -(1 << 20)
SUBLANES = 8
BF16_ROWS = 16
LANES = 128
MXU_DIM = 256
VMEM_LIMIT = 56 * 1024 * 1024


class Cfg(NamedTuple):
    d: int
    gw: int
    r: int
    dh: int
    e: int
    f: int
    n_ctx_req: int
    n_ctx: int
    n_lat_req: int
    n_lat: int
    grid_w: int

    @property
    def t(self): return self.n_ctx
    @property
    def p(self): return self.gw * len(POOL_HALF_WINDOWS)
    @property
    def ntc(self): return self.n_ctx_req
    @property
    def tpr(self): return self.n_lat // self.t
    @property
    def nt(self): return self.ntc + self.n_lat_req * self.tpr
    @property
    def n_tok(self): return self.nt * self.t
    @property
    def c_ctx(self): return CAPACITY_FACTOR * self.n_ctx // self.e
    @property
    def c_lat(self): return CAPACITY_FACTOR * self.n_lat // self.e
    @property
    def sblk(self): return self.c_lat
    @property
    def rpb(self): return self.c_lat // self.c_ctx
    @property
    def n_sblk(self): return self.n_ctx_req // self.rpb + self.n_lat_req
    @property
    def w(self): return min(64, self.sblk)
    @property
    def bd(self): return min(MXU_DIM, self.r)
    @property
    def n_tab(self): return 16


def _cparams():
    return pltpu.CompilerParams(dimension_semantics=("arbitrary",), vmem_limit_bytes=VMEM_LIMIT)


def _tab_row(i, cfg):
    return jnp.where(i < cfg.ntc, 0, 1 + (i - cfg.ntc) // cfg.tpr)


def _sigmoid(x):
    return 0.5 * jnp.tanh(0.5 * x) + 0.5


def _lat_pos(i, cfg):
    j = jnp.maximum(i - cfg.ntc, 0)
    return j // cfg.tpr, j % cfg.tpr


def _x_specs(cfg, tile_of):
    def ctx_map(s, *_):
        return (jnp.minimum(tile_of(s), cfg.ntc - 1), 0, 0)

    def lat_map(s, *_):
        b, k = _lat_pos(tile_of(s), cfg)
        return (b, k, 0)

    return [pl.BlockSpec((1, cfg.t, cfg.d), ctx_map), pl.BlockSpec((1, cfg.t, cfg.d), lat_map)]


def _norm_mod(x, g, scale, shift):
    y = x * lax.rsqrt(jnp.mean(x * x, axis=-1, keepdims=True) + EPS)
    return (y * g) * (1.0 + scale) + shift


def _mod_kernel(c_ref, w_ref, b_ref, o_ref):
    c = c_ref[...]
    s = c * _sigmoid(c)
    o_ref[0] = jnp.dot(s, w_ref[0], precision=HIGHEST, preferred_element_type=F32) + b_ref[0]


def _modulation(cvec, w_mod, b_mod, cfg):
    depth, d, n6 = w_mod.shape
    nc = n6 // 4
    return pl.pallas_call(
        _mod_kernel,
        grid=(depth, n6 // nc),
        in_specs=[pl.BlockSpec((cfg.n_tab, d), lambda l, j: (0, 0)),
                  pl.BlockSpec((1, d, nc), lambda l, j: (l, 0, j)),
                  pl.BlockSpec((1, 1, nc), lambda l, j: (l, 0, j))],
        out_specs=pl.BlockSpec((1, cfg.n_tab, nc), lambda l, j: (l, 0, j)),
        out_shape=jax.ShapeDtypeStruct((depth, cfg.n_tab, n6), F32),
        compiler_params=pltpu.CompilerParams(dimension_semantics=("arbitrary", "arbitrary"),
                                             vmem_limit_bytes=VMEM_LIMIT),
        name="modulation",
    )(cvec, w_mod, b_mod.reshape(depth, 1, n6))


def _conv(v, prev, nxt, w, b):
    t = v.shape[0]
    row = lax.broadcasted_iota(I32, v.shape, 0)
    vm1 = jnp.where(row == 0, prev[7:8], pltpu.roll(v, 1, 0))
    vm2 = jnp.where(row == 0, prev[6:7], jnp.where(row == 1, prev[7:8], pltpu.roll(v, 2, 0)))
    vp1 = jnp.where(row == t - 1, nxt[0:1], pltpu.roll(v, t - 1, 0))
    return b + vm2 * w[0:1] + vm1 * w[1:2] + v * w[2:3] + vp1 * w[3:4]


def _gates(vc, wbd_ref, br, bi, lam, a_scr, u_scr, cfg, between=None):
    bd = cfg.bd
    vcb = vc.astype(BF16)
    nl = -lam
    rate = LRU_C * (jnp.maximum(nl, 0.0) + jnp.log1p(jnp.exp(-jnp.abs(nl))))
    for j in range(cfg.r // bd):
        sl = slice(j * bd, (j + 1) * bd)
        z = jnp.dot(vcb[:, sl], wbd_ref[j], preferred_element_type=F32)
        if between is not None:
            between(j)
        rg = _sigmoid(z[:, :bd] + br[:, sl])
        ig = _sigmoid(z[:, bd:] + bi[:, sl])
        m = rg * rate[:, sl]
        a = jnp.exp(-m)
        a_scr[:, sl] = a
        u_scr[:, sl] = jnp.sqrt(jnp.tanh(m) * (a * a + 1.0)) * ig * vc[:, sl]


def _scan(a_scr, u_scr, out_ref, h_in, reverse):
    groups = a_scr.shape[0] // SUBLANES
    row = lax.broadcasted_iota(I32, (SUBLANES, a_scr.shape[1]), 0)
    steps = [((row < SUBLANES - s) if reverse else (row >= s), SUBLANES - s if reverse else s)
             for s in (1, 2, 4)]

    def body(j, h):
        g = groups - 1 - j if reverse else j
        r0 = pl.multiple_of(g * SUBLANES, SUBLANES)
        a = a_scr[pl.ds(r0, SUBLANES), :]
        u = u_scr[pl.ds(r0, SUBLANES), :]
        for ok, shift in steps:
            u = u + a * jnp.where(ok, pltpu.roll(u, shift, 0), 0.0)
            a = a * jnp.where(ok, pltpu.roll(a, shift, 0), 1.0)
        hg = a * h + u
        out_ref[pl.ds(r0, SUBLANES), :] = hg
        return hg[0:1] if reverse else hg[SUBLANES - 1:SUBLANES]

    return lax.fori_loop(0, groups, body, h_in, unroll=4)


def _tile_flags(i, cfg):
    is_ctx = i < cfg.ntc
    k = jnp.maximum(i - cfg.ntc, 0) % cfg.tpr
    first = jnp.logical_or(is_ctx, k == 0)
    last = jnp.logical_or(is_ctx, k == cfg.tpr - 1)
    return is_ctx, first, last


def _scan_bwd_kernel(xc_ref, xl_ref, xprev_ref, xnext_ref, mod_ref, g1_ref, wv_ref, cw_ref, cb_ref, wbd_ref,
                     br_ref, bi_ref, lam_ref, h0_ref, vc_ref, hb_ref, hlast_ref, a_scr, u_scr, carry_scr,
                     *, cfg):
    i = cfg.nt - 1 - pl.program_id(0)
    t, d = cfg.t, cfg.d
    is_ctx, first, last = _tile_flags(i, cfg)
    x_ext = jnp.concatenate([xprev_ref[0], jnp.where(is_ctx, xc_ref[0], xl_ref[0]), xnext_ref[0]], axis=0)
    m = mod_ref[0]
    hn = _norm_mod(x_ext, g1_ref[...], m[:, d:2 * d], m[:, 0:d]).astype(BF16)
    v_ext = jnp.dot(hn, wv_ref[...], preferred_element_type=F32)
    prev = jnp.where(first, 0.0, v_ext[0:SUBLANES])
    nxt = jnp.where(last, 0.0, v_ext[t + SUBLANES:t + 2 * SUBLANES])
    vc = _conv(v_ext[SUBLANES:t + SUBLANES], prev, nxt, cw_ref[...], cb_ref[...])
    vc_ref[...] = vc
    _gates(vc, wbd_ref, br_ref[...], bi_ref[...], lam_ref[...], a_scr, u_scr, cfg)

    @pl.when(last)
    def _():
        carry_scr[...] = h0_ref[0]

    h = _scan(a_scr, u_scr, hb_ref, carry_scr[...], True)
    carry_scr[...] = h

    @pl.when(is_ctx)
    def _():
        hlast_ref[0] = h


def _scan_bwd(xc, xl, mod, g1, wv, cw, cb, wbd, br, bi, lam, h0, cfg):
    t, d, r, nt = cfg.t, cfg.d, cfg.r, cfg.nt
    n = cfg.n_tok
    tb = t // SUBLANES
    rev = lambda s: nt - 1 - s
    full = lambda a: pl.BlockSpec(a.shape, lambda s: (0,) * a.ndim)
    tile = pl.BlockSpec((t, r), lambda s: (rev(s), 0))
    tab = lambda width: pl.BlockSpec((1, 1, width), lambda s: (_tab_row(rev(s), cfg), 0, 0))

    def prev_map(s):
        b, k = _lat_pos(rev(s), cfg)
        return (b, jnp.maximum(k * tb - 1, 0), 0)

    def next_map(s):
        b, k = _lat_pos(rev(s), cfg)
        return (b, jnp.minimum((k + 1) * tb, cfg.n_lat // SUBLANES - 1), 0)

    return pl.pallas_call(
        functools.partial(_scan_bwd_kernel, cfg=cfg),
        grid=(nt,),
        in_specs=_x_specs(cfg, rev) + [
            pl.BlockSpec((1, SUBLANES, d), prev_map), pl.BlockSpec((1, SUBLANES, d), next_map),
            tab(N_MOD * d), full(g1), full(wv), full(cw), full(cb), full(wbd), full(br), full(bi), full(lam),
            tab(r)],
        out_specs=[tile, tile,
                   pl.BlockSpec((1, 1, r), lambda s: (jnp.minimum(rev(s), cfg.ntc - 1), 0, 0))],
        out_shape=[jax.ShapeDtypeStruct((n, r), F32),
                   jax.ShapeDtypeStruct((n, r), F32),
                   jax.ShapeDtypeStruct((cfg.n_ctx_req, 1, r), F32)],
        scratch_shapes=[pltpu.VMEM((t, r), F32), pltpu.VMEM((t, r), F32), pltpu.VMEM((1, r), F32)],
        compiler_params=_cparams(),
        name="scan_bwd",
    )(xc, xl, xl, xl, mod, g1, wv, cw, cb, wbd, br, bi, lam, h0)


def _pool(pv, is_ctx, cfg):
    t = pv.shape[0]
    gw = cfg.gw
    row = lax.broadcasted_iota(I32, (t, gw), 0)
    pos = jnp.where(is_ctx, row % cfg.n_ctx, row % cfg.grid_w)
    length = jnp.where(is_ctx, cfg.n_ctx, cfg.grid_w)

    def shifted(x, dlt):
        ok = jnp.logical_and(pos + dlt >= 0, pos + dlt < length)
        return jnp.where(ok, pltpu.roll(x, (-dlt) % t, 0), 0.0)

    outs = []
    for g, m in enumerate(POOL_HALF_WINDOWS):
        x = pv[:, g * gw:(g + 1) * gw]
        back, fwd, k = x, x, 1
        while k < m:
            back = back + shifted(back, -k)
            fwd = fwd + shifted(fwd, k)
            k *= 2
        s = shifted(back, -1) + fwd
        cnt = (jnp.minimum(pos + m, length) - jnp.maximum(pos - m, 0)).astype(F32)
        outs.append(s / cnt - x)
    return outs


def _scan_fwd_kernel(xc_ref, xl_ref, vc_ref, hb_ref, mod_ref, h0_ref, g1_ref, wp_ref, wg_ref, wbd_ref, br_ref,
                     bi_ref, lam_ref, pw_ref, ps_ref, wbp_ref, wbl_ref, wo_ref, g2_ref, rwa_ref, rwb_ref,
                     ew1_ref, ew3_ref, ew2_ref,
                     x1_ref, hn2_ref, lg_ref, hlast_ref, ew1b_ref, ew3b_ref, ew2b_ref,
                     a_scr, u_scr, hf_scr, carry_scr, p_scr, gl_scr, *, cfg):
    i = pl.program_id(0)
    d, gw = cfg.d, cfg.gw
    is_ctx, first, _ = _tile_flags(i, cfg)

    ew1b_ref[0] = ew1_ref[0, 0].astype(BF16)
    ew3b_ref[0] = ew3_ref[0, 0].astype(BF16)
    ew2b_ref[0] = ew2_ref[0, 0].astype(BF16)

    m = mod_ref[0]
    hn = _norm_mod(jnp.where(is_ctx, xc_ref[0], xl_ref[0]), g1_ref[...], m[:, d:2 * d], m[:, 0:d]).astype(BF16)
    nblk = cfg.r // cfg.bd
    gcols = 2 * d // nblk

    def project(j):
        if j == 0:
            p_scr[...] = jnp.dot(hn, wp_ref[...], preferred_element_type=F32)
        sl = slice(j * gcols, (j + 1) * gcols)
        gl_scr[:, sl] = jnp.dot(hn, wg_ref[:, sl], preferred_element_type=F32)

    _gates(vc_ref[...], wbd_ref, br_ref[...], bi_ref[...], lam_ref[...], a_scr, u_scr, cfg, between=project)

    @pl.when(first)
    def _():
        carry_scr[...] = h0_ref[0]

    h = _scan(a_scr, u_scr, hf_scr, carry_scr[...], False)
    carry_scr[...] = h

    @pl.when(is_ctx)
    def _():
        hlast_ref[0] = h

    y_lru = (hf_scr[...] + hb_ref[...]).astype(BF16)
    pooled = _pool(p_scr[...], is_ctx, cfg)
    ps = ps_ref[...]
    y_pool = jnp.concatenate(
        [jnp.dot(pooled[g].astype(BF16), pw_ref[g], preferred_element_type=F32) * ps[:, g * gw:(g + 1) * gw]
         for g in range(len(POOL_HALF_WINDOWS))], axis=1).astype(BF16)

    merged = (_sigmoid(gl_scr[:, :d]) * jnp.dot(y_pool, wbp_ref[...], preferred_element_type=F32)
              + _sigmoid(gl_scr[:, d:]) * jnp.dot(y_lru, wbl_ref[...], preferred_element_type=F32))
    mix = jnp.dot(merged.astype(BF16), wo_ref[...], preferred_element_type=F32)
    x1 = jnp.where(is_ctx, xc_ref[0], xl_ref[0]) + m[:, 2 * d:3 * d] * mix
    x1_ref[...] = x1
    hn2 = _norm_mod(x1, g2_ref[...], m[:, 4 * d:5 * d], m[:, 3 * d:4 * d])
    hi = hn2.astype(BF16)
    hn2_ref[...] = hi
    lo = (hn2 - hi.astype(F32)).astype(BF16)
    lg_ref[...] = (jnp.dot(hi, rwa_ref[...], preferred_element_type=F32)
                   + jnp.dot(hi, rwb_ref[...], preferred_element_type=F32)
                   + jnp.dot(lo, rwa_ref[...], preferred_element_type=F32))


def _scan_fwd(xc, xl, vc, hb, mod, h0, g1, wp, wg, wbd, br, bi, lam, pw, ps, wbp, wbl, wo, g2, rwa, rwb,
              ew1, ew3, ew2, layer, cfg):
    t, d, r = cfg.t, cfg.d, cfg.r
    n = cfg.n_tok
    nlt = cfg.nt - cfg.ntc
    tile = lambda width: pl.BlockSpec((t, width), lambda i: (i, 0))
    full = lambda a: pl.BlockSpec(a.shape, lambda i: (0,) * a.ndim)
    tab = lambda width: pl.BlockSpec((1, 1, width), lambda i: (_tab_row(i, cfg), 0, 0))
    chunk_of = lambda i: jnp.maximum(i - cfg.ntc, 0)

    def chunked(w):
        depth, e, rows, cols = w.shape
        assert (e * rows) % (nlt * BF16_ROWS) == 0
        return w.reshape(depth, nlt, e * rows // nlt, cols)

    ews = [chunked(w) for w in (ew1, ew3, ew2)]
    outs = pl.pallas_call(
        functools.partial(_scan_fwd_kernel, cfg=cfg),
        grid=(cfg.nt,),
        in_specs=_x_specs(cfg, lambda i: i) + [
            tile(r), tile(r), tab(N_MOD * d), tab(r), full(g1), full(wp), full(wg),
            full(wbd), full(br), full(bi), full(lam), full(pw), full(ps), full(wbp), full(wbl),
            full(wo), full(g2), full(rwa), full(rwb)] + [
            pl.BlockSpec((1, 1) + w.shape[2:], lambda i: (layer, chunk_of(i), 0, 0)) for w in ews],
        out_specs=[tile(d), tile(d), tile(cfg.e),
                   pl.BlockSpec((1, 1, r), lambda i: (jnp.minimum(i, cfg.ntc - 1), 0, 0))] + [
            pl.BlockSpec((1,) + w.shape[2:], lambda i: (chunk_of(i), 0, 0)) for w in ews],
        out_shape=[jax.ShapeDtypeStruct((n, d), F32),
                   jax.ShapeDtypeStruct((n, d), BF16),
                   jax.ShapeDtypeStruct((n, cfg.e), F32),
                   jax.ShapeDtypeStruct((cfg.n_ctx_req, 1, r), F32)] + [
            jax.ShapeDtypeStruct(w.shape[1:], BF16) for w in ews],
        scratch_shapes=[pltpu.VMEM((t, r), F32), pltpu.VMEM((t, r), F32), pltpu.VMEM((t, r), F32),
                        pltpu.VMEM((1, r), F32), pltpu.VMEM((t, cfg.p), F32), pltpu.VMEM((t, 2 * d), F32)],
        compiler_params=_cparams(),
        name="scan_fwd",
    )(xc, xl, vc, hb, mod, h0, g1, wp, wg, wbd, br, bi, lam, pw, ps, wbp, wbl, wo, g2, rwa, rwb, *ews)
    return list(outs[:4]) + [o.reshape(w.shape[1:]) for o, w in zip(outs[4:], (ew1, ew3, ew2))]


def _route_kernel(l_ref, selpos_ref, gate_ref, offs_ref, cnts_ref, *, cap, t):
    lg = l_ref[...]
    nreq, e, n = lg.shape
    rows = nreq * e
    nch = n // t
    ex = jnp.exp(lg - jnp.max(lg, axis=1, keepdims=True))
    aff = (ex / jnp.sum(ex, axis=1, keepdims=True)).reshape(rows, n)
    gate_ref[...] = aff
    keys = pltpu.bitcast(aff, I32)

    def count(mask):
        return jnp.sum(mask.astype(F32), axis=1, keepdims=True)

    thr = jnp.zeros((rows, 1), I32)
    for bit in range(30, -1, -1):
        cand = thr | (1 << bit)
        thr = jnp.where(count(keys >= cand) >= cap, cand, thr)
    gt = keys > thr
    eq = keys == thr
    need = cap - count(gt)

    tri = (lax.broadcasted_iota(I32, (t, t), 0) < lax.broadcasted_iota(I32, (t, t), 1)).astype(BF16)
    tok = lax.broadcasted_iota(I32, (n, LANES), 0)
    chunk = lax.broadcasted_iota(I32, (n, LANES), 1)
    before = (tok < chunk * t).astype(BF16)
    inside = (tok // t == chunk).astype(BF16)

    def prefix(mask):
        mb = mask.astype(BF16)
        offs = jnp.dot(mb, before, preferred_element_type=F32)
        pre = [jnp.dot(mb[:, k * t:(k + 1) * t], tri, preferred_element_type=F32) + offs[:, k:k + 1]
               for k in range(nch)]
        return jnp.concatenate(pre, axis=1) if nch > 1 else pre[0], offs, mb

    eq_rank, _, _ = prefix(eq)
    sel = jnp.logical_or(gt, jnp.logical_and(eq, eq_rank < need))
    pos, offs, selb = prefix(sel)
    selpos_ref[...] = jnp.where(sel, pos.astype(I32), NOT_SELECTED)
    offs_ref[...] = offs.astype(I32)
    cnts_ref[...] = jnp.dot(selb, inside, preferred_element_type=F32).astype(I32)


def _route(logits, cap, cfg):
    nreq, e, n = logits.shape
    rows = nreq * e
    whole = lambda width: pl.BlockSpec((rows, width), lambda b: (0, 0))
    return pl.pallas_call(
        functools.partial(_route_kernel, cap=cap, t=cfg.t),
        grid=(1,),
        in_specs=[pl.BlockSpec((nreq, e, n), lambda b: (0, 0, 0))],
        out_specs=[whole(n), whole(n), whole(LANES), whole(LANES)],
        out_shape=[jax.ShapeDtypeStruct((rows, n), I32),
                   jax.ShapeDtypeStruct((rows, n), F32),
                   jax.ShapeDtypeStruct((rows, LANES), I32),
                   jax.ShapeDtypeStruct((rows, LANES), I32)],
        compiler_params=_cparams(),
        name="route",
    )(logits)


def _slot_block(i, cfg):
    return jnp.where(i < cfg.ntc, i // cfg.rpb, cfg.ntc // cfg.rpb + (i - cfg.ntc) // cfg.tpr)


def _windows(i, toff_ref, tcnt_ref, cfg):
    is_ctx = i < cfg.ntc
    base = jnp.where(is_ctx, (i % cfg.rpb) * cfg.c_ctx, 0)
    starts, npass = [], 0
    for e in range(cfg.e):
        off = base + toff_ref[i * cfg.e + e]
        start = (off // BF16_ROWS) * BF16_ROWS
        starts.append(start)
        npass = jnp.maximum(npass, (off + tcnt_ref[i * cfg.e + e] - start + cfg.w - 1) // cfg.w)
    return base, starts, npass


def _window(start, j, cfg):
    want = start + j * cfg.w
    a = pl.multiple_of(jnp.minimum(want, cfg.sblk - cfg.w), BF16_ROWS)
    return a, want - a


def _dispatch_kernel(toff_ref, tcnt_ref, x_ref, sp_ref, g_ref, xs_ref, gs_ref, p_scr, *, cfg):
    i = pl.program_id(0)
    w = cfg.w
    is_ctx = i < cfg.ntc
    new_block = jnp.where(is_ctx, i % cfg.rpb == 0, jnp.maximum(i - cfg.ntc, 0) % cfg.tpr == 0)

    @pl.when(new_block)
    def _():
        xs_ref[...] = jnp.zeros(xs_ref.shape, xs_ref.dtype)
        gs_ref[...] = jnp.zeros(gs_ref.shape, gs_ref.dtype)

    base, starts, npass = _windows(i, toff_ref, tcnt_ref, cfg)
    slot = lax.broadcasted_iota(I32, (w, cfg.t), 0)

    def one_pass(j, carry):
        firsts = []
        for e in range(cfg.e):
            a, owned = _window(starts[e], j, cfg)
            firsts.append(a)
            rel = sp_ref[e:e + 1, :] - (a - base)
            hit = jnp.logical_and(rel == slot, slot >= owned)
            p_scr[e * w:(e + 1) * w, :] = hit.astype(BF16)
            gs_ref[e, pl.ds(a, w), :] += jnp.sum(jnp.where(hit, g_ref[e:e + 1, :], 0.0), axis=1, keepdims=True)
        rows = jnp.dot(p_scr[...], x_ref[...], preferred_element_type=F32)
        for e in range(cfg.e):
            cur = xs_ref[e, pl.ds(firsts[e], w), :].astype(F32)
            xs_ref[e, pl.ds(firsts[e], w), :] = (cur + rows[e * w:(e + 1) * w]).astype(BF16)
        return carry

    lax.fori_loop(0, npass, one_pass, 0)


def _dispatch(toff, tcnt, hn2, selpos, gate, cfg):
    t, d, e = cfg.t, cfg.d, cfg.e
    s = cfg.n_sblk * cfg.sblk
    slots = lambda width: pl.BlockSpec((e, cfg.sblk, width), lambda i, *_: (0, _slot_block(i, cfg), 0))
    return pl.pallas_call(
        functools.partial(_dispatch_kernel, cfg=cfg),
        grid_spec=pltpu.PrefetchScalarGridSpec(
            num_scalar_prefetch=2,
            grid=(cfg.nt,),
            in_specs=[pl.BlockSpec((t, d), lambda i, *_: (i, 0)),
                      pl.BlockSpec((e, t), lambda i, *_: (0, i)),
                      pl.BlockSpec((e, t), lambda i, *_: (0, i))],
            out_specs=[slots(d), slots(1)],
            scratch_shapes=[pltpu.VMEM((e * cfg.w, t), BF16)]),
        out_shape=[jax.ShapeDtypeStruct((e, s, d), BF16), jax.ShapeDtypeStruct((e, s, 1), F32)],
        compiler_params=_cparams(),
        name="dispatch",
    )(toff, tcnt, hn2, selpos, gate)


def _ffn_kernel(x_ref, gs_ref, w1_ref, w3_ref, w2_ref, o_ref, *, fc):
    x = x_ref[0]
    acc = jnp.zeros((x.shape[0], o_ref.shape[2]), F32)
    for c in range(w1_ref.shape[2] // fc):
        sl = slice(c * fc, (c + 1) * fc)
        h1 = jnp.dot(x, w1_ref[0, :, sl], preferred_element_type=F32)
        h3 = jnp.dot(x, w3_ref[0, :, sl], preferred_element_type=F32)
        hid = (h1 * _sigmoid(h1)) * h3
        acc = acc + jnp.dot(hid.astype(BF16), w2_ref[0, sl, :], preferred_element_type=F32)
    o_ref[0] = (acc * gs_ref[0]).astype(BF16)


def _ffn(xs, gs, w1, w3, w2, cfg):
    e, s, d = xs.shape
    f = w1.shape[2]
    tm = cfg.sblk
    return pl.pallas_call(
        functools.partial(_ffn_kernel, fc=min(512, f)),
        grid=(e, s // tm),
        in_specs=[pl.BlockSpec((1, tm, d), lambda k, m: (k, m, 0)),
                  pl.BlockSpec((1, tm, 1), lambda k, m: (k, m, 0)),
                  pl.BlockSpec((1, d, f), lambda k, m: (k, 0, 0)),
                  pl.BlockSpec((1, d, f), lambda k, m: (k, 0, 0)),
                  pl.BlockSpec((1, f, d), lambda k, m: (k, 0, 0))],
        out_specs=pl.BlockSpec((1, tm, d), lambda k, m: (k, m, 0)),
        out_shape=jax.ShapeDtypeStruct((e, s, d), BF16),
        compiler_params=pltpu.CompilerParams(dimension_semantics=("arbitrary", "arbitrary"),
                                             vmem_limit_bytes=VMEM_LIMIT),
        name="ffn",
    )(xs, gs, w1, w3, w2)


def _combine_kernel(toff_ref, tcnt_ref, x1_ref, mod_ref, spt_ref, o_ref, fg_ref, oc_ref, ol_ref, ow_scr,
                    *, cfg, final):
    i = pl.program_id(0)
    w, d = cfg.w, cfg.d
    base, starts, npass = _windows(i, toff_ref, tcnt_ref, cfg)
    ne = cfg.e
    expert = lax.broadcasted_iota(I32, (1, ne), 1)
    expand = (lax.broadcasted_iota(I32, (ne, ne * w), 1) // w
              == lax.broadcasted_iota(I32, (ne, ne * w), 0)).astype(BF16)
    row_in_window = (lax.broadcasted_iota(I32, (cfg.t, ne * w), 1) % w).astype(F32)
    spt = spt_ref[...]

    def one_pass(j, y):
        shift = jnp.zeros((1, ne), I32)
        owned = jnp.zeros((1, ne), I32)
        for e in range(ne):
            a, own = _window(starts[e], j, cfg)
            ow_scr[e * w:(e + 1) * w, :] = o_ref[e, pl.ds(a, w), :]
            shift = jnp.where(expert == e, a - base, shift)
            owned = jnp.where(expert == e, own, owned)
        rel = spt - shift
        rel = jnp.where(jnp.logical_and(rel >= owned, rel < w), rel, -1)
        wide = jnp.dot(rel.astype(F32).astype(BF16), expand, preferred_element_type=F32)
        hits = (wide == row_in_window).astype(BF16)
        return y + jnp.dot(hits, ow_scr[...], preferred_element_type=F32)

    y = lax.fori_loop(0, npass, one_pass, jnp.zeros((cfg.t, d), F32))
    x2 = x1_ref[...] + mod_ref[0][:, 5 * d:6 * d] * y
    if final:
        x2 = (x2 * lax.rsqrt(jnp.mean(x2 * x2, axis=-1, keepdims=True) + EPS)) * fg_ref[...]
    is_ctx = i < cfg.ntc

    @pl.when(is_ctx)
    def _():
        oc_ref[0] = x2

    @pl.when(jnp.logical_not(is_ctx))
    def _():
        ol_ref[0] = x2


def _combine(toff, tcnt, x1, mod, selpos_t, o, fg, cfg, final):
    t, d, e = cfg.t, cfg.d, cfg.e
    return pl.pallas_call(
        functools.partial(_combine_kernel, cfg=cfg, final=final),
        grid_spec=pltpu.PrefetchScalarGridSpec(
            num_scalar_prefetch=2,
            grid=(cfg.nt,),
            in_specs=[pl.BlockSpec((t, d), lambda i, *_: (i, 0)),
                      pl.BlockSpec((1, 1, N_MOD * d), lambda i, *_: (_tab_row(i, cfg), 0, 0)),
                      pl.BlockSpec((t, e), lambda i, *_: (i, 0)),
                      pl.BlockSpec((e, cfg.sblk, d), lambda i, *_: (0, _slot_block(i, cfg), 0)),
                      pl.BlockSpec((1, d), lambda i, *_: (0, 0))],
            out_specs=_x_specs(cfg, lambda i: i),
            scratch_shapes=[pltpu.VMEM((e * cfg.w, d), BF16)]),
        out_shape=[jax.ShapeDtypeStruct((cfg.n_ctx_req, cfg.n_ctx, d), F32),
                   jax.ShapeDtypeStruct((cfg.n_lat_req, cfg.n_lat, d), F32)],
        compiler_params=_cparams(),
        name="combine",
    )(toff, tcnt, x1, mod, selpos_t, o, fg)


def _block_diag_gates(wr, wi, cfg):
    hpb = cfg.bd // cfg.dh
    nblk = cfg.r // cfg.bd
    eye = jnp.eye(hpb, dtype=wr.dtype)

    def blocks(w):
        w4 = w.reshape(nblk, hpb, cfg.dh, cfg.dh)
        return jnp.einsum('jhde,hk->jhdke', w4, eye).reshape(nblk, cfg.bd, cfg.bd)

    return jnp.concatenate([blocks(wr), blocks(wi)], axis=2).astype(BF16)


def _tile_tables(offs_ctx, offs_lat, cfg):
    ctx = offs_ctx[:, 0].reshape(-1, cfg.e)
    lat = jnp.swapaxes(offs_lat[:, :cfg.tpr].reshape(-1, cfg.e, cfg.tpr), 1, 2).reshape(-1, cfg.e)
    return jnp.concatenate([ctx, lat], axis=0).reshape(-1)


def _by_token(rows_ctx, rows_lat, cfg):
    def flip(a):
        return jnp.swapaxes(a.reshape(-1, cfg.e, a.shape[1]), 1, 2).reshape(-1, cfg.e)
    return jnp.concatenate([flip(rows_ctx), flip(rows_lat)], axis=0)


def _forward(cfg, x_prompt, x_sample, state_lru, c, c_ctx, norm1_g, norm2_g, final_g, w_mod, b_mod, w_in,
             pool_w, pool_scale, conv_w, conv_b, lru_wr, lru_br, lru_wi, lru_bi, lru_lambda,
             w_br_pool, w_br_lru, w_out, router_w, exp_w1, exp_w3, exp_w2):
    d, r, p = cfg.d, cfg.r, cfg.p
    depth = w_in.shape[0]
    assert cfg.n_lat % cfg.t == 0 and cfg.t % cfg.grid_w == 0 and cfg.n_ctx_req % cfg.rpb == 0
    assert (cfg.n_ctx_req * cfg.n_ctx) % cfg.n_lat == 0 and cfg.w % BF16_ROWS == 0

    n_ctx_tok = cfg.n_ctx_req * cfg.n_ctx
    xc, xl = x_prompt, x_sample

    cvec = jnp.zeros((cfg.n_tab, d), F32).at[0].set(c_ctx).at[1:1 + cfg.n_lat_req].set(c)
    mod = _modulation(cvec, w_mod, b_mod, cfg)

    states = []
    for l in range(depth):
        mod_l = mod[l].reshape(cfg.n_tab, 1, N_MOD * d)
        row = lambda a: a.reshape(1, -1)
        wp = w_in[l][:, :p].astype(BF16)
        wv = w_in[l][:, p:p + r].astype(BF16)
        wg = w_in[l][:, p + r:].astype(BF16)
        rwa = router_w[l].astype(BF16)
        rwb = (router_w[l] - rwa.astype(F32)).astype(BF16)

        def h0(direction):
            tab = jnp.zeros((cfg.n_tab, 1, r), F32)
            return tab.at[1:1 + cfg.n_lat_req, 0].set(state_lru[:, l, direction].astype(F32))

        wbd = [_block_diag_gates(lru_wr[l, z], lru_wi[l, z], cfg) for z in range(2)]
        vc, hb, hb_last = _scan_bwd(xc, xl, mod_l, row(norm1_g[l]), wv, conv_w[l], row(conv_b[l]), wbd[1],
                                    row(lru_br[l, 1]), row(lru_bi[l, 1]), row(lru_lambda[l, 1]), h0(1), cfg)
        x1, hn2, logits, hf_last, w1b, w3b, w2b = _scan_fwd(
            xc, xl, vc, hb, mod_l, h0(0), row(norm1_g[l]), wp, wg, wbd[0], row(lru_br[l, 0]), row(lru_bi[l, 0]),
            row(lru_lambda[l, 0]), pool_w[l].astype(BF16), row(pool_scale[l]), w_br_pool[l].astype(BF16),
            w_br_lru[l].astype(BF16), w_out[l].astype(BF16), row(norm2_g[l]), rwa, rwb,
            exp_w1, exp_w3, exp_w2, l, cfg)
        states.append(jnp.stack([hf_last[:, 0], hb_last[:, 0]], axis=1))

        by_request = lambda a, n: jnp.swapaxes(a.reshape(-1, n, cfg.e), 1, 2)
        sp_c, g_c, off_c, cnt_c = _route(by_request(logits[:n_ctx_tok], cfg.n_ctx), cfg.c_ctx, cfg)
        sp_l, g_l, off_l, cnt_l = _route(by_request(logits[n_ctx_tok:], cfg.n_lat), cfg.c_lat, cfg)
        selpos_t = _by_token(sp_c, sp_l, cfg)
        toff = _tile_tables(off_c, off_l, cfg)
        tcnt = _tile_tables(cnt_c, cnt_l, cfg)

        xs, gs = _dispatch(toff, tcnt, hn2, selpos_t.T, _by_token(g_c, g_l, cfg).T, cfg)
        o = _ffn(xs, gs, w1b, w3b, w2b, cfg)
        xc, xl = _combine(toff, tcnt, x1, mod_l, selpos_t, o, row(final_g), cfg, l == depth - 1)

    new_state = jnp.stack(states, axis=1).astype(x_prompt.dtype)
    return xc, xl, new_state


def kernel(x_prompt, x_sample, state_lru, c, c_ctx, norm1_g, norm2_g, final_g, w_mod, b_mod, w_in, pool_w,
           pool_scale, conv_w, conv_b, lru_wr, lru_br, lru_wi, lru_bi, lru_lambda, w_br_pool, w_br_lru, w_out,
           router_w, exp_w1, exp_w3, exp_w2):
    cfg = Cfg(d=x_prompt.shape[2], gw=pool_w.shape[2], r=lru_lambda.shape[2], dh=lru_wr.shape[3],
              e=router_w.shape[2], f=exp_w1.shape[3], n_ctx_req=x_prompt.shape[0], n_ctx=x_prompt.shape[1],
              n_lat_req=x_sample.shape[0], n_lat=x_sample.shape[1], grid_w=GRID_W)
    return _forward(cfg, x_prompt, x_sample, state_lru, c, c_ctx, norm1_g, norm2_g, final_g, w_mod, b_mod, w_in,
                    pool_w, pool_scale, conv_w, conv_b, lru_wr, lru_br, lru_wi, lru_bi, lru_lambda,
                    w_br_pool, w_br_lru, w_out, router_w, exp_w1, exp_w3, exp_w2)
```

```python
import functools
from typing import NamedTuple

import jax
import jax.numpy as jnp
from jax import lax
from jax.experimental import pallas as pl
from jax.experimental.pallas import tpu as pltpu

F32 = jnp.float32
BF16 = jnp.bfloat16
I32 = jnp.int32
HIGHEST = lax.Precision.HIGHEST

EPS = 1e-6
LRU_C = 8.0
CONV_WIDTH = 4
POOL_HALF_WINDOWS = (1, 2, 4, 8)
N_MOD = 6
CAPACITY_FACTOR = 2
GRID_W = 64
NOT_SELECTED = -(1 << 20)
SUBLANES = 8
BF16_ROWS = 16
LANES = 128
MXU_DIM = 256
VMEM_LIMIT = 56 * 1024 * 1024


class Cfg(NamedTuple):
    d: int
    gw: int
    r: int
    dh: int
    e: int
    f: int
    n_ctx_req: int
    n_ctx: int
    n_lat_req: int
    n_lat: int
    grid_w: int

    @property
    def t(self): return self.n_ctx
    @property
    def tb(self): return 2 * self.n_ctx
    @property
    def nbc(self): return self.n_ctx_req * self.n_ctx // self.tb
    @property
    def p(self): return self.gw * len(POOL_HALF_WINDOWS)
    @property
    def ntc(self): return self.n_ctx_req
    @property
    def tpr(self): return self.n_lat // self.t
    @property
    def nt(self): return self.ntc + self.n_lat_req * self.tpr
    @property
    def n_tok(self): return self.nt * self.t
    @property
    def c_ctx(self): return CAPACITY_FACTOR * self.n_ctx // self.e
    @property
    def c_lat(self): return CAPACITY_FACTOR * self.n_lat // self.e
    @property
    def sblk(self): return self.c_lat
    @property
    def rpb(self): return self.c_lat // self.c_ctx
    @property
    def n_sblk(self): return self.n_ctx_req // self.rpb + self.n_lat_req
    @property
    def w(self): return min(64, self.sblk)
    @property
    def bd(self): return min(MXU_DIM, self.r)
    @property
    def n_tab(self): return 16


def _cparams():
    return pltpu.CompilerParams(dimension_semantics=("arbitrary",), vmem_limit_bytes=VMEM_LIMIT)


def _tab_row(i, cfg):
    return jnp.where(i < cfg.ntc, 0, 1 + (i - cfg.ntc) // cfg.tpr)


def _sigmoid(x):
    return 0.5 * jnp.tanh(0.5 * x) + 0.5


def _lat_pos(i, cfg):
    j = jnp.maximum(i - cfg.ntc, 0)
    return j // cfg.tpr, j % cfg.tpr


def _x_specs(cfg, tile_of):
    def ctx_map(s, *_):
        return (jnp.minimum(tile_of(s), cfg.ntc - 1), 0, 0)

    def lat_map(s, *_):
        b, k = _lat_pos(tile_of(s), cfg)
        return (b, k, 0)

    return [pl.BlockSpec((1, cfg.t, cfg.d), ctx_map), pl.BlockSpec((1, cfg.t, cfg.d), lat_map)]


def _norm_mod(x, g, scale, shift):
    y = x * lax.rsqrt(jnp.mean(x * x, axis=-1, keepdims=True) + EPS)
    return (y * g) * (1.0 + scale) + shift


def _mod_kernel(c_ref, w_ref, b_ref, o_ref):
    c = c_ref[...]
    s = c * _sigmoid(c)
    o_ref[0] = jnp.dot(s, w_ref[0], precision=HIGHEST, preferred_element_type=F32) + b_ref[0]


def _modulation(cvec, w_mod, b_mod, cfg):
    depth, d, n6 = w_mod.shape
    nc = n6 // 4
    return pl.pallas_call(
        _mod_kernel,
        grid=(depth, n6 // nc),
        in_specs=[pl.BlockSpec((cfg.n_tab, d), lambda l, j: (0, 0)),
                  pl.BlockSpec((1, d, nc), lambda l, j: (l, 0, j)),
                  pl.BlockSpec((1, 1, nc), lambda l, j: (l, 0, j))],
        out_specs=pl.BlockSpec((1, cfg.n_tab, nc), lambda l, j: (l, 0, j)),
        out_shape=jax.ShapeDtypeStruct((depth, cfg.n_tab, n6), F32),
        compiler_params=pltpu.CompilerParams(dimension_semantics=("arbitrary", "arbitrary"),
                                             vmem_limit_bytes=VMEM_LIMIT),
        name="modulation",
    )(cvec, w_mod, b_mod.reshape(depth, 1, n6))


def _conv(v, prev, nxt, w, b):
    t = v.shape[0]
    row = lax.broadcasted_iota(I32, v.shape, 0)
    vm1 = jnp.where(row == 0, prev[7:8], pltpu.roll(v, 1, 0))
    vm2 = jnp.where(row == 0, prev[6:7], jnp.where(row == 1, prev[7:8], pltpu.roll(v, 2, 0)))
    vp1 = jnp.where(row == t - 1, nxt[0:1], pltpu.roll(v, t - 1, 0))
    return b + vm2 * w[0:1] + vm1 * w[1:2] + v * w[2:3] + vp1 * w[3:4]


def _to_slab(slab, val, col0):
    nj = slab.shape[0] // val.shape[0]
    for g in range(val.shape[0] // SUBLANES):
        for j in range(val.shape[1] // LANES):
            dst = pl.ds((g * nj + col0 // LANES + j) * SUBLANES, SUBLANES)
            slab[dst, :] = val[g * SUBLANES:(g + 1) * SUBLANES, j * LANES:(j + 1) * LANES]


def _gates(vc, wbd_ref, br, bi, lam, sa, su, cfg):
    bd = cfg.bd
    vcb = vc.astype(BF16)
    nl = -lam
    rate = LRU_C * (jnp.maximum(nl, 0.0) + jnp.log1p(jnp.exp(-jnp.abs(nl))))
    for j in range(cfg.r // bd):
        sl = slice(j * bd, (j + 1) * bd)
        z = jnp.dot(vcb[:, sl], wbd_ref[j], preferred_element_type=F32)
        rg = _sigmoid(z[:, :bd] + br[:, sl])
        ig = _sigmoid(z[:, bd:] + bi[:, sl])
        m = rg * rate[:, sl]
        a = jnp.exp(-m)
        _to_slab(sa, a, j * bd)
        _to_slab(su, jnp.sqrt(jnp.tanh(m) * (a * a + 1.0)) * ig * vc[:, sl], j * bd)


def _scan_rows(out_ref, h_in, reverse, sa, su, sh):
    t, r = out_ref.shape
    nj = r // LANES
    groups = t // SUBLANES
    for j in range(nj):
        sh[j:j + 1, :] = h_in[:, j * LANES:(j + 1) * LANES]
    h0 = sh[0:nj, :]

    def body(k, h):
        g = groups - 1 - k if reverse else k
        base = g * (nj * SUBLANES)
        for row in (range(SUBLANES - 1, -1, -1) if reverse else range(SUBLANES)):
            src = pl.ds(base + row, nj, stride=SUBLANES)
            h = sa[src, :] * h + su[src, :]
            sh[src, :] = h
        return h

    h = lax.fori_loop(0, groups, body, h0, unroll=2)
    for g in range(t // SUBLANES):
        for j in range(nj):
            out_ref[g * SUBLANES:(g + 1) * SUBLANES, j * LANES:(j + 1) * LANES] = \
                sh[pl.ds((g * nj + j) * SUBLANES, SUBLANES), :]
    sh[0:nj, :] = h
    return jnp.concatenate([sh[j:j + 1, :] for j in range(nj)], axis=1)


def _tile_flags(i, cfg):
    is_ctx = i < cfg.ntc
    k = jnp.maximum(i - cfg.ntc, 0) % cfg.tpr
    first = jnp.logical_or(is_ctx, k == 0)
    last = jnp.logical_or(is_ctx, k == cfg.tpr - 1)
    return is_ctx, first, last


def _scan_bwd_kernel(xc_ref, xl_ref, xprev_ref, xnext_ref, mod_ref, g1_ref, wv_ref, cw_ref, cb_ref, wbd_ref,
                     br_ref, bi_ref, lam_ref, h0_ref, vc_ref, hb_ref, hlast_ref, carry_scr, sa, su, sh,
                     *, cfg):
    i = cfg.nt - 1 - pl.program_id(0)
    t, d = cfg.t, cfg.d
    is_ctx, first, last = _tile_flags(i, cfg)
    x_ext = jnp.concatenate([xprev_ref[0], jnp.where(is_ctx, xc_ref[0], xl_ref[0]), xnext_ref[0]], axis=0)
    m = mod_ref[0]
    hn = _norm_mod(x_ext, g1_ref[...], m[:, d:2 * d], m[:, 0:d]).astype(BF16)
    v_ext = jnp.dot(hn, wv_ref[...], preferred_element_type=F32)
    prev = jnp.where(first, 0.0, v_ext[0:SUBLANES])
    nxt = jnp.where(last, 0.0, v_ext[t + SUBLANES:t + 2 * SUBLANES])
    vc = _conv(v_ext[SUBLANES:t + SUBLANES], prev, nxt, cw_ref[...], cb_ref[...])
    vc_ref[...] = vc
    _gates(vc, wbd_ref, br_ref[...], bi_ref[...], lam_ref[...], sa, su, cfg)

    @pl.when(last)
    def _():
        carry_scr[...] = h0_ref[0]

    h = _scan_rows(hb_ref, carry_scr[...], True, sa, su, sh)
    carry_scr[...] = h

    @pl.when(is_ctx)
    def _():
        hlast_ref[0] = h


def _scan_bwd(xc, xl, mod, g1, wv, cw, cb, wbd, br, bi, lam, h0, cfg):
    t, d, r, nt = cfg.t, cfg.d, cfg.r, cfg.nt
    n = cfg.n_tok
    tb = t // SUBLANES
    rev = lambda s: nt - 1 - s
    full = lambda a: pl.BlockSpec(a.shape, lambda s: (0,) * a.ndim)
    tile = pl.BlockSpec((t, r), lambda s: (rev(s), 0))
    tab = lambda width: pl.BlockSpec((1, 1, width), lambda s: (_tab_row(rev(s), cfg), 0, 0))

    def prev_map(s):
        b, k = _lat_pos(rev(s), cfg)
        return (b, jnp.maximum(k * tb - 1, 0), 0)

    def next_map(s):
        b, k = _lat_pos(rev(s), cfg)
        return (b, jnp.minimum((k + 1) * tb, cfg.n_lat // SUBLANES - 1), 0)

    return pl.pallas_call(
        functools.partial(_scan_bwd_kernel, cfg=cfg),
        grid=(nt,),
        in_specs=_x_specs(cfg, rev) + [
            pl.BlockSpec((1, SUBLANES, d), prev_map), pl.BlockSpec((1, SUBLANES, d), next_map),
            tab(N_MOD * d), full(g1), full(wv), full(cw), full(cb), full(wbd), full(br), full(bi), full(lam),
            tab(r)],
        out_specs=[tile, tile,
                   pl.BlockSpec((1, 1, r), lambda s: (jnp.minimum(rev(s), cfg.ntc - 1), 0, 0))],
        out_shape=[jax.ShapeDtypeStruct((n, r), F32),
                   jax.ShapeDtypeStruct((n, r), F32),
                   jax.ShapeDtypeStruct((cfg.n_ctx_req, 1, r), F32)],
        scratch_shapes=[pltpu.VMEM((1, r), F32)] + [pltpu.VMEM((t * r // LANES, LANES), F32)] * 3,
        compiler_params=_cparams(),
        name="scan_bwd",
    )(xc, xl, xl, xl, mod, g1, wv, cw, cb, wbd, br, bi, lam, h0)


def _pool(pv, is_ctx, cfg):
    t = pv.shape[0]
    gw = cfg.gw
    row = lax.broadcasted_iota(I32, (t, gw), 0)
    pos = jnp.where(is_ctx, row % cfg.n_ctx, row % cfg.grid_w)
    length = jnp.where(is_ctx, cfg.n_ctx, cfg.grid_w)

    def shifted(x, dlt):
        ok = jnp.logical_and(pos + dlt >= 0, pos + dlt < length)
        return jnp.where(ok, pltpu.roll(x, (-dlt) % t, 0), 0.0)

    outs = []
    for g, m in enumerate(POOL_HALF_WINDOWS):
        x = pv[:, g * gw:(g + 1) * gw]
        back, fwd, k = x, x, 1
        while k < m:
            back = back + shifted(back, -k)
            fwd = fwd + shifted(fwd, k)
            k *= 2
        s = shifted(back, -1) + fwd
        cnt = (jnp.minimum(pos + m, length) - jnp.maximum(pos - m, 0)).astype(F32)
        outs.append(s / cnt - x)
    return outs


def _scan_fwd_kernel(vc_ref, h0_ref, wbd_ref, br_ref, bi_ref, lam_ref, ew1_ref, ew3_ref, ew2_ref,
                     hf_ref, hlast_ref, ew1b_ref, ew3b_ref, ew2b_ref, carry_scr, sa, su, sh, *, cfg):
    i = pl.program_id(0)
    is_ctx, first, _ = _tile_flags(i, cfg)

    ew1b_ref[0] = ew1_ref[0, 0].astype(BF16)
    ew3b_ref[0] = ew3_ref[0, 0].astype(BF16)
    ew2b_ref[0] = ew2_ref[0, 0].astype(BF16)

    _gates(vc_ref[...], wbd_ref, br_ref[...], bi_ref[...], lam_ref[...], sa, su, cfg)

    @pl.when(first)
    def _():
        carry_scr[...] = h0_ref[0]

    h = _scan_rows(hf_ref, carry_scr[...], False, sa, su, sh)
    carry_scr[...] = h

    @pl.when(is_ctx)
    def _():
        hlast_ref[0] = h


def _scan_fwd(vc, h0, wbd, br, bi, lam, ew1, ew3, ew2, layer, cfg):
    t, r = cfg.t, cfg.r
    nlt = cfg.nt - cfg.ntc
    tile = pl.BlockSpec((t, r), lambda i: (i, 0))
    full = lambda a: pl.BlockSpec(a.shape, lambda i: (0,) * a.ndim)
    chunk_of = lambda i: jnp.maximum(i - cfg.ntc, 0)

    def chunked(w):
        depth, e, rows, cols = w.shape
        assert (e * rows) % (nlt * BF16_ROWS) == 0
        return w.reshape(depth, nlt, e * rows // nlt, cols)

    ews = [chunked(w) for w in (ew1, ew3, ew2)]
    outs = pl.pallas_call(
        functools.partial(_scan_fwd_kernel, cfg=cfg),
        grid=(cfg.nt,),
        in_specs=[tile, pl.BlockSpec((1, 1, r), lambda i: (_tab_row(i, cfg), 0, 0)),
                  full(wbd), full(br), full(bi), full(lam)] + [
            pl.BlockSpec((1, 1) + w.shape[2:], lambda i: (layer, chunk_of(i), 0, 0)) for w in ews],
        out_specs=[tile, pl.BlockSpec((1, 1, r), lambda i: (jnp.minimum(i, cfg.ntc - 1), 0, 0))] + [
            pl.BlockSpec((1,) + w.shape[2:], lambda i: (chunk_of(i), 0, 0)) for w in ews],
        out_shape=[jax.ShapeDtypeStruct((cfg.n_tok, r), F32),
                   jax.ShapeDtypeStruct((cfg.n_ctx_req, 1, r), F32)] + [
            jax.ShapeDtypeStruct(w.shape[1:], BF16) for w in ews],
        scratch_shapes=[pltpu.VMEM((1, r), F32)] + [pltpu.VMEM((t * r // LANES, LANES), F32)] * 3,
        compiler_params=_cparams(),
        name="scan_fwd",
    )(vc, h0, wbd, br, bi, lam, *ews)
    return list(outs[:2]) + [o.reshape(w.shape[1:]) for o, w in zip(outs[2:], (ew1, ew3, ew2))]


def _mix_kernel(xc_ref, xl_ref, hf_ref, hb_ref, mod_ref, g1_ref, wp_ref, wg_ref, pw_ref, ps_ref, wbp_ref,
                wbl_ref, wo_ref, g2_ref, rwa_ref, rwb_ref, x1_ref, hn2_ref, lg_ref, *, cfg):
    i = pl.program_id(0)
    d, gw = cfg.d, cfg.gw
    is_ctx = i < cfg.nbc
    m = mod_ref[0]
    ps = ps_ref[...]
    dot = functools.partial(jnp.dot, preferred_element_type=F32)
    halves = [slice(h * cfg.n_ctx, (h + 1) * cfg.n_ctx) for h in range(cfg.tb // cfg.n_ctx)]
    st = [dict() for _ in halves]

    def head(s, rs):
        s['x'] = jnp.where(is_ctx, xc_ref[rs, :], xl_ref[rs, :])
        s['hn'] = _norm_mod(s['x'], g1_ref[...], m[:, d:2 * d], m[:, 0:d]).astype(BF16)

    def project(s, rs):
        s['p'] = dot(s['hn'], wp_ref[...])
        s['gl'] = dot(s['hn'], wg_ref[...])

    def pool(s, rs):
        s['y_lru'] = (hf_ref[rs, :] + hb_ref[rs, :]).astype(BF16)
        s['pooled'] = [q.astype(BF16) for q in _pool(s.pop('p'), is_ctx, cfg)]

    def branches(s, rs):
        y_pool = jnp.concatenate([dot(q, pw_ref[g]) * ps[:, g * gw:(g + 1) * gw]
                                  for g, q in enumerate(s.pop('pooled'))], axis=1).astype(BF16)
        s['bp'] = dot(y_pool, wbp_ref[...])
        s['bl'] = dot(s.pop('y_lru'), wbl_ref[...])

    def merge(s, rs):
        gl = s.pop('gl')
        s['merged'] = (_sigmoid(gl[:, :d]) * s.pop('bp') + _sigmoid(gl[:, d:]) * s.pop('bl')).astype(BF16)

    def out_proj(s, rs):
        s['mix'] = dot(s.pop('merged'), wo_ref[...])

    def tail(s, rs):
        x1 = s.pop('x') + m[:, 2 * d:3 * d] * s.pop('mix')
        x1_ref[rs, :] = x1
        hn2 = _norm_mod(x1, g2_ref[...], m[:, 4 * d:5 * d], m[:, 3 * d:4 * d])
        hi = hn2.astype(BF16)
        hn2_ref[rs, :] = hi
        lo = (hn2 - hi.astype(F32)).astype(BF16)
        lg_ref[rs, :] = dot(hi, rwa_ref[...]) + dot(hi, rwb_ref[...]) + dot(lo, rwa_ref[...])

    for stage in (head, project, pool, branches, merge, out_proj, tail):
        for s, rs in zip(st, halves):
            stage(s, rs)


def _mix(xc, xl, hf, hb, mod, g1, wp, wg, pw, ps, wbp, wbl, wo, g2, rwa, rwb, cfg):
    tb, d, r = cfg.tb, cfg.d, cfg.r
    n = cfg.n_tok
    nbc = cfg.nbc
    tile = lambda width: pl.BlockSpec((tb, width), lambda i: (i, 0))
    full = lambda a: pl.BlockSpec(a.shape, lambda i: (0,) * a.ndim)
    tab_row = lambda i: jnp.where(i < nbc, 0, 1 + (i - nbc) // (cfg.n_lat // tb))
    return pl.pallas_call(
        functools.partial(_mix_kernel, cfg=cfg),
        grid=(n // tb,),
        in_specs=[pl.BlockSpec((tb, d), lambda i: (jnp.minimum(i, nbc - 1), 0)),
                  pl.BlockSpec((tb, d), lambda i: (jnp.maximum(i - nbc, 0), 0)),
                  tile(r), tile(r), pl.BlockSpec((1, 1, N_MOD * d), lambda i: (tab_row(i), 0, 0)),
                  full(g1), full(wp), full(wg), full(pw), full(ps), full(wbp), full(wbl), full(wo),
                  full(g2), full(rwa), full(rwb)],
        out_specs=[tile(d), tile(d), tile(cfg.e)],
        out_shape=[jax.ShapeDtypeStruct((n, d), F32),
                   jax.ShapeDtypeStruct((n, d), BF16),
                   jax.ShapeDtypeStruct((n, cfg.e), F32)],
        compiler_params=_cparams(),
        name="mix",
    )(xc.reshape(-1, d), xl.reshape(-1, d), hf, hb, mod, g1, wp, wg, pw, ps, wbp, wbl, wo, g2, rwa, rwb)


def _route_kernel(l_ref, selpos_ref, gate_ref, offs_ref, cnts_ref, *, cap, t):
    lg = l_ref[...]
    nreq, e, n = lg.shape
    rows = nreq * e
    nch = n // t
    ex = jnp.exp(lg - jnp.max(lg, axis=1, keepdims=True))
    aff = (ex / jnp.sum(ex, axis=1, keepdims=True)).reshape(rows, n)
    gate_ref[...] = aff
    keys = pltpu.bitcast(aff, I32)

    def count(mask):
        return jnp.sum(mask.astype(F32), axis=1, keepdims=True)

    thr = jnp.zeros((rows, 1), I32)
    for bit in range(30, -1, -1):
        cand = thr | (1 << bit)
        thr = jnp.where(count(keys >= cand) >= cap, cand, thr)
    gt = keys > thr
    eq = keys == thr
    need = cap - count(gt)

    tri = (lax.broadcasted_iota(I32, (t, t), 0) < lax.broadcasted_iota(I32, (t, t), 1)).astype(BF16)
    tok = lax.broadcasted_iota(I32, (n, LANES), 0)
    chunk = lax.broadcasted_iota(I32, (n, LANES), 1)
    before = (tok < chunk * t).astype(BF16)
    inside = (tok // t == chunk).astype(BF16)

    def prefix(mask):
        mb = mask.astype(BF16)
        offs = jnp.dot(mb, before, preferred_element_type=F32)
        pre = [jnp.dot(mb[:, k * t:(k + 1) * t], tri, preferred_element_type=F32) + offs[:, k:k + 1]
               for k in range(nch)]
        return jnp.concatenate(pre, axis=1) if nch > 1 else pre[0], offs, mb

    eq_rank, _, _ = prefix(eq)
    sel = jnp.logical_or(gt, jnp.logical_and(eq, eq_rank < need))
    pos, offs, selb = prefix(sel)
    selpos_ref[...] = jnp.where(sel, pos.astype(I32), NOT_SELECTED)
    offs_ref[...] = offs.astype(I32)
    cnts_ref[...] = jnp.dot(selb, inside, preferred_element_type=F32).astype(I32)


def _route(logits, cap, cfg):
    nreq, e, n = logits.shape
    rows = nreq * e
    whole = lambda width: pl.BlockSpec((rows, width), lambda b: (0, 0))
    return pl.pallas_call(
        functools.partial(_route_kernel, cap=cap, t=cfg.t),
        grid=(1,),
        in_specs=[pl.BlockSpec((nreq, e, n), lambda b: (0, 0, 0))],
        out_specs=[whole(n), whole(n), whole(LANES), whole(LANES)],
        out_shape=[jax.ShapeDtypeStruct((rows, n), I32),
                   jax.ShapeDtypeStruct((rows, n), F32),
                   jax.ShapeDtypeStruct((rows, LANES), I32),
                   jax.ShapeDtypeStruct((rows, LANES), I32)],
        compiler_params=_cparams(),
        name="route",
    )(logits)


def _slot_block(i, cfg):
    return jnp.where(i < cfg.ntc, i // cfg.rpb, cfg.ntc // cfg.rpb + (i - cfg.ntc) // cfg.tpr)


def _windows(i, toff_ref, tcnt_ref, cfg):
    is_ctx = i < cfg.ntc
    base = jnp.where(is_ctx, (i % cfg.rpb) * cfg.c_ctx, 0)
    starts, npass = [], 0
    for e in range(cfg.e):
        off = base + toff_ref[i * cfg.e + e]
        start = (off // BF16_ROWS) * BF16_ROWS
        starts.append(start)
        npass = jnp.maximum(npass, (off + tcnt_ref[i * cfg.e + e] - start + cfg.w - 1) // cfg.w)
    return base, starts, npass


def _window(start, j, cfg):
    want = start + j * cfg.w
    a = pl.multiple_of(jnp.minimum(want, cfg.sblk - cfg.w), BF16_ROWS)
    return a, want - a


def _dispatch_kernel(toff_ref, tcnt_ref, x_ref, sp_ref, g_ref, xs_ref, gs_ref, p_scr, *, cfg):
    i = pl.program_id(0)
    w = cfg.w
    is_ctx = i < cfg.ntc
    new_block = jnp.where(is_ctx, i % cfg.rpb == 0, jnp.maximum(i - cfg.ntc, 0) % cfg.tpr == 0)

    @pl.when(new_block)
    def _():
        xs_ref[...] = jnp.zeros(xs_ref.shape, xs_ref.dtype)
        gs_ref[...] = jnp.zeros(gs_ref.shape, gs_ref.dtype)

    base, starts, npass = _windows(i, toff_ref, tcnt_ref, cfg)
    slot = lax.broadcasted_iota(I32, (w, cfg.t), 0)

    def one_pass(j, carry):
        firsts = []
        for e in range(cfg.e):
            a, owned = _window(starts[e], j, cfg)
            firsts.append(a)
            rel = sp_ref[e:e + 1, :] - (a - base)
            hit = jnp.logical_and(rel == slot, slot >= owned)
            p_scr[e * w:(e + 1) * w, :] = hit.astype(BF16)
            gs_ref[e, pl.ds(a, w), :] += jnp.sum(jnp.where(hit, g_ref[e:e + 1, :], 0.0), axis=1, keepdims=True)
        rows = jnp.dot(p_scr[...], x_ref[...], preferred_element_type=F32)
        for e in range(cfg.e):
            cur = xs_ref[e, pl.ds(firsts[e], w), :].astype(F32)
            xs_ref[e, pl.ds(firsts[e], w), :] = (cur + rows[e * w:(e + 1) * w]).astype(BF16)
        return carry

    lax.fori_loop(0, npass, one_pass, 0)


def _dispatch(toff, tcnt, hn2, selpos, gate, cfg):
    t, d, e = cfg.t, cfg.d, cfg.e
    s = cfg.n_sblk * cfg.sblk
    slots = lambda width: pl.BlockSpec((e, cfg.sblk, width), lambda i, *_: (0, _slot_block(i, cfg), 0))
    return pl.pallas_call(
        functools.partial(_dispatch_kernel, cfg=cfg),
        grid_spec=pltpu.PrefetchScalarGridSpec(
            num_scalar_prefetch=2,
            grid=(cfg.nt,),
            in_specs=[pl.BlockSpec((t, d), lambda i, *_: (i, 0)),
                      pl.BlockSpec((e, t), lambda i, *_: (0, i)),
                      pl.BlockSpec((e, t), lambda i, *_: (0, i))],
            out_specs=[slots(d), slots(1)],
            scratch_shapes=[pltpu.VMEM((e * cfg.w, t), BF16)]),
        out_shape=[jax.ShapeDtypeStruct((e, s, d), BF16), jax.ShapeDtypeStruct((e, s, 1), F32)],
        compiler_params=_cparams(),
        name="dispatch",
    )(toff, tcnt, hn2, selpos, gate)


def _ffn_kernel(x_ref, gs_ref, w1_ref, w3_ref, w2_ref, o_ref, *, fc):
    x = x_ref[0]
    acc = jnp.zeros((x.shape[0], o_ref.shape[2]), F32)
    for c in range(w1_ref.shape[2] // fc):
        sl = slice(c * fc, (c + 1) * fc)
        h1 = jnp.dot(x, w1_ref[0, :, sl], preferred_element_type=F32)
        h3 = jnp.dot(x, w3_ref[0, :, sl], preferred_element_type=F32)
        hid = (h1 * _sigmoid(h1)) * h3
        acc = acc + jnp.dot(hid.astype(BF16), w2_ref[0, sl, :], preferred_element_type=F32)
    o_ref[0] = (acc * gs_ref[0]).astype(BF16)


def _ffn(xs, gs, w1, w3, w2, cfg):
    e, s, d = xs.shape
    f = w1.shape[2]
    tm = cfg.sblk
    return pl.pallas_call(
        functools.partial(_ffn_kernel, fc=min(512, f)),
        grid=(e, s // tm),
        in_specs=[pl.BlockSpec((1, tm, d), lambda k, m: (k, m, 0)),
                  pl.BlockSpec((1, tm, 1), lambda k, m: (k, m, 0)),
                  pl.BlockSpec((1, d, f), lambda k, m: (k, 0, 0)),
                  pl.BlockSpec((1, d, f), lambda k, m: (k, 0, 0)),
                  pl.BlockSpec((1, f, d), lambda k, m: (k, 0, 0))],
        out_specs=pl.BlockSpec((1, tm, d), lambda k, m: (k, m, 0)),
        out_shape=jax.ShapeDtypeStruct((e, s, d), BF16),
        compiler_params=pltpu.CompilerParams(dimension_semantics=("arbitrary", "arbitrary"),
                                             vmem_limit_bytes=VMEM_LIMIT),
        name="ffn",
    )(xs, gs, w1, w3, w2)


def _combine_kernel(toff_ref, tcnt_ref, x1_ref, mod_ref, spt_ref, o_ref, fg_ref, oc_ref, ol_ref, ow_scr,
                    *, cfg, final):
    i = pl.program_id(0)
    w, d = cfg.w, cfg.d
    base, starts, npass = _windows(i, toff_ref, tcnt_ref, cfg)
    ne = cfg.e
    expert = lax.broadcasted_iota(I32, (1, ne), 1)
    expand = (lax.broadcasted_iota(I32, (ne, ne * w), 1) // w
              == lax.broadcasted_iota(I32, (ne, ne * w), 0)).astype(BF16)
    row_in_window = (lax.broadcasted_iota(I32, (cfg.t, ne * w), 1) % w).astype(F32)
    spt = spt_ref[...]

    def one_pass(j, y):
        shift = jnp.zeros((1, ne), I32)
        owned = jnp.zeros((1, ne), I32)
        for e in range(ne):
            a, own = _window(starts[e], j, cfg)
            ow_scr[e * w:(e + 1) * w, :] = o_ref[e, pl.ds(a, w), :]
            shift = jnp.where(expert == e, a - base, shift)
            owned = jnp.where(expert == e, own, owned)
        rel = spt - shift
        rel = jnp.where(jnp.logical_and(rel >= owned, rel < w), rel, -1)
        wide = jnp.dot(rel.astype(F32).astype(BF16), expand, preferred_element_type=F32)
        hits = (wide == row_in_window).astype(BF16)
        return y + jnp.dot(hits, ow_scr[...], preferred_element_type=F32)

    y = lax.fori_loop(0, npass, one_pass, jnp.zeros((cfg.t, d), F32))
    x2 = x1_ref[...] + mod_ref[0][:, 5 * d:6 * d] * y
    if final:
        x2 = (x2 * lax.rsqrt(jnp.mean(x2 * x2, axis=-1, keepdims=True) + EPS)) * fg_ref[...]
    is_ctx = i < cfg.ntc

    @pl.when(is_ctx)
    def _():
        oc_ref[0] = x2

    @pl.when(jnp.logical_not(is_ctx))
    def _():
        ol_ref[0] = x2


def _combine(toff, tcnt, x1, mod, selpos_t, o, fg, cfg, final):
    t, d, e = cfg.t, cfg.d, cfg.e
    return pl.pallas_call(
        functools.partial(_combine_kernel, cfg=cfg, final=final),
        grid_spec=pltpu.PrefetchScalarGridSpec(
            num_scalar_prefetch=2,
            grid=(cfg.nt,),
            in_specs=[pl.BlockSpec((t, d), lambda i, *_: (i, 0)),
                      pl.BlockSpec((1, 1, N_MOD * d), lambda i, *_: (_tab_row(i, cfg), 0, 0)),
                      pl.BlockSpec((t, e), lambda i, *_: (i, 0)),
                      pl.BlockSpec((e, cfg.sblk, d), lambda i, *_: (0, _slot_block(i, cfg), 0)),
                      pl.BlockSpec((1, d), lambda i, *_: (0, 0))],
            out_specs=_x_specs(cfg, lambda i: i),
            scratch_shapes=[pltpu.VMEM((e * cfg.w, d), BF16)]),
        out_shape=[jax.ShapeDtypeStruct((cfg.n_ctx_req, cfg.n_ctx, d), F32),
                   jax.ShapeDtypeStruct((cfg.n_lat_req, cfg.n_lat, d), F32)],
        compiler_params=_cparams(),
        name="combine",
    )(toff, tcnt, x1, mod, selpos_t, o, fg)


def _block_diag_gates(wr, wi, cfg):
    hpb = cfg.bd // cfg.dh
    nblk = cfg.r // cfg.bd
    eye = jnp.eye(hpb, dtype=wr.dtype)

    def blocks(w):
        w4 = w.reshape(nblk, hpb, cfg.dh, cfg.dh)
        return jnp.einsum('jhde,hk->jhdke', w4, eye).reshape(nblk, cfg.bd, cfg.bd)

    return jnp.concatenate([blocks(wr), blocks(wi)], axis=2).astype(BF16)


def _tile_tables(offs_ctx, offs_lat, cfg):
    ctx = offs_ctx[:, 0].reshape(-1, cfg.e)
    lat = jnp.swapaxes(offs_lat[:, :cfg.tpr].reshape(-1, cfg.e, cfg.tpr), 1, 2).reshape(-1, cfg.e)
    return jnp.concatenate([ctx, lat], axis=0).reshape(-1)


def _by_token(rows_ctx, rows_lat, cfg):
    def flip(a):
        return jnp.swapaxes(a.reshape(-1, cfg.e, a.shape[1]), 1, 2).reshape(-1, cfg.e)
    return jnp.concatenate([flip(rows_ctx), flip(rows_lat)], axis=0)


def _forward(cfg, x_prompt, x_sample, state_lru, c, c_ctx, norm1_g, norm2_g, final_g, w_mod, b_mod, w_in,
             pool_w, pool_scale, conv_w, conv_b, lru_wr, lru_br, lru_wi, lru_bi, lru_lambda,
             w_br_pool, w_br_lru, w_out, router_w, exp_w1, exp_w3, exp_w2):
    d, r, p = cfg.d, cfg.r, cfg.p
    depth = w_in.shape[0]
    assert cfg.n_lat % cfg.t == 0 and cfg.t % cfg.grid_w == 0 and cfg.n_ctx_req % cfg.rpb == 0
    assert cfg.w % BF16_ROWS == 0
    assert (cfg.n_ctx_req * cfg.n_ctx) % cfg.tb == 0 and cfg.n_lat % cfg.tb == 0 and cfg.tb % cfg.grid_w == 0

    n_ctx_tok = cfg.n_ctx_req * cfg.n_ctx
    xc, xl = x_prompt, x_sample

    cvec = jnp.zeros((cfg.n_tab, d), F32).at[0].set(c_ctx).at[1:1 + cfg.n_lat_req].set(c)
    mod = _modulation(cvec, w_mod, b_mod, cfg)

    states = []
    for l in range(depth):
        mod_l = mod[l].reshape(cfg.n_tab, 1, N_MOD * d)
        row = lambda a: a.reshape(1, -1)
        wp = w_in[l][:, :p].astype(BF16)
        wv = w_in[l][:, p:p + r].astype(BF16)
        wg = w_in[l][:, p + r:].astype(BF16)
        rwa = router_w[l].astype(BF16)
        rwb = (router_w[l] - rwa.astype(F32)).astype(BF16)

        def h0(direction):
            tab = jnp.zeros((cfg.n_tab, 1, r), F32)
            return tab.at[1:1 + cfg.n_lat_req, 0].set(state_lru[:, l, direction].astype(F32))

        wbd = [_block_diag_gates(lru_wr[l, z], lru_wi[l, z], cfg) for z in range(2)]
        vc, hb, hb_last = _scan_bwd(xc, xl, mod_l, row(norm1_g[l]), wv, conv_w[l], row(conv_b[l]), wbd[1],
                                    row(lru_br[l, 1]), row(lru_bi[l, 1]), row(lru_lambda[l, 1]), h0(1), cfg)
        hf, hf_last, w1b, w3b, w2b = _scan_fwd(vc, h0(0), wbd[0], row(lru_br[l, 0]), row(lru_bi[l, 0]),
                                               row(lru_lambda[l, 0]), exp_w1, exp_w3, exp_w2, l, cfg)
        x1, hn2, logits = _mix(xc, xl, hf, hb, mod_l, row(norm1_g[l]), wp, wg, pool_w[l].astype(BF16),
                               row(pool_scale[l]), w_br_pool[l].astype(BF16), w_br_lru[l].astype(BF16),
                               w_out[l].astype(BF16), row(norm2_g[l]), rwa, rwb, cfg)
        states.append(jnp.stack([hf_last[:, 0], hb_last[:, 0]], axis=1))

        by_request = lambda a, n: jnp.swapaxes(a.reshape(-1, n, cfg.e), 1, 2)
        sp_c, g_c, off_c, cnt_c = _route(by_request(logits[:n_ctx_tok], cfg.n_ctx), cfg.c_ctx, cfg)
        sp_l, g_l, off_l, cnt_l = _route(by_request(logits[n_ctx_tok:], cfg.n_lat), cfg.c_lat, cfg)
        selpos_t = _by_token(sp_c, sp_l, cfg)
        toff = _tile_tables(off_c, off_l, cfg)
        tcnt = _tile_tables(cnt_c, cnt_l, cfg)

        xs, gs = _dispatch(toff, tcnt, hn2, selpos_t.T, _by_token(g_c, g_l, cfg).T, cfg)
        o = _ffn(xs, gs, w1b, w3b, w2b, cfg)
        xc, xl = _combine(toff, tcnt, x1, mod_l, selpos_t, o, row(final_g), cfg, l == depth - 1)

    new_state = jnp.stack(states, axis=1).astype(x_prompt.dtype)
    return xc, xl, new_state


def kernel(x_prompt, x_sample, state_lru, c, c_ctx, norm1_g, norm2_g, final_g, w_mod, b_mod, w_in, pool_w,
           pool_scale, conv_w, conv_b, lru_wr, lru_br, lru_wi, lru_bi, lru_lambda, w_br_pool, w_br_lru, w_out,
           router_w, exp_w1, exp_w3, exp_w2):
    cfg = Cfg(d=x_prompt.shape[2], gw=pool_w.shape[2], r=lru_lambda.shape[2], dh=lru_wr.shape[3],
              e=router_w.shape[2], f=exp_w1.shape[3], n_ctx_req=x_prompt.shape[0], n_ctx=x_prompt.shape[1],
              n_lat_req=x_sample.shape[0], n_lat=x_sample.shape[1], grid_w=GRID_W)
    return _forward(cfg, x_prompt, x_sample, state_lru, c, c_ctx, norm1_g, norm2_g, final_g, w_mod, b_mod, w_in,
                    pool_w, pool_scale, conv_w, conv_b, lru_wr, lru_br, lru_wi, lru_bi, lru_lambda,
                    w_br_pool, w_br_lru, w_out, router_w, exp_w1, exp_w3, exp_w2)
```

```python
import functools
from typing import NamedTuple

import jax
import jax.numpy as jnp
from jax import lax
from jax.experimental import pallas as pl
from jax.experimental.pallas import tpu as pltpu

F32 = jnp.float32
BF16 = jnp.bfloat16
I32 = jnp.int32
HIGHEST = lax.Precision.HIGHEST

EPS = 1e-6
LRU_C = 8.0
CONV_WIDTH = 4
POOL_HALF_WINDOWS = (1, 2, 4, 8)
N_MOD = 6
CAPACITY_FACTOR = 2
GRID_W = 64
NOT_SELECTED = -(1 << 20)
SUBLANES = 8
BF16_ROWS = 16
LANES = 128
MXU_DIM = 256
VMEM_LIMIT = 56 * 1024 * 1024


class Cfg(NamedTuple):
    d: int
    gw: int
    r: int
    dh: int
    e: int
    f: int
    n_ctx_req: int
    n_ctx: int
    n_lat_req: int
    n_lat: int
    grid_w: int

    @property
    def t(self): return self.n_ctx
    @property
    def tb(self): return 2 * self.n_ctx
    @property
    def nbc(self): return self.n_ctx_req * self.n_ctx // self.tb
    @property
    def p(self): return self.gw * len(POOL_HALF_WINDOWS)
    @property
    def ntc(self): return self.n_ctx_req
    @property
    def tpr(self): return self.n_lat // self.t
    @property
    def nt(self): return self.ntc + self.n_lat_req * self.tpr
    @property
    def n_tok(self): return self.nt * self.t
    @property
    def c_ctx(self): return CAPACITY_FACTOR * self.n_ctx // self.e
    @property
    def c_lat(self): return CAPACITY_FACTOR * self.n_lat // self.e
    @property
    def sblk(self): return self.c_lat
    @property
    def rpb(self): return self.c_lat // self.c_ctx
    @property
    def n_sblk(self): return self.n_ctx_req // self.rpb + self.n_lat_req
    @property
    def w(self): return min(64, self.sblk)
    @property
    def bd(self): return min(MXU_DIM, self.r)
    @property
    def n_tab(self): return 16


def _cparams():
    return pltpu.CompilerParams(dimension_semantics=("arbitrary",), vmem_limit_bytes=VMEM_LIMIT)


def _tab_row(i, cfg):
    return jnp.where(i < cfg.ntc, 0, 1 + (i - cfg.ntc) // cfg.tpr)


def _sigmoid(x):
    return 0.5 * jnp.tanh(0.5 * x) + 0.5


def _lat_pos(i, cfg):
    j = jnp.maximum(i - cfg.ntc, 0)
    return j // cfg.tpr, j % cfg.tpr


def _x_specs(cfg, tile_of):
    def ctx_map(s, *_):
        return (jnp.minimum(tile_of(s), cfg.ntc - 1), 0, 0)

    def lat_map(s, *_):
        b, k = _lat_pos(tile_of(s), cfg)
        return (b, k, 0)

    return [pl.BlockSpec((1, cfg.t, cfg.d), ctx_map), pl.BlockSpec((1, cfg.t, cfg.d), lat_map)]


def _norm_mod(x, g, scale, shift):
    y = x * lax.rsqrt(jnp.mean(x * x, axis=-1, keepdims=True) + EPS)
    return (y * g) * (1.0 + scale) + shift


def _mod_kernel(c_ref, w_ref, b_ref, o_ref):
    c = c_ref[...]
    s = c * _sigmoid(c)
    o_ref[0] = jnp.dot(s, w_ref[0], precision=HIGHEST, preferred_element_type=F32) + b_ref[0]


def _modulation(cvec, w_mod, b_mod, cfg):
    depth, d, n6 = w_mod.shape
    nc = n6 // 4
    return pl.pallas_call(
        _mod_kernel,
        grid=(depth, n6 // nc),
        in_specs=[pl.BlockSpec((cfg.n_tab, d), lambda l, j: (0, 0)),
                  pl.BlockSpec((1, d, nc), lambda l, j: (l, 0, j)),
                  pl.BlockSpec((1, 1, nc), lambda l, j: (l, 0, j))],
        out_specs=pl.BlockSpec((1, cfg.n_tab, nc), lambda l, j: (l, 0, j)),
        out_shape=jax.ShapeDtypeStruct((depth, cfg.n_tab, n6), F32),
        compiler_params=pltpu.CompilerParams(dimension_semantics=("arbitrary", "arbitrary"),
                                             vmem_limit_bytes=VMEM_LIMIT),
        name="modulation",
    )(cvec, w_mod, b_mod.reshape(depth, 1, n6))


def _conv(v, prev, nxt, w, b):
    t = v.shape[0]
    row = lax.broadcasted_iota(I32, v.shape, 0)
    vm1 = jnp.where(row == 0, prev[7:8], pltpu.roll(v, 1, 0))
    vm2 = jnp.where(row == 0, prev[6:7], jnp.where(row == 1, prev[7:8], pltpu.roll(v, 2, 0)))
    vp1 = jnp.where(row == t - 1, nxt[0:1], pltpu.roll(v, t - 1, 0))
    return b + vm2 * w[0:1] + vm1 * w[1:2] + v * w[2:3] + vp1 * w[3:4]


def _to_slab(slab, val, col0):
    nj = slab.shape[0] // val.shape[0]
    for g in range(val.shape[0] // SUBLANES):
        for j in range(val.shape[1] // LANES):
            dst = pl.ds((g * nj + col0 // LANES + j) * SUBLANES, SUBLANES)
            slab[dst, :] = val[g * SUBLANES:(g + 1) * SUBLANES, j * LANES:(j + 1) * LANES]


def _gates(vc, wbd_ref, br, bi, lam, sa, su, cfg):
    bd = cfg.bd
    vcb = vc.astype(BF16)
    nl = -lam
    half_rate = (0.5 * LRU_C) * (jnp.maximum(nl, 0.0) + jnp.log1p(jnp.exp(-jnp.abs(nl))))
    for j in range(cfg.r // bd):
        sl = slice(j * bd, (j + 1) * bd)
        z = jnp.dot(vcb[:, sl], wbd_ref[j], preferred_element_type=F32)
        ig = 0.5 * jnp.tanh(z[:, bd:] + bi[:, sl]) + 0.5
        m = jnp.tanh(z[:, :bd] + br[:, sl]) * half_rate[:, sl] + half_rate[:, sl]
        a = jnp.exp(-m)
        _to_slab(sa, a, j * bd)
        _to_slab(su, jnp.sqrt(jnp.tanh(m) * (a * a + 1.0)) * ig * vc[:, sl], j * bd)


def _scan_rows(out_ref, h_in, reverse, sa, su, sh):
    t, r = out_ref.shape
    nj = r // LANES
    groups = t // SUBLANES
    for j in range(nj):
        sh[j:j + 1, :] = h_in[:, j * LANES:(j + 1) * LANES]
    h0 = sh[0:nj, :]

    def body(k, h):
        g = groups - 1 - k if reverse else k
        base = g * (nj * SUBLANES)
        for row in (range(SUBLANES - 1, -1, -1) if reverse else range(SUBLANES)):
            src = pl.ds(base + row, nj, stride=SUBLANES)
            h = sa[src, :] * h + su[src, :]
            sh[src, :] = h
        return h

    h = lax.fori_loop(0, groups, body, h0, unroll=2)
    for g in range(t // SUBLANES):
        for j in range(nj):
            out_ref[g * SUBLANES:(g + 1) * SUBLANES, j * LANES:(j + 1) * LANES] = \
                sh[pl.ds((g * nj + j) * SUBLANES, SUBLANES), :]
    sh[0:nj, :] = h
    return jnp.concatenate([sh[j:j + 1, :] for j in range(nj)], axis=1)


def _tile_flags(i, cfg):
    is_ctx = i < cfg.ntc
    k = jnp.maximum(i - cfg.ntc, 0) % cfg.tpr
    first = jnp.logical_or(is_ctx, k == 0)
    last = jnp.logical_or(is_ctx, k == cfg.tpr - 1)
    return is_ctx, first, last


def _cast_chunks(w, layer, n_chunks, chunk_of):
    depth, e, rows, cols = w.shape
    assert (e * rows) % (n_chunks * BF16_ROWS) == 0
    cr = e * rows // n_chunks
    return (w.reshape(depth, n_chunks, cr, cols),
            pl.BlockSpec((1, 1, cr, cols), lambda s: (layer, chunk_of(s), 0, 0)),
            pl.BlockSpec((1, cr, cols), lambda s: (chunk_of(s), 0, 0)),
            jax.ShapeDtypeStruct((n_chunks, cr, cols), BF16))


def _scan_bwd_kernel(xc_ref, xl_ref, xprev_ref, xnext_ref, mod_ref, g1_ref, wv_ref, cw_ref, cb_ref, wbd_ref,
                     br_ref, bi_ref, lam_ref, h0_ref, ew_ref, vc_ref, hb_ref, hlast_ref, ewb_ref,
                     carry_scr, sa, su, sh, *, cfg):
    i = cfg.nt - 1 - pl.program_id(0)
    t, d = cfg.t, cfg.d
    is_ctx, first, last = _tile_flags(i, cfg)
    ewb_ref[0] = ew_ref[0, 0].astype(BF16)
    x_ext = jnp.concatenate([xprev_ref[0], jnp.where(is_ctx, xc_ref[0], xl_ref[0]), xnext_ref[0]], axis=0)
    m = mod_ref[0]
    hn = _norm_mod(x_ext, g1_ref[...], m[:, d:2 * d], m[:, 0:d]).astype(BF16)
    v_ext = jnp.dot(hn, wv_ref[...], preferred_element_type=F32)
    prev = jnp.where(first, 0.0, v_ext[0:SUBLANES])
    nxt = jnp.where(last, 0.0, v_ext[t + SUBLANES:t + 2 * SUBLANES])
    vc = _conv(v_ext[SUBLANES:t + SUBLANES], prev, nxt, cw_ref[...], cb_ref[...])
    vc_ref[...] = vc
    _gates(vc, wbd_ref, br_ref[...], bi_ref[...], lam_ref[...], sa, su, cfg)

    @pl.when(last)
    def _():
        carry_scr[...] = h0_ref[0]

    h = _scan_rows(hb_ref, carry_scr[...], True, sa, su, sh)
    carry_scr[...] = h

    @pl.when(is_ctx)
    def _():
        hlast_ref[0] = h


def _scan_bwd(xc, xl, mod, g1, wv, cw, cb, wbd, br, bi, lam, h0, ew, layer, cfg):
    t, d, r, nt = cfg.t, cfg.d, cfg.r, cfg.nt
    n = cfg.n_tok
    tb = t // SUBLANES
    rev = lambda s: nt - 1 - s
    full = lambda a: pl.BlockSpec(a.shape, lambda s: (0,) * a.ndim)
    tile = pl.BlockSpec((t, r), lambda s: (rev(s), 0))
    tab = lambda width: pl.BlockSpec((1, 1, width), lambda s: (_tab_row(rev(s), cfg), 0, 0))

    def prev_map(s):
        b, k = _lat_pos(rev(s), cfg)
        return (b, jnp.maximum(k * tb - 1, 0), 0)

    def next_map(s):
        b, k = _lat_pos(rev(s), cfg)
        return (b, jnp.minimum((k + 1) * tb, cfg.n_lat // SUBLANES - 1), 0)

    ewc, ew_in, ew_out, ew_shape = _cast_chunks(ew, layer, nt - cfg.ntc,
                                                lambda s: jnp.maximum(rev(s) - cfg.ntc, 0))
    vc, hb, hb_last, ewb = pl.pallas_call(
        functools.partial(_scan_bwd_kernel, cfg=cfg),
        grid=(nt,),
        in_specs=_x_specs(cfg, rev) + [
            pl.BlockSpec((1, SUBLANES, d), prev_map), pl.BlockSpec((1, SUBLANES, d), next_map),
            tab(N_MOD * d), full(g1), full(wv), full(cw), full(cb), full(wbd), full(br), full(bi), full(lam),
            tab(r), ew_in],
        out_specs=[tile, tile,
                   pl.BlockSpec((1, 1, r), lambda s: (jnp.minimum(rev(s), cfg.ntc - 1), 0, 0)), ew_out],
        out_shape=[jax.ShapeDtypeStruct((n, r), F32),
                   jax.ShapeDtypeStruct((n, r), F32),
                   jax.ShapeDtypeStruct((cfg.n_ctx_req, 1, r), F32), ew_shape],
        scratch_shapes=[pltpu.VMEM((1, r), F32)] + [pltpu.VMEM((t * r // LANES, LANES), F32)] * 3,
        compiler_params=_cparams(),
        name="scan_bwd",
    )(xc, xl, xl, xl, mod, g1, wv, cw, cb, wbd, br, bi, lam, h0, ewc)
    return vc, hb, hb_last, ewb.reshape(ew.shape[1:])


def _pool(pv, is_ctx, cfg):
    t = pv.shape[0]
    gw = cfg.gw
    row = lax.broadcasted_iota(I32, (t, gw), 0)
    pos = jnp.where(is_ctx, row % cfg.n_ctx, row % cfg.grid_w)
    length = jnp.where(is_ctx, cfg.n_ctx, cfg.grid_w)

    def shifted(x, dlt):
        ok = jnp.logical_and(pos + dlt >= 0, pos + dlt < length)
        return jnp.where(ok, pltpu.roll(x, (-dlt) % t, 0), 0.0)

    outs = []
    for g, m in enumerate(POOL_HALF_WINDOWS):
        x = pv[:, g * gw:(g + 1) * gw]
        back, fwd, k = x, x, 1
        while k < m:
            back = back + shifted(back, -k)
            fwd = fwd + shifted(fwd, k)
            k *= 2
        s = shifted(back, -1) + fwd
        cnt = (jnp.minimum(pos + m, length) - jnp.maximum(pos - m, 0)).astype(F32)
        outs.append(s / cnt - x)
    return outs


def _scan_fwd_kernel(vc_ref, h0_ref, wbd_ref, br_ref, bi_ref, lam_ref, ew_ref,
                     hf_ref, hlast_ref, ewb_ref, carry_scr, sa, su, sh, *, cfg):
    i = pl.program_id(0)
    is_ctx, first, _ = _tile_flags(i, cfg)
    ewb_ref[0] = ew_ref[0, 0].astype(BF16)
    _gates(vc_ref[...], wbd_ref, br_ref[...], bi_ref[...], lam_ref[...], sa, su, cfg)

    @pl.when(first)
    def _():
        carry_scr[...] = h0_ref[0]

    h = _scan_rows(hf_ref, carry_scr[...], False, sa, su, sh)
    carry_scr[...] = h

    @pl.when(is_ctx)
    def _():
        hlast_ref[0] = h


def _scan_fwd(vc, h0, wbd, br, bi, lam, ew, layer, cfg):
    t, r = cfg.t, cfg.r
    tile = pl.BlockSpec((t, r), lambda i: (i, 0))
    full = lambda a: pl.BlockSpec(a.shape, lambda i: (0,) * a.ndim)
    ewc, ew_in, ew_out, ew_shape = _cast_chunks(ew, layer, cfg.nt - cfg.ntc,
                                                lambda i: jnp.maximum(i - cfg.ntc, 0))
    hf, hf_last, ewb = pl.pallas_call(
        functools.partial(_scan_fwd_kernel, cfg=cfg),
        grid=(cfg.nt,),
        in_specs=[tile, pl.BlockSpec((1, 1, r), lambda i: (_tab_row(i, cfg), 0, 0)),
                  full(wbd), full(br), full(bi), full(lam), ew_in],
        out_specs=[tile, pl.BlockSpec((1, 1, r), lambda i: (jnp.minimum(i, cfg.ntc - 1), 0, 0)), ew_out],
        out_shape=[jax.ShapeDtypeStruct((cfg.n_tok, r), F32),
                   jax.ShapeDtypeStruct((cfg.n_ctx_req, 1, r), F32), ew_shape],
        scratch_shapes=[pltpu.VMEM((1, r), F32)] + [pltpu.VMEM((t * r // LANES, LANES), F32)] * 3,
        compiler_params=_cparams(),
        name="scan_fwd",
    )(vc, h0, wbd, br, bi, lam, ewc)
    return hf, hf_last, ewb.reshape(ew.shape[1:])


def _mix_kernel(xc_ref, xl_ref, hf_ref, hb_ref, mod_ref, g1_ref, wp_ref, wg_ref, pw_ref, ps_ref, wbp_ref,
                wbl_ref, wo_ref, g2_ref, rwa_ref, rwb_ref, ew_ref, x1_ref, hn2_ref, lg_ref, ewb_ref, *, cfg):
    i = pl.program_id(0)
    ewb_ref[0] = ew_ref[0, 0].astype(BF16)
    d, gw = cfg.d, cfg.gw
    is_ctx = i < cfg.nbc
    m = mod_ref[0]
    ps = ps_ref[...]
    dot = functools.partial(jnp.dot, preferred_element_type=F32)
    halves = [slice(h * cfg.n_ctx, (h + 1) * cfg.n_ctx) for h in range(cfg.tb // cfg.n_ctx)]
    st = [dict() for _ in halves]

    def head(s, rs):
        s['x'] = jnp.where(is_ctx, xc_ref[rs, :], xl_ref[rs, :])
        s['hn'] = _norm_mod(s['x'], g1_ref[...], m[:, d:2 * d], m[:, 0:d]).astype(BF16)

    def project(s, rs):
        s['p'] = dot(s['hn'], wp_ref[...])
        s['gl'] = dot(s['hn'], wg_ref[...])

    def pool(s, rs):
        s['y_lru'] = (hf_ref[rs, :] + hb_ref[rs, :]).astype(BF16)
        s['pooled'] = [q.astype(BF16) for q in _pool(s.pop('p'), is_ctx, cfg)]

    def branches(s, rs):
        y_pool = jnp.concatenate([dot(q, pw_ref[g]) * ps[:, g * gw:(g + 1) * gw]
                                  for g, q in enumerate(s.pop('pooled'))], axis=1).astype(BF16)
        s['bp'] = dot(y_pool, wbp_ref[...])
        s['bl'] = dot(s.pop('y_lru'), wbl_ref[...])

    def merge(s, rs):
        gl = s.pop('gl')
        s['merged'] = (_sigmoid(gl[:, :d]) * s.pop('bp') + _sigmoid(gl[:, d:]) * s.pop('bl')).astype(BF16)

    def out_proj(s, rs):
        s['mix'] = dot(s.pop('merged'), wo_ref[...])

    def tail(s, rs):
        x1 = s.pop('x') + m[:, 2 * d:3 * d] * s.pop('mix')
        x1_ref[rs, :] = x1
        hn2 = _norm_mod(x1, g2_ref[...], m[:, 4 * d:5 * d], m[:, 3 * d:4 * d])
        hi = hn2.astype(BF16)
        hn2_ref[rs, :] = hi
        lo = (hn2 - hi.astype(F32)).astype(BF16)
        lg_ref[rs, :] = dot(hi, rwa_ref[...]) + dot(hi, rwb_ref[...]) + dot(lo, rwa_ref[...])

    for stage in (head, project, pool, branches, merge, out_proj, tail):
        for s, rs in zip(st, halves):
            stage(s, rs)


def _mix(xc, xl, hf, hb, mod, g1, wp, wg, pw, ps, wbp, wbl, wo, g2, rwa, rwb, ew, layer, cfg):
    tb, d, r = cfg.tb, cfg.d, cfg.r
    n = cfg.n_tok
    nbc = cfg.nbc
    tile = lambda width: pl.BlockSpec((tb, width), lambda i: (i, 0))
    full = lambda a: pl.BlockSpec(a.shape, lambda i: (0,) * a.ndim)
    tab_row = lambda i: jnp.where(i < nbc, 0, 1 + (i - nbc) // (cfg.n_lat // tb))
    ewc, ew_in, ew_out, ew_shape = _cast_chunks(ew, layer, n // tb - nbc, lambda i: jnp.maximum(i - nbc, 0))
    x1, hn2, logits, ewb = pl.pallas_call(
        functools.partial(_mix_kernel, cfg=cfg),
        grid=(n // tb,),
        in_specs=[pl.BlockSpec((tb, d), lambda i: (jnp.minimum(i, nbc - 1), 0)),
                  pl.BlockSpec((tb, d), lambda i: (jnp.maximum(i - nbc, 0), 0)),
                  tile(r), tile(r), pl.BlockSpec((1, 1, N_MOD * d), lambda i: (tab_row(i), 0, 0)),
                  full(g1), full(wp), full(wg), full(pw), full(ps), full(wbp), full(wbl), full(wo),
                  full(g2), full(rwa), full(rwb), ew_in],
        out_specs=[tile(d), tile(d), tile(cfg.e), ew_out],
        out_shape=[jax.ShapeDtypeStruct((n, d), F32),
                   jax.ShapeDtypeStruct((n, d), BF16),
                   jax.ShapeDtypeStruct((n, cfg.e), F32), ew_shape],
        compiler_params=_cparams(),
        name="mix",
    )(xc.reshape(-1, d), xl.reshape(-1, d), hf, hb, mod, g1, wp, wg, pw, ps, wbp, wbl, wo, g2, rwa, rwb, ewc)
    return x1, hn2, logits, ewb.reshape(ew.shape[1:])


def _route_kernel(l_ref, selpos_ref, gate_ref, offs_ref, cnts_ref, *, cap, t):
    lg = l_ref[...]
    nreq, e, n = lg.shape
    rows = nreq * e
    nch = n // t
    ex = jnp.exp(lg - jnp.max(lg, axis=1, keepdims=True))
    aff = (ex / jnp.sum(ex, axis=1, keepdims=True)).reshape(rows, n)
    gate_ref[...] = aff
    keys = pltpu.bitcast(aff, I32)

    def count(mask):
        return jnp.sum(mask.astype(F32), axis=1, keepdims=True)

    thr = jnp.zeros((rows, 1), I32)
    for bit in range(30, -1, -1):
        cand = thr | (1 << bit)
        thr = jnp.where(count(keys >= cand) >= cap, cand, thr)
    gt = keys > thr
    eq = keys == thr
    need = cap - count(gt)

    tri = (lax.broadcasted_iota(I32, (t, t), 0) < lax.broadcasted_iota(I32, (t, t), 1)).astype(BF16)
    tok = lax.broadcasted_iota(I32, (n, LANES), 0)
    chunk = lax.broadcasted_iota(I32, (n, LANES), 1)
    before = (tok < chunk * t).astype(BF16)
    inside = (tok // t == chunk).astype(BF16)

    def prefix(mask):
        mb = mask.astype(BF16)
        offs = jnp.dot(mb, before, preferred_element_type=F32)
        pre = [jnp.dot(mb[:, k * t:(k + 1) * t], tri, preferred_element_type=F32) + offs[:, k:k + 1]
               for k in range(nch)]
        return jnp.concatenate(pre, axis=1) if nch > 1 else pre[0], offs, mb

    eq_rank, _, _ = prefix(eq)
    sel = jnp.logical_or(gt, jnp.logical_and(eq, eq_rank < need))
    pos, offs, selb = prefix(sel)
    selpos_ref[...] = jnp.where(sel, pos.astype(I32), NOT_SELECTED)
    offs_ref[...] = offs.astype(I32)
    cnts_ref[...] = jnp.dot(selb, inside, preferred_element_type=F32).astype(I32)


def _route(logits, cap, cfg):
    nreq, e, n = logits.shape
    rows = nreq * e
    whole = lambda width: pl.BlockSpec((rows, width), lambda b: (0, 0))
    return pl.pallas_call(
        functools.partial(_route_kernel, cap=cap, t=cfg.t),
        grid=(1,),
        in_specs=[pl.BlockSpec((nreq, e, n), lambda b: (0, 0, 0))],
        out_specs=[whole(n), whole(n), whole(LANES), whole(LANES)],
        out_shape=[jax.ShapeDtypeStruct((rows, n), I32),
                   jax.ShapeDtypeStruct((rows, n), F32),
                   jax.ShapeDtypeStruct((rows, LANES), I32),
                   jax.ShapeDtypeStruct((rows, LANES), I32)],
        compiler_params=_cparams(),
        name="route",
    )(logits)


def _slot_block(i, cfg):
    return jnp.where(i < cfg.ntc, i // cfg.rpb, cfg.ntc // cfg.rpb + (i - cfg.ntc) // cfg.tpr)


def _windows(i, toff_ref, tcnt_ref, cfg):
    is_ctx = i < cfg.ntc
    base = jnp.where(is_ctx, (i % cfg.rpb) * cfg.c_ctx, 0)
    starts, npass = [], 0
    for e in range(cfg.e):
        off = base + toff_ref[i * cfg.e + e]
        start = (off // BF16_ROWS) * BF16_ROWS
        starts.append(start)
        npass = jnp.maximum(npass, (off + tcnt_ref[i * cfg.e + e] - start + cfg.w - 1) // cfg.w)
    return base, starts, npass


def _window(start, j, cfg):
    want = start + j * cfg.w
    a = pl.multiple_of(jnp.minimum(want, cfg.sblk - cfg.w), BF16_ROWS)
    return a, want - a


def _dispatch_kernel(toff_ref, tcnt_ref, x_ref, sp_ref, g_ref, xs_ref, gs_ref, p_scr, *, cfg):
    i = pl.program_id(0)
    w = cfg.w
    is_ctx = i < cfg.ntc
    new_block = jnp.where(is_ctx, i % cfg.rpb == 0, jnp.maximum(i - cfg.ntc, 0) % cfg.tpr == 0)

    @pl.when(new_block)
    def _():
        xs_ref[...] = jnp.zeros(xs_ref.shape, xs_ref.dtype)
        gs_ref[...] = jnp.zeros(gs_ref.shape, gs_ref.dtype)

    base, starts, npass = _windows(i, toff_ref, tcnt_ref, cfg)
    slot = lax.broadcasted_iota(I32, (w, cfg.t), 0)

    def one_pass(j, carry):
        firsts = []
        for e in range(cfg.e):
            a, owned = _window(starts[e], j, cfg)
            firsts.append(a)
            rel = sp_ref[e:e + 1, :] - (a - base)
            hit = jnp.logical_and(rel == slot, slot >= owned)
            p_scr[e * w:(e + 1) * w, :] = hit.astype(BF16)
            gs_ref[e, pl.ds(a, w), :] += jnp.sum(jnp.where(hit, g_ref[e:e + 1, :], 0.0), axis=1, keepdims=True)
        rows = jnp.dot(p_scr[...], x_ref[...], preferred_element_type=F32)
        for e in range(cfg.e):
            cur = xs_ref[e, pl.ds(firsts[e], w), :].astype(F32)
            xs_ref[e, pl.ds(firsts[e], w), :] = (cur + rows[e * w:(e + 1) * w]).astype(BF16)
        return carry

    lax.fori_loop(0, npass, one_pass, 0)


def _dispatch(toff, tcnt, hn2, selpos, gate, cfg):
    t, d, e = cfg.t, cfg.d, cfg.e
    s = cfg.n_sblk * cfg.sblk
    slots = lambda width: pl.BlockSpec((e, cfg.sblk, width), lambda i, *_: (0, _slot_block(i, cfg), 0))
    return pl.pallas_call(
        functools.partial(_dispatch_kernel, cfg=cfg),
        grid_spec=pltpu.PrefetchScalarGridSpec(
            num_scalar_prefetch=2,
            grid=(cfg.nt,),
            in_specs=[pl.BlockSpec((t, d), lambda i, *_: (i, 0)),
                      pl.BlockSpec((e, t), lambda i, *_: (0, i)),
                      pl.BlockSpec((e, t), lambda i, *_: (0, i))],
            out_specs=[slots(d), slots(1)],
            scratch_shapes=[pltpu.VMEM((e * cfg.w, t), BF16)]),
        out_shape=[jax.ShapeDtypeStruct((e, s, d), BF16), jax.ShapeDtypeStruct((e, s, 1), F32)],
        compiler_params=_cparams(),
        name="dispatch",
    )(toff, tcnt, hn2, selpos, gate)


def _ffn_kernel(x_ref, gs_ref, w1_ref, w3_ref, w2_ref, o_ref, *, fc):
    x = x_ref[0]
    acc = jnp.zeros((x.shape[0], o_ref.shape[2]), F32)
    for c in range(w1_ref.shape[2] // fc):
        sl = slice(c * fc, (c + 1) * fc)
        h1 = jnp.dot(x, w1_ref[0, :, sl], preferred_element_type=F32)
        h3 = jnp.dot(x, w3_ref[0, :, sl], preferred_element_type=F32)
        hid = (h1 * _sigmoid(h1)) * h3
        acc = acc + jnp.dot(hid.astype(BF16), w2_ref[0, sl, :], preferred_element_type=F32)
    o_ref[0] = (acc * gs_ref[0]).astype(BF16)


def _ffn(xs, gs, w1, w3, w2, cfg):
    e, s, d = xs.shape
    f = w1.shape[2]
    tm = cfg.sblk
    return pl.pallas_call(
        functools.partial(_ffn_kernel, fc=min(512, f)),
        grid=(e, s // tm),
        in_specs=[pl.BlockSpec((1, tm, d), lambda k, m: (k, m, 0)),
                  pl.BlockSpec((1, tm, 1), lambda k, m: (k, m, 0)),
                  pl.BlockSpec((1, d, f), lambda k, m: (k, 0, 0)),
                  pl.BlockSpec((1, d, f), lambda k, m: (k, 0, 0)),
                  pl.BlockSpec((1, f, d), lambda k, m: (k, 0, 0))],
        out_specs=pl.BlockSpec((1, tm, d), lambda k, m: (k, m, 0)),
        out_shape=jax.ShapeDtypeStruct((e, s, d), BF16),
        compiler_params=pltpu.CompilerParams(dimension_semantics=("arbitrary", "arbitrary"),
                                             vmem_limit_bytes=VMEM_LIMIT),
        name="ffn",
    )(xs, gs, w1, w3, w2)


def _combine_kernel(toff_ref, tcnt_ref, x1_ref, mod_ref, spt_ref, o_ref, fg_ref, oc_ref, ol_ref, ow_scr,
                    *, cfg, final):
    i = pl.program_id(0)
    w, d = cfg.w, cfg.d
    base, starts, npass = _windows(i, toff_ref, tcnt_ref, cfg)
    ne = cfg.e
    expert = lax.broadcasted_iota(I32, (1, ne), 1)
    expand = (lax.broadcasted_iota(I32, (ne, ne * w), 1) // w
              == lax.broadcasted_iota(I32, (ne, ne * w), 0)).astype(BF16)
    row_in_window = (lax.broadcasted_iota(I32, (cfg.t, ne * w), 1) % w).astype(F32)
    spt = spt_ref[...]

    def one_pass(j, y):
        shift = jnp.zeros((1, ne), I32)
        owned = jnp.zeros((1, ne), I32)
        for e in range(ne):
            a, own = _window(starts[e], j, cfg)
            ow_scr[e * w:(e + 1) * w, :] = o_ref[e, pl.ds(a, w), :]
            shift = jnp.where(expert == e, a - base, shift)
            owned = jnp.where(expert == e, own, owned)
        rel = spt - shift
        rel = jnp.where(jnp.logical_and(rel >= owned, rel < w), rel, -1)
        wide = jnp.dot(rel.astype(F32).astype(BF16), expand, preferred_element_type=F32)
        hits = (wide == row_in_window).astype(BF16)
        return y + jnp.dot(hits, ow_scr[...], preferred_element_type=F32)

    y = lax.fori_loop(0, npass, one_pass, jnp.zeros((cfg.t, d), F32))
    x2 = x1_ref[...] + mod_ref[0][:, 5 * d:6 * d] * y
    if final:
        x2 = (x2 * lax.rsqrt(jnp.mean(x2 * x2, axis=-1, keepdims=True) + EPS)) * fg_ref[...]
    is_ctx = i < cfg.ntc

    @pl.when(is_ctx)
    def _():
        oc_ref[0] = x2

    @pl.when(jnp.logical_not(is_ctx))
    def _():
        ol_ref[0] = x2


def _combine(toff, tcnt, x1, mod, selpos_t, o, fg, cfg, final):
    t, d, e = cfg.t, cfg.d, cfg.e
    return pl.pallas_call(
        functools.partial(_combine_kernel, cfg=cfg, final=final),
        grid_spec=pltpu.PrefetchScalarGridSpec(
            num_scalar_prefetch=2,
            grid=(cfg.nt,),
            in_specs=[pl.BlockSpec((t, d), lambda i, *_: (i, 0)),
                      pl.BlockSpec((1, 1, N_MOD * d), lambda i, *_: (_tab_row(i, cfg), 0, 0)),
                      pl.BlockSpec((t, e), lambda i, *_: (i, 0)),
                      pl.BlockSpec((e, cfg.sblk, d), lambda i, *_: (0, _slot_block(i, cfg), 0)),
                      pl.BlockSpec((1, d), lambda i, *_: (0, 0))],
            out_specs=_x_specs(cfg, lambda i: i),
            scratch_shapes=[pltpu.VMEM((e * cfg.w, d), BF16)]),
        out_shape=[jax.ShapeDtypeStruct((cfg.n_ctx_req, cfg.n_ctx, d), F32),
                   jax.ShapeDtypeStruct((cfg.n_lat_req, cfg.n_lat, d), F32)],
        compiler_params=_cparams(),
        name="combine",
    )(toff, tcnt, x1, mod, selpos_t, o, fg)


def _block_diag_gates(wr, wi, cfg):
    hpb = cfg.bd // cfg.dh
    nblk = cfg.r // cfg.bd
    eye = jnp.eye(hpb, dtype=wr.dtype)

    def blocks(w):
        w4 = w.reshape(nblk, hpb, cfg.dh, cfg.dh)
        return jnp.einsum('jhde,hk->jhdke', w4, eye).reshape(nblk, cfg.bd, cfg.bd)

    return (0.5 * jnp.concatenate([blocks(wr), blocks(wi)], axis=2)).astype(BF16)


def _tile_tables(offs_ctx, offs_lat, cfg):
    ctx = offs_ctx[:, 0].reshape(-1, cfg.e)
    lat = jnp.swapaxes(offs_lat[:, :cfg.tpr].reshape(-1, cfg.e, cfg.tpr), 1, 2).reshape(-1, cfg.e)
    return jnp.concatenate([ctx, lat], axis=0).reshape(-1)


def _by_token(rows_ctx, rows_lat, cfg):
    def flip(a):
        return jnp.swapaxes(a.reshape(-1, cfg.e, a.shape[1]), 1, 2).reshape(-1, cfg.e)
    return jnp.concatenate([flip(rows_ctx), flip(rows_lat)], axis=0)


def _forward(cfg, x_prompt, x_sample, state_lru, c, c_ctx, norm1_g, norm2_g, final_g, w_mod, b_mod, w_in,
             pool_w, pool_scale, conv_w, conv_b, lru_wr, lru_br, lru_wi, lru_bi, lru_lambda,
             w_br_pool, w_br_lru, w_out, router_w, exp_w1, exp_w3, exp_w2):
    d, r, p = cfg.d, cfg.r, cfg.p
    depth = w_in.shape[0]
    assert cfg.n_lat % cfg.t == 0 and cfg.t % cfg.grid_w == 0 and cfg.n_ctx_req % cfg.rpb == 0
    assert cfg.w % BF16_ROWS == 0
    assert (cfg.n_ctx_req * cfg.n_ctx) % cfg.tb == 0 and cfg.n_lat % cfg.tb == 0 and cfg.tb % cfg.grid_w == 0

    n_ctx_tok = cfg.n_ctx_req * cfg.n_ctx
    xc, xl = x_prompt, x_sample

    cvec = jnp.zeros((cfg.n_tab, d), F32).at[0].set(c_ctx).at[1:1 + cfg.n_lat_req].set(c)
    mod = _modulation(cvec, w_mod, b_mod, cfg)

    states = []
    for l in range(depth):
        mod_l = mod[l].reshape(cfg.n_tab, 1, N_MOD * d)
        row = lambda a: a.reshape(1, -1)
        wp = w_in[l][:, :p].astype(BF16)
        wv = w_in[l][:, p:p + r].astype(BF16)
        wg = w_in[l][:, p + r:].astype(BF16)
        rwa = router_w[l].astype(BF16)
        rwb = (router_w[l] - rwa.astype(F32)).astype(BF16)

        def h0(direction):
            tab = jnp.zeros((cfg.n_tab, 1, r), F32)
            return tab.at[1:1 + cfg.n_lat_req, 0].set(state_lru[:, l, direction].astype(F32))

        wbd = [_block_diag_gates(lru_wr[l, z], lru_wi[l, z], cfg) for z in range(2)]
        vc, hb, hb_last, w1b = _scan_bwd(xc, xl, mod_l, row(norm1_g[l]), wv, conv_w[l], row(conv_b[l]), wbd[1],
                                         row(0.5 * lru_br[l, 1]), row(0.5 * lru_bi[l, 1]),
                                         row(lru_lambda[l, 1]), h0(1), exp_w1, l, cfg)
        hf, hf_last, w3b = _scan_fwd(vc, h0(0), wbd[0], row(0.5 * lru_br[l, 0]), row(0.5 * lru_bi[l, 0]),
                                     row(lru_lambda[l, 0]), exp_w3, l, cfg)
        x1, hn2, logits, w2b = _mix(xc, xl, hf, hb, mod_l, row(norm1_g[l]), wp, wg, pool_w[l].astype(BF16),
                                    row(pool_scale[l]), w_br_pool[l].astype(BF16), w_br_lru[l].astype(BF16),
                                    w_out[l].astype(BF16), row(norm2_g[l]), rwa, rwb, exp_w2, l, cfg)
        states.append(jnp.stack([hf_last[:, 0], hb_last[:, 0]], axis=1))

        by_request = lambda a, n: jnp.swapaxes(a.reshape(-1, n, cfg.e), 1, 2)
        sp_c, g_c, off_c, cnt_c = _route(by_request(logits[:n_ctx_tok], cfg.n_ctx), cfg.c_ctx, cfg)
        sp_l, g_l, off_l, cnt_l = _route(by_request(logits[n_ctx_tok:], cfg.n_lat), cfg.c_lat, cfg)
        selpos_t = _by_token(sp_c, sp_l, cfg)
        toff = _tile_tables(off_c, off_l, cfg)
        tcnt = _tile_tables(cnt_c, cnt_l, cfg)

        xs, gs = _dispatch(toff, tcnt, hn2, selpos_t.T, _by_token(g_c, g_l, cfg).T, cfg)
        o = _ffn(xs, gs, w1b, w3b, w2b, cfg)
        xc, xl = _combine(toff, tcnt, x1, mod_l, selpos_t, o, row(final_g), cfg, l == depth - 1)

    new_state = jnp.stack(states, axis=1).astype(x_prompt.dtype)
    return xc, xl, new_state


def kernel(x_prompt, x_sample, state_lru, c, c_ctx, norm1_g, norm2_g, final_g, w_mod, b_mod, w_in, pool_w,
           pool_scale, conv_w, conv_b, lru_wr, lru_br, lru_wi, lru_bi, lru_lambda, w_br_pool, w_br_lru, w_out,
           router_w, exp_w1, exp_w3, exp_w2):
    cfg = Cfg(d=x_prompt.shape[2], gw=pool_w.shape[2], r=lru_lambda.shape[2], dh=lru_wr.shape[3],
              e=router_w.shape[2], f=exp_w1.shape[3], n_ctx_req=x_prompt.shape[0], n_ctx=x_prompt.shape[1],
              n_lat_req=x_sample.shape[0], n_lat=x_sample.shape[1], grid_w=GRID_W)
    return _forward(cfg, x_prompt, x_sample, state_lru, c, c_ctx, norm1_g, norm2_g, final_g, w_mod, b_mod, w_in,
                    pool_w, pool_scale, conv_w, conv_b, lru_wr, lru_br, lru_wi, lru_bi, lru_lambda,
                    w_br_pool, w_br_lru, w_out, router_w, exp_w1, exp_w3, exp_w2)
```

```python
import functools
from typing import NamedTuple

import jax
import jax.numpy as jnp
from jax import lax
from jax.experimental import pallas as pl
from jax.experimental.pallas import tpu as pltpu

F32 = jnp.float32
BF16 = jnp.bfloat16
I32 = jnp.int32
HIGHEST = lax.Precision.HIGHEST

EPS = 1e-6
LRU_C = 8.0
CONV_WIDTH = 4
POOL_HALF_WINDOWS = (1, 2, 4, 8)
N_MOD = 6
CAPACITY_FACTOR = 2
GRID_W = 64
NOT_SELECTED = -(1 << 20)
SUBLANES = 8
BF16_ROWS = 16
LANES = 128
MXU_DIM = 256
VMEM_LIMIT = 56 * 1024 * 1024


class Cfg(NamedTuple):
    d: int
    gw: int
    r: int
    dh: int
    e: int
    f: int
    n_ctx_req: int
    n_ctx: int
    n_lat_req: int
    n_lat: int
    grid_w: int

    @property
    def t(self): return self.n_ctx
    @property
    def tb(self): return 2 * self.n_ctx
    @property
    def nbc(self): return self.n_ctx_req * self.n_ctx // self.tb
    @property
    def p(self): return self.gw * len(POOL_HALF_WINDOWS)
    @property
    def ntc(self): return self.n_ctx_req
    @property
    def tpr(self): return self.n_lat // self.t
    @property
    def nt(self): return self.ntc + self.n_lat_req * self.tpr
    @property
    def n_tok(self): return self.nt * self.t
    @property
    def c_ctx(self): return CAPACITY_FACTOR * self.n_ctx // self.e
    @property
    def c_lat(self): return CAPACITY_FACTOR * self.n_lat // self.e
    @property
    def sblk(self): return self.c_lat
    @property
    def rpb(self): return self.c_lat // self.c_ctx
    @property
    def n_sblk(self): return self.n_ctx_req // self.rpb + self.n_lat_req
    @property
    def w(self): return min(64, self.sblk)
    @property
    def bd(self): return min(MXU_DIM, self.r)
    @property
    def n_tab(self): return 16


def _cparams():
    return pltpu.CompilerParams(dimension_semantics=("arbitrary",), vmem_limit_bytes=VMEM_LIMIT)


def _tab_row(i, cfg):
    return jnp.where(i < cfg.ntc, 0, 1 + (i - cfg.ntc) // cfg.tpr)


def _sigmoid(x):
    return 0.5 * jnp.tanh(0.5 * x) + 0.5


def _lat_pos(i, cfg):
    j = jnp.maximum(i - cfg.ntc, 0)
    return j // cfg.tpr, j % cfg.tpr


def _x_specs(cfg, tile_of):
    def ctx_map(s, *_):
        return (jnp.minimum(tile_of(s), cfg.ntc - 1), 0, 0)

    def lat_map(s, *_):
        b, k = _lat_pos(tile_of(s), cfg)
        return (b, k, 0)

    return [pl.BlockSpec((1, cfg.t, cfg.d), ctx_map), pl.BlockSpec((1, cfg.t, cfg.d), lat_map)]


def _norm_mod(x, g, scale, shift):
    y = x * lax.rsqrt(jnp.mean(x * x, axis=-1, keepdims=True) + EPS)
    return y * (g * (1.0 + scale)) + shift


def _mod_kernel(c_ref, w_ref, b_ref, o_ref):
    c = c_ref[...]
    s = c * _sigmoid(c)
    o_ref[0] = jnp.dot(s, w_ref[0], precision=HIGHEST, preferred_element_type=F32) + b_ref[0]


def _modulation(cvec, w_mod, b_mod, cfg):
    depth, d, n6 = w_mod.shape
    nc = n6 // 4
    return pl.pallas_call(
        _mod_kernel,
        grid=(depth, n6 // nc),
        in_specs=[pl.BlockSpec((cfg.n_tab, d), lambda l, j: (0, 0)),
                  pl.BlockSpec((1, d, nc), lambda l, j: (l, 0, j)),
                  pl.BlockSpec((1, 1, nc), lambda l, j: (l, 0, j))],
        out_specs=pl.BlockSpec((1, cfg.n_tab, nc), lambda l, j: (l, 0, j)),
        out_shape=jax.ShapeDtypeStruct((depth, cfg.n_tab, n6), F32),
        compiler_params=pltpu.CompilerParams(dimension_semantics=("arbitrary", "arbitrary"),
                                             vmem_limit_bytes=VMEM_LIMIT),
        name="modulation",
    )(cvec, w_mod, b_mod.reshape(depth, 1, n6))


def _conv(v, prev, nxt, w, b):
    t = v.shape[0]
    row = lax.broadcasted_iota(I32, v.shape, 0)
    vm1 = jnp.where(row == 0, prev[7:8], pltpu.roll(v, 1, 0))
    vm2 = jnp.where(row == 0, prev[6:7], jnp.where(row == 1, prev[7:8], pltpu.roll(v, 2, 0)))
    vp1 = jnp.where(row == t - 1, nxt[0:1], pltpu.roll(v, t - 1, 0))
    return b + vm2 * w[0:1] + vm1 * w[1:2] + v * w[2:3] + vp1 * w[3:4]


def _to_slab(slab, val, col0):
    nj = slab.shape[0] // val.shape[0]
    for g in range(val.shape[0] // SUBLANES):
        for j in range(val.shape[1] // LANES):
            dst = pl.ds((g * nj + col0 // LANES + j) * SUBLANES, SUBLANES)
            slab[dst, :] = val[g * SUBLANES:(g + 1) * SUBLANES, j * LANES:(j + 1) * LANES]


def _gates(vc, wbd_ref, br, bi, lam, sa, su, cfg):
    bd = cfg.bd
    vcb = vc.astype(BF16)
    nl = -lam
    half_rate = (0.5 * LRU_C) * (jnp.maximum(nl, 0.0) + jnp.log1p(jnp.exp(-jnp.abs(nl))))
    for j in range(cfg.r // bd):
        sl = slice(j * bd, (j + 1) * bd)
        z = jnp.dot(vcb[:, sl], wbd_ref[j], preferred_element_type=F32)
        ig = 0.5 * jnp.tanh(z[:, bd:] + bi[:, sl]) + 0.5
        m = jnp.tanh(z[:, :bd] + br[:, sl]) * half_rate[:, sl] + half_rate[:, sl]
        a = jnp.exp(-m)
        _to_slab(sa, a, j * bd)
        _to_slab(su, jnp.sqrt(jnp.tanh(m) * (a * a + 1.0)) * ig * vc[:, sl], j * bd)


def _scan_rows(out_ref, h_in, reverse, sa, su, sh):
    t, r = out_ref.shape
    nj = r // LANES
    groups = t // SUBLANES
    for j in range(nj):
        sh[j:j + 1, :] = h_in[:, j * LANES:(j + 1) * LANES]
    h0 = sh[0:nj, :]

    def body(k, h):
        g = groups - 1 - k if reverse else k
        base = g * (nj * SUBLANES)
        for row in (range(SUBLANES - 1, -1, -1) if reverse else range(SUBLANES)):
            src = pl.ds(base + row, nj, stride=SUBLANES)
            h = sa[src, :] * h + su[src, :]
            sh[src, :] = h
        return h

    h = lax.fori_loop(0, groups, body, h0, unroll=2)
    for g in range(t // SUBLANES):
        for j in range(nj):
            out_ref[g * SUBLANES:(g + 1) * SUBLANES, j * LANES:(j + 1) * LANES] = \
                sh[pl.ds((g * nj + j) * SUBLANES, SUBLANES), :]
    sh[0:nj, :] = h
    return jnp.concatenate([sh[j:j + 1, :] for j in range(nj)], axis=1)


def _tile_flags(i, cfg):
    is_ctx = i < cfg.ntc
    k = jnp.maximum(i - cfg.ntc, 0) % cfg.tpr
    first = jnp.logical_or(is_ctx, k == 0)
    last = jnp.logical_or(is_ctx, k == cfg.tpr - 1)
    return is_ctx, first, last


def _cast_chunks(w, layer, n_chunks, chunk_of):
    depth, e, rows, cols = w.shape
    assert (e * rows) % (n_chunks * BF16_ROWS) == 0
    cr = e * rows // n_chunks
    return (w.reshape(depth, n_chunks, cr, cols),
            pl.BlockSpec((1, 1, cr, cols), lambda s: (layer, chunk_of(s), 0, 0)),
            pl.BlockSpec((1, cr, cols), lambda s: (chunk_of(s), 0, 0)),
            jax.ShapeDtypeStruct((n_chunks, cr, cols), BF16))


def _scan_bwd_kernel(xc_ref, xl_ref, xprev_ref, xnext_ref, mod_ref, g1_ref, wv_ref, cw_ref, cb_ref, wbd_ref,
                     br_ref, bi_ref, lam_ref, h0_ref, ew_ref, vc_ref, hb_ref, hlast_ref, ewb_ref,
                     carry_scr, sa, su, sh, *, cfg):
    i = cfg.nt - 1 - pl.program_id(0)
    t, d = cfg.t, cfg.d
    is_ctx, first, last = _tile_flags(i, cfg)
    ewb_ref[0] = ew_ref[0, 0].astype(BF16)
    x_ext = jnp.concatenate([xprev_ref[0], jnp.where(is_ctx, xc_ref[0], xl_ref[0]), xnext_ref[0]], axis=0)
    m = mod_ref[0]
    hn = _norm_mod(x_ext, g1_ref[...], m[:, d:2 * d], m[:, 0:d]).astype(BF16)
    v_ext = jnp.dot(hn, wv_ref[...], preferred_element_type=F32)
    prev = jnp.where(first, 0.0, v_ext[0:SUBLANES])
    nxt = jnp.where(last, 0.0, v_ext[t + SUBLANES:t + 2 * SUBLANES])
    vc = _conv(v_ext[SUBLANES:t + SUBLANES], prev, nxt, cw_ref[...], cb_ref[...])
    vc_ref[...] = vc
    _gates(vc, wbd_ref, br_ref[...], bi_ref[...], lam_ref[...], sa, su, cfg)

    @pl.when(last)
    def _():
        carry_scr[...] = h0_ref[0]

    h = _scan_rows(hb_ref, carry_scr[...], True, sa, su, sh)
    carry_scr[...] = h

    @pl.when(is_ctx)
    def _():
        hlast_ref[0] = h


def _scan_bwd(xc, xl, mod, g1, wv, cw, cb, wbd, br, bi, lam, h0, ew, layer, cfg):
    t, d, r, nt = cfg.t, cfg.d, cfg.r, cfg.nt
    n = cfg.n_tok
    tb = t // SUBLANES
    rev = lambda s: nt - 1 - s
    full = lambda a: pl.BlockSpec(a.shape, lambda s: (0,) * a.ndim)
    tile = pl.BlockSpec((t, r), lambda s: (rev(s), 0))
    tab = lambda width: pl.BlockSpec((1, 1, width), lambda s: (_tab_row(rev(s), cfg), 0, 0))

    def prev_map(s):
        b, k = _lat_pos(rev(s), cfg)
        return (b, jnp.maximum(k * tb - 1, 0), 0)

    def next_map(s):
        b, k = _lat_pos(rev(s), cfg)
        return (b, jnp.minimum((k + 1) * tb, cfg.n_lat // SUBLANES - 1), 0)

    ewc, ew_in, ew_out, ew_shape = _cast_chunks(ew, layer, nt - cfg.ntc,
                                                lambda s: jnp.maximum(rev(s) - cfg.ntc, 0))
    vc, hb, hb_last, ewb = pl.pallas_call(
        functools.partial(_scan_bwd_kernel, cfg=cfg),
        grid=(nt,),
        in_specs=_x_specs(cfg, rev) + [
            pl.BlockSpec((1, SUBLANES, d), prev_map), pl.BlockSpec((1, SUBLANES, d), next_map),
            tab(N_MOD * d), full(g1), full(wv), full(cw), full(cb), full(wbd), full(br), full(bi), full(lam),
            tab(r), ew_in],
        out_specs=[tile, tile,
                   pl.BlockSpec((1, 1, r), lambda s: (jnp.minimum(rev(s), cfg.ntc - 1), 0, 0)), ew_out],
        out_shape=[jax.ShapeDtypeStruct((n, r), F32),
                   jax.ShapeDtypeStruct((n, r), F32),
                   jax.ShapeDtypeStruct((cfg.n_ctx_req, 1, r), F32), ew_shape],
        scratch_shapes=[pltpu.VMEM((1, r), F32)] + [pltpu.VMEM((t * r // LANES, LANES), F32)] * 3,
        compiler_params=_cparams(),
        name="scan_bwd",
    )(xc, xl, xl, xl, mod, g1, wv, cw, cb, wbd, br, bi, lam, h0, ewc)
    return vc, hb, hb_last, ewb.reshape(ew.shape[1:])


def _pool(pv, is_ctx, cfg):
    t = pv.shape[0]
    gw = cfg.gw
    row = lax.broadcasted_iota(I32, (t, gw), 0)
    pos = jnp.where(is_ctx, row % cfg.n_ctx, row % cfg.grid_w)
    length = jnp.where(is_ctx, cfg.n_ctx, cfg.grid_w)

    def shifted(x, dlt):
        ok = jnp.logical_and(pos + dlt >= 0, pos + dlt < length)
        return jnp.where(ok, pltpu.roll(x, (-dlt) % t, 0), 0.0)

    outs = []
    for g, m in enumerate(POOL_HALF_WINDOWS):
        x = pv[:, g * gw:(g + 1) * gw]
        back, fwd, k = x, x, 1
        while k < m:
            back = back + shifted(back, -k)
            fwd = fwd + shifted(fwd, k)
            k *= 2
        s = shifted(back, -1) + fwd
        cnt = (jnp.minimum(pos + m, length) - jnp.maximum(pos - m, 0)).astype(F32)
        outs.append(s / cnt - x)
    return outs


def _scan_fwd_kernel(vc_ref, h0_ref, wbd_ref, br_ref, bi_ref, lam_ref, ew_ref,
                     hf_ref, hlast_ref, ewb_ref, carry_scr, sa, su, sh, *, cfg):
    i = pl.program_id(0)
    is_ctx, first, _ = _tile_flags(i, cfg)
    ewb_ref[0] = ew_ref[0, 0].astype(BF16)
    _gates(vc_ref[...], wbd_ref, br_ref[...], bi_ref[...], lam_ref[...], sa, su, cfg)

    @pl.when(first)
    def _():
        carry_scr[...] = h0_ref[0]

    h = _scan_rows(hf_ref, carry_scr[...], False, sa, su, sh)
    carry_scr[...] = h

    @pl.when(is_ctx)
    def _():
        hlast_ref[0] = h


def _scan_fwd(vc, h0, wbd, br, bi, lam, ew, layer, cfg):
    t, r = cfg.t, cfg.r
    tile = pl.BlockSpec((t, r), lambda i: (i, 0))
    full = lambda a: pl.BlockSpec(a.shape, lambda i: (0,) * a.ndim)
    ewc, ew_in, ew_out, ew_shape = _cast_chunks(ew, layer, cfg.nt - cfg.ntc,
                                                lambda i: jnp.maximum(i - cfg.ntc, 0))
    hf, hf_last, ewb = pl.pallas_call(
        functools.partial(_scan_fwd_kernel, cfg=cfg),
        grid=(cfg.nt,),
        in_specs=[tile, pl.BlockSpec((1, 1, r), lambda i: (_tab_row(i, cfg), 0, 0)),
                  full(wbd), full(br), full(bi), full(lam), ew_in],
        out_specs=[tile, pl.BlockSpec((1, 1, r), lambda i: (jnp.minimum(i, cfg.ntc - 1), 0, 0)), ew_out],
        out_shape=[jax.ShapeDtypeStruct((cfg.n_tok, r), F32),
                   jax.ShapeDtypeStruct((cfg.n_ctx_req, 1, r), F32), ew_shape],
        scratch_shapes=[pltpu.VMEM((1, r), F32)] + [pltpu.VMEM((t * r // LANES, LANES), F32)] * 3,
        compiler_params=_cparams(),
        name="scan_fwd",
    )(vc, h0, wbd, br, bi, lam, ewc)
    return hf, hf_last, ewb.reshape(ew.shape[1:])


def _mix_kernel(xc_ref, xl_ref, hf_ref, hb_ref, mod_ref, g1_ref, wp_ref, wg_ref, pw_ref, ps_ref, wbp_ref,
                wbl_ref, wo_ref, g2_ref, rwa_ref, rwb_ref, ew_ref, x1_ref, hn2_ref, lg_ref, ewb_ref, *, cfg):
    i = pl.program_id(0)
    ewb_ref[0] = ew_ref[0, 0].astype(BF16)
    d, gw = cfg.d, cfg.gw
    is_ctx = i < cfg.nbc
    m = mod_ref[0]
    ps = ps_ref[...]
    dot = functools.partial(jnp.dot, preferred_element_type=F32)
    halves = [slice(h * cfg.n_ctx, (h + 1) * cfg.n_ctx) for h in range(cfg.tb // cfg.n_ctx)]
    st = [dict() for _ in halves]

    def head(s, rs):
        s['x'] = jnp.where(is_ctx, xc_ref[rs, :], xl_ref[rs, :])
        s['hn'] = _norm_mod(s['x'], g1_ref[...], m[:, d:2 * d], m[:, 0:d]).astype(BF16)

    def project(s, rs):
        s['p'] = dot(s['hn'], wp_ref[...])
        s['gl'] = dot(s['hn'], wg_ref[...])

    def pool(s, rs):
        s['y_lru'] = (hf_ref[rs, :] + hb_ref[rs, :]).astype(BF16)
        s['pooled'] = [q.astype(BF16) for q in _pool(s.pop('p'), is_ctx, cfg)]

    def branches(s, rs):
        y_pool = jnp.concatenate([dot(q, pw_ref[g]) * ps[:, g * gw:(g + 1) * gw]
                                  for g, q in enumerate(s.pop('pooled'))], axis=1).astype(BF16)
        s['bp'] = dot(y_pool, wbp_ref[...])
        s['bl'] = dot(s.pop('y_lru'), wbl_ref[...])

    def merge(s, rs):
        gl = s.pop('gl')
        s['merged'] = (_sigmoid(gl[:, :d]) * s.pop('bp') + _sigmoid(gl[:, d:]) * s.pop('bl')).astype(BF16)

    def out_proj(s, rs):
        s['mix'] = dot(s.pop('merged'), wo_ref[...])

    def tail(s, rs):
        x1 = s.pop('x') + m[:, 2 * d:3 * d] * s.pop('mix')
        x1_ref[rs, :] = x1
        hn2 = _norm_mod(x1, g2_ref[...], m[:, 4 * d:5 * d], m[:, 3 * d:4 * d])
        hi = hn2.astype(BF16)
        hn2_ref[rs, :] = hi
        lo = (hn2 - hi.astype(F32)).astype(BF16)
        lg_ref[rs, :] = dot(hi, rwa_ref[...]) + dot(hi, rwb_ref[...]) + dot(lo, rwa_ref[...])

    for stage in (head, project, pool, branches, merge, out_proj, tail):
        for s, rs in zip(st, halves):
            stage(s, rs)


def _mix(xc, xl, hf, hb, mod, g1, wp, wg, pw, ps, wbp, wbl, wo, g2, rwa, rwb, ew, layer, cfg):
    tb, d, r = cfg.tb, cfg.d, cfg.r
    n = cfg.n_tok
    nbc = cfg.nbc
    tile = lambda width: pl.BlockSpec((tb, width), lambda i: (i, 0))
    full = lambda a: pl.BlockSpec(a.shape, lambda i: (0,) * a.ndim)
    tab_row = lambda i: jnp.where(i < nbc, 0, 1 + (i - nbc) // (cfg.n_lat // tb))
    ewc, ew_in, ew_out, ew_shape = _cast_chunks(ew, layer, n // tb - nbc, lambda i: jnp.maximum(i - nbc, 0))
    x1, hn2, logits, ewb = pl.pallas_call(
        functools.partial(_mix_kernel, cfg=cfg),
        grid=(n // tb,),
        in_specs=[pl.BlockSpec((tb, d), lambda i: (jnp.minimum(i, nbc - 1), 0)),
                  pl.BlockSpec((tb, d), lambda i: (jnp.maximum(i - nbc, 0), 0)),
                  tile(r), tile(r), pl.BlockSpec((1, 1, N_MOD * d), lambda i: (tab_row(i), 0, 0)),
                  full(g1), full(wp), full(wg), full(pw), full(ps), full(wbp), full(wbl), full(wo),
                  full(g2), full(rwa), full(rwb), ew_in],
        out_specs=[tile(d), tile(d), tile(cfg.e), ew_out],
        out_shape=[jax.ShapeDtypeStruct((n, d), F32),
                   jax.ShapeDtypeStruct((n, d), BF16),
                   jax.ShapeDtypeStruct((n, cfg.e), F32), ew_shape],
        compiler_params=_cparams(),
        name="mix",
    )(xc.reshape(-1, d), xl.reshape(-1, d), hf, hb, mod, g1, wp, wg, pw, ps, wbp, wbl, wo, g2, rwa, rwb, ewc)
    return x1, hn2, logits, ewb.reshape(ew.shape[1:])


def _route_kernel(l_ref, selpos_ref, gate_ref, offs_ref, cnts_ref, *, cap, t):
    lg = l_ref[...]
    nreq, e, n = lg.shape
    rows = nreq * e
    nch = n // t
    ex = jnp.exp(lg - jnp.max(lg, axis=1, keepdims=True))
    aff = (ex / jnp.sum(ex, axis=1, keepdims=True)).reshape(rows, n)
    gate_ref[...] = aff
    keys = pltpu.bitcast(aff, I32)

    def count(mask):
        return jnp.sum(mask.astype(F32), axis=1, keepdims=True)

    thr = jnp.zeros((rows, 1), I32)
    for bit in range(30, -1, -1):
        cand = thr | (1 << bit)
        thr = jnp.where(count(keys >= cand) >= cap, cand, thr)
    gt = keys > thr
    eq = keys == thr
    need = cap - count(gt)

    tri = (lax.broadcasted_iota(I32, (t, t), 0) < lax.broadcasted_iota(I32, (t, t), 1)).astype(BF16)
    tok = lax.broadcasted_iota(I32, (n, LANES), 0)
    chunk = lax.broadcasted_iota(I32, (n, LANES), 1)
    before = (tok < chunk * t).astype(BF16)
    inside = (tok // t == chunk).astype(BF16)

    def prefix(mask):
        mb = mask.astype(BF16)
        offs = jnp.dot(mb, before, preferred_element_type=F32)
        pre = [jnp.dot(mb[:, k * t:(k + 1) * t], tri, preferred_element_type=F32) + offs[:, k:k + 1]
               for k in range(nch)]
        return jnp.concatenate(pre, axis=1) if nch > 1 else pre[0], offs, mb

    eq_rank, _, _ = prefix(eq)
    sel = jnp.logical_or(gt, jnp.logical_and(eq, eq_rank < need))
    pos, offs, selb = prefix(sel)
    selpos_ref[...] = jnp.where(sel, pos.astype(I32), NOT_SELECTED)
    offs_ref[...] = offs.astype(I32)
    cnts_ref[...] = jnp.dot(selb, inside, preferred_element_type=F32).astype(I32)


def _route(logits, cap, cfg):
    nreq, e, n = logits.shape
    rows = nreq * e
    whole = lambda width: pl.BlockSpec((rows, width), lambda b: (0, 0))
    return pl.pallas_call(
        functools.partial(_route_kernel, cap=cap, t=cfg.t),
        grid=(1,),
        in_specs=[pl.BlockSpec((nreq, e, n), lambda b: (0, 0, 0))],
        out_specs=[whole(n), whole(n), whole(LANES), whole(LANES)],
        out_shape=[jax.ShapeDtypeStruct((rows, n), I32),
                   jax.ShapeDtypeStruct((rows, n), F32),
                   jax.ShapeDtypeStruct((rows, LANES), I32),
                   jax.ShapeDtypeStruct((rows, LANES), I32)],
        compiler_params=_cparams(),
        name="route",
    )(logits)


def _slot_block(i, cfg):
    return jnp.where(i < cfg.ntc, i // cfg.rpb, cfg.ntc // cfg.rpb + (i - cfg.ntc) // cfg.tpr)


def _windows(i, toff_ref, tcnt_ref, cfg):
    is_ctx = i < cfg.ntc
    base = jnp.where(is_ctx, (i % cfg.rpb) * cfg.c_ctx, 0)
    starts, npass = [], 0
    for e in range(cfg.e):
        off = base + toff_ref[i * cfg.e + e]
        start = (off // BF16_ROWS) * BF16_ROWS
        starts.append(start)
        npass = jnp.maximum(npass, (off + tcnt_ref[i * cfg.e + e] - start + cfg.w - 1) // cfg.w)
    return base, starts, npass


def _window(start, j, cfg):
    want = start + j * cfg.w
    a = pl.multiple_of(jnp.minimum(want, cfg.sblk - cfg.w), BF16_ROWS)
    return a, want - a


def _dispatch_kernel(toff_ref, tcnt_ref, x_ref, sp_ref, g_ref, xs_ref, gs_ref, p_scr, *, cfg):
    i = pl.program_id(0)
    w = cfg.w
    is_ctx = i < cfg.ntc
    new_block = jnp.where(is_ctx, i % cfg.rpb == 0, jnp.maximum(i - cfg.ntc, 0) % cfg.tpr == 0)

    @pl.when(new_block)
    def _():
        xs_ref[...] = jnp.zeros(xs_ref.shape, xs_ref.dtype)
        gs_ref[...] = jnp.zeros(gs_ref.shape, gs_ref.dtype)

    base, starts, npass = _windows(i, toff_ref, tcnt_ref, cfg)
    slot = lax.broadcasted_iota(I32, (w, cfg.t), 0)

    def one_pass(j, carry):
        firsts = []
        for e in range(cfg.e):
            a, owned = _window(starts[e], j, cfg)
            firsts.append(a)
            rel = sp_ref[e:e + 1, :] - (a - base)
            hit = jnp.logical_and(rel == slot, slot >= owned)
            p_scr[e * w:(e + 1) * w, :] = hit.astype(BF16)
            gs_ref[e, pl.ds(a, w), :] += jnp.sum(jnp.where(hit, g_ref[e:e + 1, :], 0.0), axis=1, keepdims=True)
        rows = jnp.dot(p_scr[...], x_ref[...], preferred_element_type=F32)
        for e in range(cfg.e):
            xs_ref[e, pl.ds(firsts[e], w), :] += rows[e * w:(e + 1) * w].astype(BF16)
        return carry

    lax.fori_loop(0, npass, one_pass, 0)


def _dispatch(toff, tcnt, hn2, selpos, gate, cfg):
    t, d, e = cfg.t, cfg.d, cfg.e
    s = cfg.n_sblk * cfg.sblk
    slots = lambda width: pl.BlockSpec((e, cfg.sblk, width), lambda i, *_: (0, _slot_block(i, cfg), 0))
    return pl.pallas_call(
        functools.partial(_dispatch_kernel, cfg=cfg),
        grid_spec=pltpu.PrefetchScalarGridSpec(
            num_scalar_prefetch=2,
            grid=(cfg.nt,),
            in_specs=[pl.BlockSpec((t, d), lambda i, *_: (i, 0)),
                      pl.BlockSpec((e, t), lambda i, *_: (0, i)),
                      pl.BlockSpec((e, t), lambda i, *_: (0, i))],
            out_specs=[slots(d), slots(1)],
            scratch_shapes=[pltpu.VMEM((e * cfg.w, t), BF16)]),
        out_shape=[jax.ShapeDtypeStruct((e, s, d), BF16), jax.ShapeDtypeStruct((e, s, 1), F32)],
        compiler_params=_cparams(),
        name="dispatch",
    )(toff, tcnt, hn2, selpos, gate)


def _ffn_kernel(x_ref, gs_ref, w1_ref, w3_ref, w2_ref, o_ref, *, fc):
    x = x_ref[0]
    acc = jnp.zeros((x.shape[0], o_ref.shape[2]), F32)
    for c in range(w1_ref.shape[2] // fc):
        sl = slice(c * fc, (c + 1) * fc)
        h1 = jnp.dot(x, w1_ref[0, :, sl], preferred_element_type=F32)
        h3 = jnp.dot(x, w3_ref[0, :, sl], preferred_element_type=F32)
        hid = (h1 * _sigmoid(h1)) * h3
        acc = acc + jnp.dot(hid.astype(BF16), w2_ref[0, sl, :], preferred_element_type=F32)
    o_ref[0] = (acc * gs_ref[0]).astype(BF16)


def _ffn(xs, gs, w1, w3, w2, cfg):
    e, s, d = xs.shape
    f = w1.shape[2]
    tm = cfg.sblk
    return pl.pallas_call(
        functools.partial(_ffn_kernel, fc=min(512, f)),
        grid=(e, s // tm),
        in_specs=[pl.BlockSpec((1, tm, d), lambda k, m: (k, m, 0)),
                  pl.BlockSpec((1, tm, 1), lambda k, m: (k, m, 0)),
                  pl.BlockSpec((1, d, f), lambda k, m: (k, 0, 0)),
                  pl.BlockSpec((1, d, f), lambda k, m: (k, 0, 0)),
                  pl.BlockSpec((1, f, d), lambda k, m: (k, 0, 0))],
        out_specs=pl.BlockSpec((1, tm, d), lambda k, m: (k, m, 0)),
        out_shape=jax.ShapeDtypeStruct((e, s, d), BF16),
        compiler_params=pltpu.CompilerParams(dimension_semantics=("arbitrary", "arbitrary"),
                                             vmem_limit_bytes=VMEM_LIMIT),
        name="ffn",
    )(xs, gs, w1, w3, w2)


def _combine_kernel(toff_ref, tcnt_ref, x1_ref, mod_ref, spt_ref, o_ref, fg_ref, oc_ref, ol_ref, ow_scr,
                    *, cfg, final):
    i = pl.program_id(0)
    w, d = cfg.w, cfg.d
    base, starts, npass = _windows(i, toff_ref, tcnt_ref, cfg)
    ne = cfg.e
    expert = lax.broadcasted_iota(I32, (1, ne), 1)
    expand = (lax.broadcasted_iota(I32, (ne, ne * w), 1) // w
              == lax.broadcasted_iota(I32, (ne, ne * w), 0)).astype(BF16)
    row_in_window = (lax.broadcasted_iota(I32, (cfg.t, ne * w), 1) % w).astype(F32)
    spt = spt_ref[...]

    def one_pass(j):
        shift = jnp.zeros((1, ne), I32)
        owned = jnp.zeros((1, ne), I32)
        for e in range(ne):
            a, own = _window(starts[e], j, cfg)
            ow_scr[e * w:(e + 1) * w, :] = o_ref[e, pl.ds(a, w), :]
            shift = jnp.where(expert == e, a - base, shift)
            owned = jnp.where(expert == e, own, owned)
        rel = spt - shift
        rel = jnp.where(jnp.logical_and(rel >= owned, rel < w), rel, -1)
        wide = jnp.dot(rel.astype(F32).astype(BF16), expand, preferred_element_type=F32)
        hits = (wide == row_in_window).astype(BF16)
        return jnp.dot(hits, ow_scr[...], preferred_element_type=F32)

    y = lax.fori_loop(1, npass, lambda j, y: y + one_pass(j), one_pass(0))
    x2 = x1_ref[...] + mod_ref[0][:, 5 * d:6 * d] * y
    if final:
        x2 = (x2 * lax.rsqrt(jnp.mean(x2 * x2, axis=-1, keepdims=True) + EPS)) * fg_ref[...]
    is_ctx = i < cfg.ntc

    @pl.when(is_ctx)
    def _():
        oc_ref[0] = x2

    @pl.when(jnp.logical_not(is_ctx))
    def _():
        ol_ref[0] = x2


def _combine(toff, tcnt, x1, mod, selpos_t, o, fg, cfg, final):
    t, d, e = cfg.t, cfg.d, cfg.e
    return pl.pallas_call(
        functools.partial(_combine_kernel, cfg=cfg, final=final),
        grid_spec=pltpu.PrefetchScalarGridSpec(
            num_scalar_prefetch=2,
            grid=(cfg.nt,),
            in_specs=[pl.BlockSpec((t, d), lambda i, *_: (i, 0)),
                      pl.BlockSpec((1, 1, N_MOD * d), lambda i, *_: (_tab_row(i, cfg), 0, 0)),
                      pl.BlockSpec((t, e), lambda i, *_: (i, 0)),
                      pl.BlockSpec((e, cfg.sblk, d), lambda i, *_: (0, _slot_block(i, cfg), 0)),
                      pl.BlockSpec((1, d), lambda i, *_: (0, 0))],
            out_specs=_x_specs(cfg, lambda i: i),
            scratch_shapes=[pltpu.VMEM((e * cfg.w, d), BF16)]),
        out_shape=[jax.ShapeDtypeStruct((cfg.n_ctx_req, cfg.n_ctx, d), F32),
                   jax.ShapeDtypeStruct((cfg.n_lat_req, cfg.n_lat, d), F32)],
        compiler_params=_cparams(),
        name="combine",
    )(toff, tcnt, x1, mod, selpos_t, o, fg)


def _block_diag_gates(wr, wi, cfg):
    hpb = cfg.bd // cfg.dh
    nblk = cfg.r // cfg.bd
    eye = jnp.eye(hpb, dtype=wr.dtype)

    def blocks(w):
        w4 = w.reshape(nblk, hpb, cfg.dh, cfg.dh)
        return jnp.einsum('jhde,hk->jhdke', w4, eye).reshape(nblk, cfg.bd, cfg.bd)

    return (0.5 * jnp.concatenate([blocks(wr), blocks(wi)], axis=2)).astype(BF16)


def _tile_tables(offs_ctx, offs_lat, cfg):
    ctx = offs_ctx[:, 0].reshape(-1, cfg.e)
    lat = jnp.swapaxes(offs_lat[:, :cfg.tpr].reshape(-1, cfg.e, cfg.tpr), 1, 2).reshape(-1, cfg.e)
    return jnp.concatenate([ctx, lat], axis=0).reshape(-1)


def _by_token(rows_ctx, rows_lat, cfg):
    def flip(a):
        return jnp.swapaxes(a.reshape(-1, cfg.e, a.shape[1]), 1, 2).reshape(-1, cfg.e)
    return jnp.concatenate([flip(rows_ctx), flip(rows_lat)], axis=0)


def _forward(cfg, x_prompt, x_sample, state_lru, c, c_ctx, norm1_g, norm2_g, final_g, w_mod, b_mod, w_in,
             pool_w, pool_scale, conv_w, conv_b, lru_wr, lru_br, lru_wi, lru_bi, lru_lambda,
             w_br_pool, w_br_lru, w_out, router_w, exp_w1, exp_w3, exp_w2):
    d, r, p = cfg.d, cfg.r, cfg.p
    depth = w_in.shape[0]
    assert cfg.n_lat % cfg.t == 0 and cfg.t % cfg.grid_w == 0 and cfg.n_ctx_req % cfg.rpb == 0
    assert cfg.w % BF16_ROWS == 0
    assert (cfg.n_ctx_req * cfg.n_ctx) % cfg.tb == 0 and cfg.n_lat % cfg.tb == 0 and cfg.tb % cfg.grid_w == 0

    n_ctx_tok = cfg.n_ctx_req * cfg.n_ctx
    xc, xl = x_prompt, x_sample

    cvec = jnp.zeros((cfg.n_tab, d), F32).at[0].set(c_ctx).at[1:1 + cfg.n_lat_req].set(c)
    mod = _modulation(cvec, w_mod, b_mod, cfg)

    states = []
    for l in range(depth):
        mod_l = mod[l].reshape(cfg.n_tab, 1, N_MOD * d)
        row = lambda a: a.reshape(1, -1)
        wp = w_in[l][:, :p].astype(BF16)
        wv = w_in[l][:, p:p + r].astype(BF16)
        wg = w_in[l][:, p + r:].astype(BF16)
        rwa = router_w[l].astype(BF16)
        rwb = (router_w[l] - rwa.astype(F32)).astype(BF16)

        def h0(direction):
            tab = jnp.zeros((cfg.n_tab, 1, r), F32)
            return tab.at[1:1 + cfg.n_lat_req, 0].set(state_lru[:, l, direction].astype(F32))

        wbd = [_block_diag_gates(lru_wr[l, z], lru_wi[l, z], cfg) for z in range(2)]
        vc, hb, hb_last, w1b = _scan_bwd(xc, xl, mod_l, row(norm1_g[l]), wv, conv_w[l], row(conv_b[l]), wbd[1],
                                         row(0.5 * lru_br[l, 1]), row(0.5 * lru_bi[l, 1]),
                                         row(lru_lambda[l, 1]), h0(1), exp_w1, l, cfg)
        hf, hf_last, w3b = _scan_fwd(vc, h0(0), wbd[0], row(0.5 * lru_br[l, 0]), row(0.5 * lru_bi[l, 0]),
                                     row(lru_lambda[l, 0]), exp_w3, l, cfg)
        x1, hn2, logits, w2b = _mix(xc, xl, hf, hb, mod_l, row(norm1_g[l]), wp, wg, pool_w[l].astype(BF16),
                                    row(pool_scale[l]), w_br_pool[l].astype(BF16), w_br_lru[l].astype(BF16),
                                    w_out[l].astype(BF16), row(norm2_g[l]), rwa, rwb, exp_w2, l, cfg)
        states.append(jnp.stack([hf_last[:, 0], hb_last[:, 0]], axis=1))

        by_request = lambda a, n: jnp.swapaxes(a.reshape(-1, n, cfg.e), 1, 2)
        sp_c, g_c, off_c, cnt_c = _route(by_request(logits[:n_ctx_tok], cfg.n_ctx), cfg.c_ctx, cfg)
        sp_l, g_l, off_l, cnt_l = _route(by_request(logits[n_ctx_tok:], cfg.n_lat), cfg.c_lat, cfg)
        selpos_t = _by_token(sp_c, sp_l, cfg)
        toff = _tile_tables(off_c, off_l, cfg)
        tcnt = _tile_tables(cnt_c, cnt_l, cfg)

        xs, gs = _dispatch(toff, tcnt, hn2, selpos_t.T, _by_token(g_c, g_l, cfg).T, cfg)
        o = _ffn(xs, gs, w1b, w3b, w2b, cfg)
        xc, xl = _combine(toff, tcnt, x1, mod_l, selpos_t, o, row(final_g), cfg, l == depth - 1)

    new_state = jnp.stack(states, axis=1).astype(x_prompt.dtype)
    return xc, xl, new_state


def kernel(x_prompt, x_sample, state_lru, c, c_ctx, norm1_g, norm2_g, final_g, w_mod, b_mod, w_in, pool_w,
           pool_scale, conv_w, conv_b, lru_wr, lru_br, lru_wi, lru_bi, lru_lambda, w_br_pool, w_br_lru, w_out,
           router_w, exp_w1, exp_w3, exp_w2):
    cfg = Cfg(d=x_prompt.shape[2], gw=pool_w.shape[2], r=lru_lambda.shape[2], dh=lru_wr.shape[3],
              e=router_w.shape[2], f=exp_w1.shape[3], n_ctx_req=x_prompt.shape[0], n_ctx=x_prompt.shape[1],
              n_lat_req=x_sample.shape[0], n_lat=x_sample.shape[1], grid_w=GRID_W)
    return _forward(cfg, x_prompt, x_sample, state_lru, c, c_ctx, norm1_g, norm2_g, final_g, w_mod, b_mod, w_in,
                    pool_w, pool_scale, conv_w, conv_b, lru_wr, lru_br, lru_wi, lru_bi, lru_lambda,
                    w_br_pool, w_br_lru, w_out, router_w, exp_w1, exp_w3, exp_w2)
```

```python
import functools
from typing import NamedTuple

import jax
import jax.numpy as jnp
from jax import lax
from jax.experimental import pallas as pl
from jax.experimental.pallas import tpu as pltpu

F32 = jnp.float32
BF16 = jnp.bfloat16
I32 = jnp.int32
HIGHEST = lax.Precision.HIGHEST

EPS = 1e-6
LRU_C = 8.0
CONV_WIDTH = 4
POOL_HALF_WINDOWS = (1, 2, 4, 8)
N_MOD = 6
CAPACITY_FACTOR = 2
GRID_W = 64
NOT_SELECTED = -(1 << 20)
SUBLANES = 8
BF16_ROWS = 16
LANES = 128
MXU_DIM = 256
VMEM_LIMIT = 56 * 1024 * 1024


class Cfg(NamedTuple):
    d: int
    gw: int
    r: int
    dh: int
    e: int
    f: int
    n_ctx_req: int
    n_ctx: int
    n_lat_req: int
    n_lat: int
    grid_w: int

    @property
    def t(self): return self.n_ctx
    @property
    def tb(self): return 2 * self.n_ctx
    @property
    def nbc(self): return self.n_ctx_req * self.n_ctx // self.tb
    @property
    def p(self): return self.gw * len(POOL_HALF_WINDOWS)
    @property
    def ntc(self): return self.n_ctx_req
    @property
    def tpr(self): return self.n_lat // self.t
    @property
    def nt(self): return self.ntc + self.n_lat_req * self.tpr
    @property
    def n_tok(self): return self.nt * self.t
    @property
    def c_ctx(self): return CAPACITY_FACTOR * self.n_ctx // self.e
    @property
    def c_lat(self): return CAPACITY_FACTOR * self.n_lat // self.e
    @property
    def sblk(self): return self.c_lat
    @property
    def rpb(self): return self.c_lat // self.c_ctx
    @property
    def n_sblk(self): return self.n_ctx_req // self.rpb + self.n_lat_req
    @property
    def w(self): return min(64, self.sblk)
    @property
    def bd(self): return min(MXU_DIM, self.r)
    @property
    def n_tab(self): return 16


def _cparams():
    return pltpu.CompilerParams(dimension_semantics=("arbitrary",), vmem_limit_bytes=VMEM_LIMIT)


def _tab_row(i, cfg):
    return jnp.where(i < cfg.ntc, 0, 1 + (i - cfg.ntc) // cfg.tpr)


def _sigmoid(x):
    return 0.5 * jnp.tanh(0.5 * x) + 0.5


def _lat_pos(i, cfg):
    j = jnp.maximum(i - cfg.ntc, 0)
    return j // cfg.tpr, j % cfg.tpr


def _x_specs(cfg, tile_of):
    def ctx_map(s, *_):
        return (jnp.minimum(tile_of(s), cfg.ntc - 1), 0, 0)

    def lat_map(s, *_):
        b, k = _lat_pos(tile_of(s), cfg)
        return (b, k, 0)

    return [pl.BlockSpec((1, cfg.t, cfg.d), ctx_map), pl.BlockSpec((1, cfg.t, cfg.d), lat_map)]


def _norm_mod(x, g, scale, shift):
    y = x * lax.rsqrt(jnp.mean(x * x, axis=-1, keepdims=True) + EPS)
    return y * (g * (1.0 + scale)) + shift


def _mod_kernel(c_ref, w_ref, b_ref, o_ref):
    c = c_ref[...]
    s = c * _sigmoid(c)
    o_ref[0] = jnp.dot(s, w_ref[0], precision=HIGHEST, preferred_element_type=F32) + b_ref[0]


def _modulation(cvec, w_mod, b_mod, cfg):
    depth, d, n6 = w_mod.shape
    nc = n6 // 4
    return pl.pallas_call(
        _mod_kernel,
        grid=(depth, n6 // nc),
        in_specs=[pl.BlockSpec((cfg.n_tab, d), lambda l, j: (0, 0)),
                  pl.BlockSpec((1, d, nc), lambda l, j: (l, 0, j)),
                  pl.BlockSpec((1, 1, nc), lambda l, j: (l, 0, j))],
        out_specs=pl.BlockSpec((1, cfg.n_tab, nc), lambda l, j: (l, 0, j)),
        out_shape=jax.ShapeDtypeStruct((depth, cfg.n_tab, n6), F32),
        compiler_params=pltpu.CompilerParams(dimension_semantics=("arbitrary", "arbitrary"),
                                             vmem_limit_bytes=VMEM_LIMIT),
        name="modulation",
    )(cvec, w_mod, b_mod.reshape(depth, 1, n6))


def _conv(v, prev, nxt, w, b):
    t = v.shape[0]
    row = lax.broadcasted_iota(I32, v.shape, 0)
    vm1 = jnp.where(row == 0, prev[7:8], pltpu.roll(v, 1, 0))
    vm2 = jnp.where(row == 0, prev[6:7], jnp.where(row == 1, prev[7:8], pltpu.roll(v, 2, 0)))
    vp1 = jnp.where(row == t - 1, nxt[0:1], pltpu.roll(v, t - 1, 0))
    return b + vm2 * w[0:1] + vm1 * w[1:2] + v * w[2:3] + vp1 * w[3:4]


def _to_slab(slab, val, col0):
    nj = slab.shape[0] // val.shape[0]
    for g in range(val.shape[0] // SUBLANES):
        for j in range(val.shape[1] // LANES):
            dst = pl.ds((g * nj + col0 // LANES + j) * SUBLANES, SUBLANES)
            slab[dst, :] = val[g * SUBLANES:(g + 1) * SUBLANES, j * LANES:(j + 1) * LANES]


def _gates(vc, wbd_ref, br, bi, lam, sa, su, cfg):
    bd = cfg.bd
    vcb = vc.astype(BF16)
    nl = -lam
    half_rate = (0.5 * LRU_C) * (jnp.maximum(nl, 0.0) + jnp.log1p(jnp.exp(-jnp.abs(nl))))
    for j in range(cfg.r // bd):
        sl = slice(j * bd, (j + 1) * bd)
        z = jnp.dot(vcb[:, sl], wbd_ref[j], preferred_element_type=F32)
        ig = 0.5 * jnp.tanh(z[:, bd:] + bi[:, sl]) + 0.5
        m = jnp.tanh(z[:, :bd] + br[:, sl]) * half_rate[:, sl] + half_rate[:, sl]
        a = jnp.exp(-m)
        _to_slab(sa, a, j * bd)
        _to_slab(su, jnp.sqrt(jnp.tanh(m) * (a * a + 1.0)) * ig * vc[:, sl], j * bd)


def _scan_rows(out_ref, h_in, reverse, sa, su, sh):
    t, r = out_ref.shape
    nj = r // LANES
    groups = t // SUBLANES
    for j in range(nj):
        sh[j:j + 1, :] = h_in[:, j * LANES:(j + 1) * LANES]
    h0 = sh[0:nj, :]

    def body(k, h):
        g = groups - 1 - k if reverse else k
        base = g * (nj * SUBLANES)
        for row in (range(SUBLANES - 1, -1, -1) if reverse else range(SUBLANES)):
            src = pl.ds(base + row, nj, stride=SUBLANES)
            h = sa[src, :] * h + su[src, :]
            sh[src, :] = h
        return h

    h = lax.fori_loop(0, groups, body, h0, unroll=2)
    for g in range(t // SUBLANES):
        for j in range(nj):
            out_ref[g * SUBLANES:(g + 1) * SUBLANES, j * LANES:(j + 1) * LANES] = \
                sh[pl.ds((g * nj + j) * SUBLANES, SUBLANES), :]
    sh[0:nj, :] = h
    return jnp.concatenate([sh[j:j + 1, :] for j in range(nj)], axis=1)


def _tile_flags(i, cfg):
    is_ctx = i < cfg.ntc
    k = jnp.maximum(i - cfg.ntc, 0) % cfg.tpr
    first = jnp.logical_or(is_ctx, k == 0)
    last = jnp.logical_or(is_ctx, k == cfg.tpr - 1)
    return is_ctx, first, last


def _cast_chunks(w, layer, n_chunks, chunk_of):
    depth, e, rows, cols = w.shape
    assert (e * rows) % (n_chunks * BF16_ROWS) == 0
    cr = e * rows // n_chunks
    return (w.reshape(depth, n_chunks, cr, cols),
            pl.BlockSpec((1, 1, cr, cols), lambda s: (layer, chunk_of(s), 0, 0)),
            pl.BlockSpec((1, cr, cols), lambda s: (chunk_of(s), 0, 0)),
            jax.ShapeDtypeStruct((n_chunks, cr, cols), BF16))


def _scan_bwd_kernel(xc_ref, xl_ref, xprev_ref, xnext_ref, mod_ref, g1_ref, wv_ref, cw_ref, cb_ref, wbd_ref,
                     br_ref, bi_ref, lam_ref, h0_ref, ew_ref, vc_ref, hb_ref, hlast_ref, ewb_ref,
                     carry_scr, sa, su, sh, *, cfg):
    i = cfg.nt - 1 - pl.program_id(0)
    t, d = cfg.t, cfg.d
    is_ctx, first, last = _tile_flags(i, cfg)
    ewb_ref[0] = ew_ref[0, 0].astype(BF16)
    x_ext = jnp.concatenate([xprev_ref[0], jnp.where(is_ctx, xc_ref[0], xl_ref[0]), xnext_ref[0]], axis=0)
    m = mod_ref[0]
    hn = _norm_mod(x_ext, g1_ref[...], m[:, d:2 * d], m[:, 0:d]).astype(BF16)
    v_ext = jnp.dot(hn, wv_ref[...], preferred_element_type=F32)
    prev = jnp.where(first, 0.0, v_ext[0:SUBLANES])
    nxt = jnp.where(last, 0.0, v_ext[t + SUBLANES:t + 2 * SUBLANES])
    vc = _conv(v_ext[SUBLANES:t + SUBLANES], prev, nxt, cw_ref[...], cb_ref[...])
    vc_ref[...] = vc
    _gates(vc, wbd_ref, br_ref[...], bi_ref[...], lam_ref[...], sa, su, cfg)

    @pl.when(last)
    def _():
        carry_scr[...] = h0_ref[0]

    h = _scan_rows(hb_ref, carry_scr[...], True, sa, su, sh)
    carry_scr[...] = h

    @pl.when(is_ctx)
    def _():
        hlast_ref[0] = h


def _scan_bwd(xc, xl, mod, g1, wv, cw, cb, wbd, br, bi, lam, h0, ew, layer, cfg):
    t, d, r, nt = cfg.t, cfg.d, cfg.r, cfg.nt
    n = cfg.n_tok
    tb = t // SUBLANES
    rev = lambda s: nt - 1 - s
    full = lambda a: pl.BlockSpec(a.shape, lambda s: (0,) * a.ndim)
    tile = pl.BlockSpec((t, r), lambda s: (rev(s), 0))
    tab = lambda width: pl.BlockSpec((1, 1, width), lambda s: (_tab_row(rev(s), cfg), 0, 0))

    def prev_map(s):
        b, k = _lat_pos(rev(s), cfg)
        return (b, jnp.maximum(k * tb - 1, 0), 0)

    def next_map(s):
        b, k = _lat_pos(rev(s), cfg)
        return (b, jnp.minimum((k + 1) * tb, cfg.n_lat // SUBLANES - 1), 0)

    ewc, ew_in, ew_out, ew_shape = _cast_chunks(ew, layer, nt - cfg.ntc,
                                                lambda s: jnp.maximum(rev(s) - cfg.ntc, 0))
    vc, hb, hb_last, ewb = pl.pallas_call(
        functools.partial(_scan_bwd_kernel, cfg=cfg),
        grid=(nt,),
        in_specs=_x_specs(cfg, rev) + [
            pl.BlockSpec((1, SUBLANES, d), prev_map), pl.BlockSpec((1, SUBLANES, d), next_map),
            tab(N_MOD * d), full(g1), full(wv), full(cw), full(cb), full(wbd), full(br), full(bi), full(lam),
            tab(r), ew_in],
        out_specs=[tile, tile,
                   pl.BlockSpec((1, 1, r), lambda s: (jnp.minimum(rev(s), cfg.ntc - 1), 0, 0)), ew_out],
        out_shape=[jax.ShapeDtypeStruct((n, r), F32),
                   jax.ShapeDtypeStruct((n, r), F32),
                   jax.ShapeDtypeStruct((cfg.n_ctx_req, 1, r), F32), ew_shape],
        scratch_shapes=[pltpu.VMEM((1, r), F32)] + [pltpu.VMEM((t * r // LANES, LANES), F32)] * 3,
        compiler_params=_cparams(),
        name="scan_bwd",
    )(xc, xl, xl, xl, mod, g1, wv, cw, cb, wbd, br, bi, lam, h0, ewc)
    return vc, hb, hb_last, ewb.reshape(ew.shape[1:])


def _pool(pv, is_ctx, cfg):
    t = pv.shape[0]
    gw = cfg.gw
    row = lax.broadcasted_iota(I32, (t, gw), 0)
    pos = jnp.where(is_ctx, row % cfg.n_ctx, row % cfg.grid_w)
    length = jnp.where(is_ctx, cfg.n_ctx, cfg.grid_w)

    def shifted(x, dlt):
        ok = jnp.logical_and(pos + dlt >= 0, pos + dlt < length)
        return jnp.where(ok, pltpu.roll(x, (-dlt) % t, 0), 0.0)

    outs = []
    for g, m in enumerate(POOL_HALF_WINDOWS):
        x = pv[:, g * gw:(g + 1) * gw]
        back, fwd, k = x, x, 1
        while k < m:
            back = back + shifted(back, -k)
            fwd = fwd + shifted(fwd, k)
            k *= 2
        s = shifted(back, -1) + fwd
        cnt = (jnp.minimum(pos + m, length) - jnp.maximum(pos - m, 0)).astype(F32)
        outs.append(s / cnt - x)
    return outs


def _scan_fwd_kernel(vc_ref, h0_ref, wbd_ref, br_ref, bi_ref, lam_ref, ew_ref,
                     hf_ref, hlast_ref, ewb_ref, carry_scr, sa, su, sh, *, cfg):
    i = pl.program_id(0)
    is_ctx, first, _ = _tile_flags(i, cfg)
    ewb_ref[0] = ew_ref[0, 0].astype(BF16)
    _gates(vc_ref[...], wbd_ref, br_ref[...], bi_ref[...], lam_ref[...], sa, su, cfg)

    @pl.when(first)
    def _():
        carry_scr[...] = h0_ref[0]

    h = _scan_rows(hf_ref, carry_scr[...], False, sa, su, sh)
    carry_scr[...] = h

    @pl.when(is_ctx)
    def _():
        hlast_ref[0] = h


def _scan_fwd(vc, h0, wbd, br, bi, lam, ew, layer, cfg):
    t, r = cfg.t, cfg.r
    tile = pl.BlockSpec((t, r), lambda i: (i, 0))
    full = lambda a: pl.BlockSpec(a.shape, lambda i: (0,) * a.ndim)
    ewc, ew_in, ew_out, ew_shape = _cast_chunks(ew, layer, cfg.nt - cfg.ntc,
                                                lambda i: jnp.maximum(i - cfg.ntc, 0))
    hf, hf_last, ewb = pl.pallas_call(
        functools.partial(_scan_fwd_kernel, cfg=cfg),
        grid=(cfg.nt,),
        in_specs=[tile, pl.BlockSpec((1, 1, r), lambda i: (_tab_row(i, cfg), 0, 0)),
                  full(wbd), full(br), full(bi), full(lam), ew_in],
        out_specs=[tile, pl.BlockSpec((1, 1, r), lambda i: (jnp.minimum(i, cfg.ntc - 1), 0, 0)), ew_out],
        out_shape=[jax.ShapeDtypeStruct((cfg.n_tok, r), F32),
                   jax.ShapeDtypeStruct((cfg.n_ctx_req, 1, r), F32), ew_shape],
        scratch_shapes=[pltpu.VMEM((1, r), F32)] + [pltpu.VMEM((t * r // LANES, LANES), F32)] * 3,
        compiler_params=_cparams(),
        name="scan_fwd",
    )(vc, h0, wbd, br, bi, lam, ewc)
    return hf, hf_last, ewb.reshape(ew.shape[1:])


def _mix_kernel(xc_ref, xl_ref, hf_ref, hb_ref, mod_ref, g1_ref, wp_ref, wg_ref, pw_ref, ps_ref, wbp_ref,
                wbl_ref, wo_ref, g2_ref, rwa_ref, rwb_ref, ew_ref, x1_ref, hn2_ref, lg_ref, ewb_ref, *, cfg):
    i = pl.program_id(0)
    ewb_ref[0] = ew_ref[0, 0].astype(BF16)
    d, gw = cfg.d, cfg.gw
    is_ctx = i < cfg.nbc
    m = mod_ref[0]
    ps = ps_ref[...]
    dot = functools.partial(jnp.dot, preferred_element_type=F32)
    halves = [slice(h * cfg.n_ctx, (h + 1) * cfg.n_ctx) for h in range(cfg.tb // cfg.n_ctx)]
    st = [dict() for _ in halves]

    def head(s, rs):
        s['x'] = jnp.where(is_ctx, xc_ref[rs, :], xl_ref[rs, :])
        s['hn'] = _norm_mod(s['x'], g1_ref[...], m[:, d:2 * d], m[:, 0:d]).astype(BF16)

    def project(s, rs):
        s['p'] = dot(s['hn'], wp_ref[...])
        s['gl'] = dot(s['hn'], wg_ref[...])

    def pool(s, rs):
        s['y_lru'] = (hf_ref[rs, :] + hb_ref[rs, :]).astype(BF16)
        s['pooled'] = [q.astype(BF16) for q in _pool(s.pop('p'), is_ctx, cfg)]

    def branches(s, rs):
        y_pool = jnp.concatenate([dot(q, pw_ref[g]) * ps[:, g * gw:(g + 1) * gw]
                                  for g, q in enumerate(s.pop('pooled'))], axis=1).astype(BF16)
        s['bp'] = dot(y_pool, wbp_ref[...])
        s['bl'] = dot(s.pop('y_lru'), wbl_ref[...])

    def merge(s, rs):
        gl = s.pop('gl')
        s['merged'] = (_sigmoid(gl[:, :d]) * s.pop('bp') + _sigmoid(gl[:, d:]) * s.pop('bl')).astype(BF16)

    def out_proj(s, rs):
        s['mix'] = dot(s.pop('merged'), wo_ref[...])

    def tail(s, rs):
        x1 = s.pop('x') + m[:, 2 * d:3 * d] * s.pop('mix')
        x1_ref[rs, :] = x1
        hn2 = _norm_mod(x1, g2_ref[...], m[:, 4 * d:5 * d], m[:, 3 * d:4 * d])
        hi = hn2.astype(BF16)
        hn2_ref[rs, :] = hi
        lo = (hn2 - hi.astype(F32)).astype(BF16)
        nt = functools.partial(lax.dot_general, dimension_numbers=(((1,), (1,)), ((), ())),
                               preferred_element_type=F32)
        lg_ref[:, rs] = nt(rwa_ref[...], hi) + nt(rwb_ref[...], hi) + nt(rwa_ref[...], lo)

    for stage in (head, project, pool, branches, merge, out_proj, tail):
        for s, rs in zip(st, halves):
            stage(s, rs)


def _mix(xc, xl, hf, hb, mod, g1, wp, wg, pw, ps, wbp, wbl, wo, g2, rwa, rwb, ew, layer, cfg):
    tb, d, r = cfg.tb, cfg.d, cfg.r
    n = cfg.n_tok
    nbc = cfg.nbc
    tile = lambda width: pl.BlockSpec((tb, width), lambda i: (i, 0))
    full = lambda a: pl.BlockSpec(a.shape, lambda i: (0,) * a.ndim)
    tab_row = lambda i: jnp.where(i < nbc, 0, 1 + (i - nbc) // (cfg.n_lat // tb))
    ewc, ew_in, ew_out, ew_shape = _cast_chunks(ew, layer, n // tb - nbc, lambda i: jnp.maximum(i - nbc, 0))
    x1, hn2, logits, ewb = pl.pallas_call(
        functools.partial(_mix_kernel, cfg=cfg),
        grid=(n // tb,),
        in_specs=[pl.BlockSpec((tb, d), lambda i: (jnp.minimum(i, nbc - 1), 0)),
                  pl.BlockSpec((tb, d), lambda i: (jnp.maximum(i - nbc, 0), 0)),
                  tile(r), tile(r), pl.BlockSpec((1, 1, N_MOD * d), lambda i: (tab_row(i), 0, 0)),
                  full(g1), full(wp), full(wg), full(pw), full(ps), full(wbp), full(wbl), full(wo),
                  full(g2), full(rwa), full(rwb), ew_in],
        out_specs=[tile(d), tile(d), pl.BlockSpec((cfg.e, tb), lambda i: (0, i)), ew_out],
        out_shape=[jax.ShapeDtypeStruct((n, d), F32),
                   jax.ShapeDtypeStruct((n, d), BF16),
                   jax.ShapeDtypeStruct((cfg.e, n), F32), ew_shape],
        compiler_params=_cparams(),
        name="mix",
    )(xc.reshape(-1, d), xl.reshape(-1, d), hf, hb, mod, g1, wp, wg, pw, ps, wbp, wbl, wo, g2, rwa, rwb, ewc)
    return x1, hn2, logits, ewb.reshape(ew.shape[1:])


def _route_kernel(l_ref, selpos_ref, gate_ref, offs_ref, cnts_ref, *, cap, t):
    lg = l_ref[...]
    nreq, e, n = lg.shape
    rows = nreq * e
    nch = n // t
    ex = jnp.exp(lg - jnp.max(lg, axis=1, keepdims=True))
    aff = (ex / jnp.sum(ex, axis=1, keepdims=True)).reshape(rows, n)
    gate_ref[...] = aff
    keys = pltpu.bitcast(aff, I32)

    def count(mask):
        return jnp.sum(mask.astype(F32), axis=1, keepdims=True)

    thr = jnp.zeros((rows, 1), I32)
    for bit in range(30, -1, -1):
        cand = thr | (1 << bit)
        thr = jnp.where(count(keys >= cand) >= cap, cand, thr)
    gt = keys > thr
    eq = keys == thr
    need = cap - count(gt)

    tri = (lax.broadcasted_iota(I32, (t, t), 0) < lax.broadcasted_iota(I32, (t, t), 1)).astype(BF16)
    tok = lax.broadcasted_iota(I32, (n, LANES), 0)
    chunk = lax.broadcasted_iota(I32, (n, LANES), 1)
    before = (tok < chunk * t).astype(BF16)
    inside = (tok // t == chunk).astype(BF16)

    def prefix(mask):
        mb = mask.astype(BF16)
        offs = jnp.dot(mb, before, preferred_element_type=F32)
        pre = [jnp.dot(mb[:, k * t:(k + 1) * t], tri, preferred_element_type=F32) + offs[:, k:k + 1]
               for k in range(nch)]
        return jnp.concatenate(pre, axis=1) if nch > 1 else pre[0], offs, mb

    eq_rank, _, _ = prefix(eq)
    sel = jnp.logical_or(gt, jnp.logical_and(eq, eq_rank < need))
    pos, offs, selb = prefix(sel)
    selpos_ref[...] = jnp.where(sel, pos.astype(I32), NOT_SELECTED)
    offs_ref[...] = offs.astype(I32)
    cnts_ref[...] = jnp.dot(selb, inside, preferred_element_type=F32).astype(I32)


def _route(logits, cap, cfg):
    nreq, e, n = logits.shape
    rows = nreq * e
    whole = lambda width: pl.BlockSpec((rows, width), lambda b: (0, 0))
    return pl.pallas_call(
        functools.partial(_route_kernel, cap=cap, t=cfg.t),
        grid=(1,),
        in_specs=[pl.BlockSpec((nreq, e, n), lambda b: (0, 0, 0))],
        out_specs=[whole(n), whole(n), whole(LANES), whole(LANES)],
        out_shape=[jax.ShapeDtypeStruct((rows, n), I32),
                   jax.ShapeDtypeStruct((rows, n), F32),
                   jax.ShapeDtypeStruct((rows, LANES), I32),
                   jax.ShapeDtypeStruct((rows, LANES), I32)],
        compiler_params=_cparams(),
        name="route",
    )(logits)


def _slot_block(i, cfg):
    return jnp.where(i < cfg.ntc, i // cfg.rpb, cfg.ntc // cfg.rpb + (i - cfg.ntc) // cfg.tpr)


def _windows(i, toff_ref, tcnt_ref, cfg):
    is_ctx = i < cfg.ntc
    base = jnp.where(is_ctx, (i % cfg.rpb) * cfg.c_ctx, 0)
    starts, npass = [], 0
    for e in range(cfg.e):
        off = base + toff_ref[i * cfg.e + e]
        start = (off // BF16_ROWS) * BF16_ROWS
        starts.append(start)
        npass = jnp.maximum(npass, (off + tcnt_ref[i * cfg.e + e] - start + cfg.w - 1) // cfg.w)
    return base, starts, npass


def _window(start, j, cfg):
    want = start + j * cfg.w
    a = pl.multiple_of(jnp.minimum(want, cfg.sblk - cfg.w), BF16_ROWS)
    return a, want - a


def _dispatch_kernel(toff_ref, tcnt_ref, x_ref, sp_ref, g_ref, xs_ref, gs_ref, p_scr, *, cfg):
    i = pl.program_id(0)
    w = cfg.w
    is_ctx = i < cfg.ntc
    new_block = jnp.where(is_ctx, i % cfg.rpb == 0, jnp.maximum(i - cfg.ntc, 0) % cfg.tpr == 0)

    @pl.when(new_block)
    def _():
        xs_ref[...] = jnp.zeros(xs_ref.shape, xs_ref.dtype)
        gs_ref[...] = jnp.zeros(gs_ref.shape, gs_ref.dtype)

    base, starts, npass = _windows(i, toff_ref, tcnt_ref, cfg)
    slot = lax.broadcasted_iota(I32, (w, cfg.t), 0)

    def one_pass(j, carry):
        firsts = []
        for e in range(cfg.e):
            a, owned = _window(starts[e], j, cfg)
            firsts.append(a)
            rel = sp_ref[e:e + 1, :] - (a - base)
            hit = jnp.logical_and(rel == slot, slot >= owned)
            p_scr[e * w:(e + 1) * w, :] = hit.astype(BF16)
            gs_ref[e, pl.ds(a, w), :] += jnp.sum(jnp.where(hit, g_ref[e:e + 1, :], 0.0), axis=1, keepdims=True)
        rows = jnp.dot(p_scr[...], x_ref[...], preferred_element_type=F32)
        for e in range(cfg.e):
            xs_ref[e, pl.ds(firsts[e], w), :] += rows[e * w:(e + 1) * w].astype(BF16)
        return carry

    lax.fori_loop(0, npass, one_pass, 0)


def _dispatch(toff, tcnt, hn2, selpos, gate, cfg):
    t, d, e = cfg.t, cfg.d, cfg.e
    s = cfg.n_sblk * cfg.sblk
    slots = lambda width: pl.BlockSpec((e, cfg.sblk, width), lambda i, *_: (0, _slot_block(i, cfg), 0))
    return pl.pallas_call(
        functools.partial(_dispatch_kernel, cfg=cfg),
        grid_spec=pltpu.PrefetchScalarGridSpec(
            num_scalar_prefetch=2,
            grid=(cfg.nt,),
            in_specs=[pl.BlockSpec((t, d), lambda i, *_: (i, 0)),
                      pl.BlockSpec((e, t), lambda i, *_: (0, i)),
                      pl.BlockSpec((e, t), lambda i, *_: (0, i))],
            out_specs=[slots(d), slots(1)],
            scratch_shapes=[pltpu.VMEM((e * cfg.w, t), BF16)]),
        out_shape=[jax.ShapeDtypeStruct((e, s, d), BF16), jax.ShapeDtypeStruct((e, s, 1), F32)],
        compiler_params=_cparams(),
        name="dispatch",
    )(toff, tcnt, hn2, selpos, gate)


def _ffn_kernel(x_ref, gs_ref, w1_ref, w3_ref, w2_ref, o_ref, *, fc):
    x = x_ref[0]
    acc = jnp.zeros((x.shape[0], o_ref.shape[2]), F32)
    for c in range(w1_ref.shape[2] // fc):
        sl = slice(c * fc, (c + 1) * fc)
        h1 = jnp.dot(x, w1_ref[0, :, sl], preferred_element_type=F32)
        h3 = jnp.dot(x, w3_ref[0, :, sl], preferred_element_type=F32)
        hid = (h1 * _sigmoid(h1)) * h3
        acc = acc + jnp.dot(hid.astype(BF16), w2_ref[0, sl, :], preferred_element_type=F32)
    o_ref[0] = (acc * gs_ref[0]).astype(BF16)


def _ffn(xs, gs, w1, w3, w2, cfg):
    e, s, d = xs.shape
    f = w1.shape[2]
    tm = 2 * cfg.sblk if cfg.n_sblk % 2 == 0 else cfg.sblk
    return pl.pallas_call(
        functools.partial(_ffn_kernel, fc=min(512, f)),
        grid=(e, s // tm),
        in_specs=[pl.BlockSpec((1, tm, d), lambda k, m: (k, m, 0)),
                  pl.BlockSpec((1, tm, 1), lambda k, m: (k, m, 0)),
                  pl.BlockSpec((1, d, f), lambda k, m: (k, 0, 0)),
                  pl.BlockSpec((1, d, f), lambda k, m: (k, 0, 0)),
                  pl.BlockSpec((1, f, d), lambda k, m: (k, 0, 0))],
        out_specs=pl.BlockSpec((1, tm, d), lambda k, m: (k, m, 0)),
        out_shape=jax.ShapeDtypeStruct((e, s, d), BF16),
        compiler_params=pltpu.CompilerParams(dimension_semantics=("arbitrary", "arbitrary"),
                                             vmem_limit_bytes=VMEM_LIMIT),
        name="ffn",
    )(xs, gs, w1, w3, w2)


def _combine_kernel(toff_ref, tcnt_ref, x1_ref, mod_ref, spt_ref, o_ref, fg_ref, oc_ref, ol_ref, ow_scr,
                    *, cfg, final):
    i = pl.program_id(0)
    w, d = cfg.w, cfg.d
    base, starts, npass = _windows(i, toff_ref, tcnt_ref, cfg)
    ne = cfg.e
    expert = lax.broadcasted_iota(I32, (1, ne), 1)
    expand = (lax.broadcasted_iota(I32, (ne, ne * w), 1) // w
              == lax.broadcasted_iota(I32, (ne, ne * w), 0)).astype(BF16)
    row_in_window = (lax.broadcasted_iota(I32, (cfg.t, ne * w), 1) % w).astype(F32)
    spt = spt_ref[...]

    def one_pass(j):
        shift = jnp.zeros((1, ne), I32)
        owned = jnp.zeros((1, ne), I32)
        for e in range(ne):
            a, own = _window(starts[e], j, cfg)
            ow_scr[e * w:(e + 1) * w, :] = o_ref[e, pl.ds(a, w), :]
            shift = jnp.where(expert == e, a - base, shift)
            owned = jnp.where(expert == e, own, owned)
        rel = spt - shift
        rel = jnp.where(jnp.logical_and(rel >= owned, rel < w), rel, -1)
        wide = jnp.dot(rel.astype(F32).astype(BF16), expand, preferred_element_type=F32)
        hits = (wide == row_in_window).astype(BF16)
        return jnp.dot(hits, ow_scr[...], preferred_element_type=F32)

    is_ctx = i < cfg.ntc

    def finish(y):
        x2 = x1_ref[...] + mod_ref[0][:, 5 * d:6 * d] * y
        if final:
            x2 = (x2 * lax.rsqrt(jnp.mean(x2 * x2, axis=-1, keepdims=True) + EPS)) * fg_ref[...]

        @pl.when(is_ctx)
        def _():
            oc_ref[0] = x2

        @pl.when(jnp.logical_not(is_ctx))
        def _():
            ol_ref[0] = x2

    @pl.when(npass <= 1)
    def _():
        finish(one_pass(0))

    @pl.when(npass > 1)
    def _():
        finish(lax.fori_loop(1, npass, lambda j, y: y + one_pass(j), one_pass(0)))


def _combine(toff, tcnt, x1, mod, selpos_t, o, fg, cfg, final):
    t, d, e = cfg.t, cfg.d, cfg.e
    return pl.pallas_call(
        functools.partial(_combine_kernel, cfg=cfg, final=final),
        grid_spec=pltpu.PrefetchScalarGridSpec(
            num_scalar_prefetch=2,
            grid=(cfg.nt,),
            in_specs=[pl.BlockSpec((t, d), lambda i, *_: (i, 0)),
                      pl.BlockSpec((1, 1, N_MOD * d), lambda i, *_: (_tab_row(i, cfg), 0, 0)),
                      pl.BlockSpec((t, e), lambda i, *_: (i, 0)),
                      pl.BlockSpec((e, cfg.sblk, d), lambda i, *_: (0, _slot_block(i, cfg), 0)),
                      pl.BlockSpec((1, d), lambda i, *_: (0, 0))],
            out_specs=_x_specs(cfg, lambda i: i),
            scratch_shapes=[pltpu.VMEM((e * cfg.w, d), BF16)]),
        out_shape=[jax.ShapeDtypeStruct((cfg.n_ctx_req, cfg.n_ctx, d), F32),
                   jax.ShapeDtypeStruct((cfg.n_lat_req, cfg.n_lat, d), F32)],
        compiler_params=_cparams(),
        name="combine",
    )(toff, tcnt, x1, mod, selpos_t, o, fg)


def _block_diag_gates(wr, wi, cfg):
    hpb = cfg.bd // cfg.dh
    nblk = cfg.r // cfg.bd
    eye = jnp.eye(hpb, dtype=wr.dtype)

    def blocks(w):
        w4 = w.reshape(nblk, hpb, cfg.dh, cfg.dh)
        return jnp.einsum('jhde,hk->jhdke', w4, eye).reshape(nblk, cfg.bd, cfg.bd)

    return (0.5 * jnp.concatenate([blocks(wr), blocks(wi)], axis=2)).astype(BF16)


def _tile_tables(offs_ctx, offs_lat, cfg):
    ctx = offs_ctx[:, 0].reshape(-1, cfg.e)
    lat = jnp.swapaxes(offs_lat[:, :cfg.tpr].reshape(-1, cfg.e, cfg.tpr), 1, 2).reshape(-1, cfg.e)
    return jnp.concatenate([ctx, lat], axis=0).reshape(-1)


def _by_expert(rows_ctx, rows_lat, cfg):
    def flip(a):
        return jnp.swapaxes(a.reshape(-1, cfg.e, a.shape[1]), 0, 1).reshape(cfg.e, -1)
    return jnp.concatenate([flip(rows_ctx), flip(rows_lat)], axis=1)


def _forward(cfg, x_prompt, x_sample, state_lru, c, c_ctx, norm1_g, norm2_g, final_g, w_mod, b_mod, w_in,
             pool_w, pool_scale, conv_w, conv_b, lru_wr, lru_br, lru_wi, lru_bi, lru_lambda,
             w_br_pool, w_br_lru, w_out, router_w, exp_w1, exp_w3, exp_w2):
    d, r, p = cfg.d, cfg.r, cfg.p
    depth = w_in.shape[0]
    assert cfg.n_lat % cfg.t == 0 and cfg.t % cfg.grid_w == 0 and cfg.n_ctx_req % cfg.rpb == 0
    assert cfg.w % BF16_ROWS == 0
    assert (cfg.n_ctx_req * cfg.n_ctx) % cfg.tb == 0 and cfg.n_lat % cfg.tb == 0 and cfg.tb % cfg.grid_w == 0

    n_ctx_tok = cfg.n_ctx_req * cfg.n_ctx
    xc, xl = x_prompt, x_sample

    cvec = jnp.zeros((cfg.n_tab, d), F32).at[0].set(c_ctx).at[1:1 + cfg.n_lat_req].set(c)
    mod = _modulation(cvec, w_mod, b_mod, cfg)

    states = []
    for l in range(depth):
        mod_l = mod[l].reshape(cfg.n_tab, 1, N_MOD * d)
        row = lambda a: a.reshape(1, -1)
        wp = w_in[l][:, :p].astype(BF16)
        wv = w_in[l][:, p:p + r].astype(BF16)
        wg = w_in[l][:, p + r:].astype(BF16)
        rwa = router_w[l].T.astype(BF16)
        rwb = (router_w[l].T - rwa.astype(F32)).astype(BF16)

        def h0(direction):
            tab = jnp.zeros((cfg.n_tab, 1, r), F32)
            return tab.at[1:1 + cfg.n_lat_req, 0].set(state_lru[:, l, direction].astype(F32))

        wbd = [_block_diag_gates(lru_wr[l, z], lru_wi[l, z], cfg) for z in range(2)]
        vc, hb, hb_last, w1b = _scan_bwd(xc, xl, mod_l, row(norm1_g[l]), wv, conv_w[l], row(conv_b[l]), wbd[1],
                                         row(0.5 * lru_br[l, 1]), row(0.5 * lru_bi[l, 1]),
                                         row(lru_lambda[l, 1]), h0(1), exp_w1, l, cfg)
        hf, hf_last, w3b = _scan_fwd(vc, h0(0), wbd[0], row(0.5 * lru_br[l, 0]), row(0.5 * lru_bi[l, 0]),
                                     row(lru_lambda[l, 0]), exp_w3, l, cfg)
        x1, hn2, logits, w2b = _mix(xc, xl, hf, hb, mod_l, row(norm1_g[l]), wp, wg, pool_w[l].astype(BF16),
                                    row(pool_scale[l]), w_br_pool[l].astype(BF16), w_br_lru[l].astype(BF16),
                                    w_out[l].astype(BF16), row(norm2_g[l]), rwa, rwb, exp_w2, l, cfg)
        states.append(jnp.stack([hf_last[:, 0], hb_last[:, 0]], axis=1))

        by_request = lambda a, n: jnp.swapaxes(a.reshape(cfg.e, -1, n), 0, 1)
        sp_c, g_c, off_c, cnt_c = _route(by_request(logits[:, :n_ctx_tok], cfg.n_ctx), cfg.c_ctx, cfg)
        sp_l, g_l, off_l, cnt_l = _route(by_request(logits[:, n_ctx_tok:], cfg.n_lat), cfg.c_lat, cfg)
        selpos = _by_expert(sp_c, sp_l, cfg)
        toff = _tile_tables(off_c, off_l, cfg)
        tcnt = _tile_tables(cnt_c, cnt_l, cfg)

        xs, gs = _dispatch(toff, tcnt, hn2, selpos, _by_expert(g_c, g_l, cfg), cfg)
        o = _ffn(xs, gs, w1b, w3b, w2b, cfg)
        xc, xl = _combine(toff, tcnt, x1, mod_l, selpos.T, o, row(final_g), cfg, l == depth - 1)

    new_state = jnp.stack(states, axis=1).astype(x_prompt.dtype)
    return xc, xl, new_state


def kernel(x_prompt, x_sample, state_lru, c, c_ctx, norm1_g, norm2_g, final_g, w_mod, b_mod, w_in, pool_w,
           pool_scale, conv_w, conv_b, lru_wr, lru_br, lru_wi, lru_bi, lru_lambda, w_br_pool, w_br_lru, w_out,
           router_w, exp_w1, exp_w3, exp_w2):
    cfg = Cfg(d=x_prompt.shape[2], gw=pool_w.shape[2], r=lru_lambda.shape[2], dh=lru_wr.shape[3],
              e=router_w.shape[2], f=exp_w1.shape[3], n_ctx_req=x_prompt.shape[0], n_ctx=x_prompt.shape[1],
              n_lat_req=x_sample.shape[0], n_lat=x_sample.shape[1], grid_w=GRID_W)
    return _forward(cfg, x_prompt, x_sample, state_lru, c, c_ctx, norm1_g, norm2_g, final_g, w_mod, b_mod, w_in,
                    pool_w, pool_scale, conv_w, conv_b, lru_wr, lru_br, lru_wi, lru_bi, lru_lambda,
                    w_br_pool, w_br_lru, w_out, router_w, exp_w1, exp_w3, exp_w2)
```

```python
import functools
from typing import NamedTuple

import jax
import jax.numpy as jnp
from jax import lax
from jax.experimental import pallas as pl
from jax.experimental.pallas import tpu as pltpu

F32 = jnp.float32
BF16 = jnp.bfloat16
I32 = jnp.int32
HIGHEST = lax.Precision.HIGHEST

EPS = 1e-6
LRU_C = 8.0
POOL_HALF_WINDOWS = (1, 2, 4, 8)
N_MOD = 6
CAPACITY_FACTOR = 2
GRID_W = 64
NOT_SELECTED = -(1 << 20)
SUBLANES = 8
BF16_ROWS = 16
LANES = 128
MXU_DIM = 256
VMEM_LIMIT = 56 * 1024 * 1024


class Cfg(NamedTuple):
    d: int
    gw: int
    r: int
    dh: int
    e: int
    f: int
    n_ctx_req: int
    n_ctx: int
    n_lat_req: int
    n_lat: int
    grid_w: int

    @property
    def t(self): return self.n_ctx
    @property
    def tb(self): return 2 * self.n_ctx
    @property
    def nbc(self): return self.n_ctx_req * self.n_ctx // self.tb
    @property
    def p(self): return self.gw * len(POOL_HALF_WINDOWS)
    @property
    def ntc(self): return self.n_ctx_req
    @property
    def tpr(self): return self.n_lat // self.t
    @property
    def nt(self): return self.ntc + self.n_lat_req * self.tpr
    @property
    def n_tok(self): return self.nt * self.t
    @property
    def c_ctx(self): return CAPACITY_FACTOR * self.n_ctx // self.e
    @property
    def c_lat(self): return CAPACITY_FACTOR * self.n_lat // self.e
    @property
    def sblk(self): return self.c_lat
    @property
    def rpb(self): return self.c_lat // self.c_ctx
    @property
    def n_sblk(self): return self.n_ctx_req // self.rpb + self.n_lat_req
    @property
    def w(self): return min(64, self.sblk)
    @property
    def bd(self): return min(MXU_DIM, self.r)
    @property
    def n_tab(self): return 16


def _cparams():
    return pltpu.CompilerParams(dimension_semantics=("arbitrary",), vmem_limit_bytes=VMEM_LIMIT)


def _tab_row(i, cfg):
    return jnp.where(i < cfg.ntc, 0, 1 + (i - cfg.ntc) // cfg.tpr)


def _sigmoid(x):
    return 0.5 * jnp.tanh(0.5 * x) + 0.5


def _lat_pos(i, cfg):
    j = jnp.maximum(i - cfg.ntc, 0)
    return j // cfg.tpr, j % cfg.tpr


def _x_specs(cfg, tile_of):
    def ctx_map(s, *_):
        return (jnp.minimum(tile_of(s), cfg.ntc - 1), 0, 0)

    def lat_map(s, *_):
        b, k = _lat_pos(tile_of(s), cfg)
        return (b, k, 0)

    return [pl.BlockSpec((1, cfg.t, cfg.d), ctx_map), pl.BlockSpec((1, cfg.t, cfg.d), lat_map)]


def _norm_mod(x, g, scale, shift):
    y = x * lax.rsqrt(jnp.mean(x * x, axis=-1, keepdims=True) + EPS)
    return y * (g * (1.0 + scale)) + shift


def _mod_kernel(c_ref, w_ref, b_ref, o_ref):
    c = c_ref[...]
    s = c * _sigmoid(c)
    o_ref[0] = jnp.dot(s, w_ref[0], precision=HIGHEST, preferred_element_type=F32) + b_ref[0]


def _modulation(cvec, w_mod, b_mod, cfg):
    depth, d, n6 = w_mod.shape
    nc = n6 // 4
    return pl.pallas_call(
        _mod_kernel,
        grid=(depth, n6 // nc),
        in_specs=[pl.BlockSpec((cfg.n_tab, d), lambda l, j: (0, 0)),
                  pl.BlockSpec((1, d, nc), lambda l, j: (l, 0, j)),
                  pl.BlockSpec((1, 1, nc), lambda l, j: (l, 0, j))],
        out_specs=pl.BlockSpec((1, cfg.n_tab, nc), lambda l, j: (l, 0, j)),
        out_shape=jax.ShapeDtypeStruct((depth, cfg.n_tab, n6), F32),
        compiler_params=pltpu.CompilerParams(dimension_semantics=("arbitrary", "arbitrary"),
                                             vmem_limit_bytes=VMEM_LIMIT),
        name="modulation",
    )(cvec, w_mod, b_mod.reshape(depth, 1, n6))


def _conv(v, prev, nxt, w, b):
    t = v.shape[0]
    row = lax.broadcasted_iota(I32, v.shape, 0)
    vm1 = jnp.where(row == 0, prev[7:8], pltpu.roll(v, 1, 0))
    vm2 = jnp.where(row == 0, prev[6:7], jnp.where(row == 1, prev[7:8], pltpu.roll(v, 2, 0)))
    vp1 = jnp.where(row == t - 1, nxt[0:1], pltpu.roll(v, t - 1, 0))
    return b + vm2 * w[0:1] + vm1 * w[1:2] + v * w[2:3] + vp1 * w[3:4]


def _to_slab(slab, val, col0):
    nj = slab.shape[0] // val.shape[0]
    for g in range(val.shape[0] // SUBLANES):
        for j in range(val.shape[1] // LANES):
            dst = pl.ds((g * nj + col0 // LANES + j) * SUBLANES, SUBLANES)
            slab[dst, :] = val[g * SUBLANES:(g + 1) * SUBLANES, j * LANES:(j + 1) * LANES]


def _gates(vc, wbd_ref, br, bi, lam, sa, su, cfg):
    bd = cfg.bd
    vcb = vc.astype(BF16)
    nl = -lam
    half_rate = (0.5 * LRU_C) * (jnp.maximum(nl, 0.0) + jnp.log1p(jnp.exp(-jnp.abs(nl))))
    for j in range(cfg.r // bd):
        sl = slice(j * bd, (j + 1) * bd)
        z = jnp.dot(vcb[:, sl], wbd_ref[j], preferred_element_type=F32)
        ig = 0.5 * jnp.tanh(z[:, bd:] + bi[:, sl]) + 0.5
        m = jnp.tanh(z[:, :bd] + br[:, sl]) * half_rate[:, sl] + half_rate[:, sl]
        a = jnp.exp(-m)
        _to_slab(sa, a, j * bd)
        _to_slab(su, jnp.sqrt(jnp.tanh(m) * (a * a + 1.0)) * ig * vc[:, sl], j * bd)


def _scan_rows(out_ref, h_in, reverse, sa, su, sh):
    t, r = out_ref.shape
    nj = r // LANES
    groups = t // SUBLANES
    for j in range(nj):
        sh[j:j + 1, :] = h_in[:, j * LANES:(j + 1) * LANES]
    h0 = sh[0:nj, :]

    def body(k, h):
        g = groups - 1 - k if reverse else k
        base = g * (nj * SUBLANES)
        for row in (range(SUBLANES - 1, -1, -1) if reverse else range(SUBLANES)):
            src = pl.ds(base + row, nj, stride=SUBLANES)
            h = sa[src, :] * h + su[src, :]
            sh[src, :] = h
        return h

    h = lax.fori_loop(0, groups, body, h0, unroll=2)
    for g in range(t // SUBLANES):
        for j in range(nj):
            out_ref[g * SUBLANES:(g + 1) * SUBLANES, j * LANES:(j + 1) * LANES] = \
                sh[pl.ds((g * nj + j) * SUBLANES, SUBLANES), :]
    sh[0:nj, :] = h
    return jnp.concatenate([sh[j:j + 1, :] for j in range(nj)], axis=1)


def _tile_flags(i, cfg):
    is_ctx = i < cfg.ntc
    k = jnp.maximum(i - cfg.ntc, 0) % cfg.tpr
    first = jnp.logical_or(is_ctx, k == 0)
    last = jnp.logical_or(is_ctx, k == cfg.tpr - 1)
    return is_ctx, first, last


def _cast_chunks(w, layer, n_chunks, chunk_of):
    depth, e, rows, cols = w.shape
    assert (e * rows) % (n_chunks * BF16_ROWS) == 0
    cr = e * rows // n_chunks
    return (w.reshape(depth, n_chunks, cr, cols),
            pl.BlockSpec((1, 1, cr, cols), lambda s: (layer, chunk_of(s), 0, 0)),
            pl.BlockSpec((1, cr, cols), lambda s: (chunk_of(s), 0, 0)),
            jax.ShapeDtypeStruct((n_chunks, cr, cols), BF16))


def _scan_bwd_kernel(xc_ref, xl_ref, xprev_ref, xnext_ref, mod_ref, g1_ref, wv_ref, cw_ref, cb_ref, wbd_ref,
                     br_ref, bi_ref, lam_ref, h0_ref, ew_ref, vc_ref, hb_ref, hlast_ref, ewb_ref,
                     carry_scr, sa, su, sh, *, cfg):
    i = cfg.nt - 1 - pl.program_id(0)
    t, d = cfg.t, cfg.d
    is_ctx, first, last = _tile_flags(i, cfg)
    ewb_ref[0] = ew_ref[0, 0].astype(BF16)
    x_ext = jnp.concatenate([xprev_ref[0], jnp.where(is_ctx, xc_ref[0], xl_ref[0]), xnext_ref[0]], axis=0)
    m = mod_ref[0]
    hn = _norm_mod(x_ext, g1_ref[...], m[:, d:2 * d], m[:, 0:d]).astype(BF16)
    v_ext = jnp.dot(hn, wv_ref[...], preferred_element_type=F32)
    prev = jnp.where(first, 0.0, v_ext[0:SUBLANES])
    nxt = jnp.where(last, 0.0, v_ext[t + SUBLANES:t + 2 * SUBLANES])
    vc = _conv(v_ext[SUBLANES:t + SUBLANES], prev, nxt, cw_ref[...], cb_ref[...])
    vc_ref[...] = vc
    _gates(vc, wbd_ref, br_ref[...], bi_ref[...], lam_ref[...], sa, su, cfg)

    @pl.when(last)
    def _():
        carry_scr[...] = h0_ref[0]

    h = _scan_rows(hb_ref, carry_scr[...], True, sa, su, sh)
    carry_scr[...] = h

    @pl.when(is_ctx)
    def _():
        hlast_ref[0] = h


def _scan_bwd(xc, xl, mod, g1, wv, cw, cb, wbd, br, bi, lam, h0, ew, layer, cfg):
    t, d, r, nt = cfg.t, cfg.d, cfg.r, cfg.nt
    n = cfg.n_tok
    tb = t // SUBLANES
    rev = lambda s: nt - 1 - s
    full = lambda a: pl.BlockSpec(a.shape, lambda s: (0,) * a.ndim)
    tile = pl.BlockSpec((t, r), lambda s: (rev(s), 0))
    tab = lambda width: pl.BlockSpec((1, 1, width), lambda s: (_tab_row(rev(s), cfg), 0, 0))

    def prev_map(s):
        b, k = _lat_pos(rev(s), cfg)
        return (b, jnp.maximum(k * tb - 1, 0), 0)

    def next_map(s):
        b, k = _lat_pos(rev(s), cfg)
        return (b, jnp.minimum((k + 1) * tb, cfg.n_lat // SUBLANES - 1), 0)

    ewc, ew_in, ew_out, ew_shape = _cast_chunks(ew, layer, nt - cfg.ntc,
                                                lambda s: jnp.maximum(rev(s) - cfg.ntc, 0))
    vc, hb, hb_last, ewb = pl.pallas_call(
        functools.partial(_scan_bwd_kernel, cfg=cfg),
        grid=(nt,),
        in_specs=_x_specs(cfg, rev) + [
            pl.BlockSpec((1, SUBLANES, d), prev_map), pl.BlockSpec((1, SUBLANES, d), next_map),
            tab(N_MOD * d), full(g1), full(wv), full(cw), full(cb), full(wbd), full(br), full(bi), full(lam),
            tab(r), ew_in],
        out_specs=[tile, tile,
                   pl.BlockSpec((1, 1, r), lambda s: (jnp.minimum(rev(s), cfg.ntc - 1), 0, 0)), ew_out],
        out_shape=[jax.ShapeDtypeStruct((n, r), F32),
                   jax.ShapeDtypeStruct((n, r), F32),
                   jax.ShapeDtypeStruct((cfg.n_ctx_req, 1, r), F32), ew_shape],
        scratch_shapes=[pltpu.VMEM((1, r), F32)] + [pltpu.VMEM((t * r // LANES, LANES), F32)] * 3,
        compiler_params=_cparams(),
        name="scan_bwd",
    )(xc, xl, xl, xl, mod, g1, wv, cw, cb, wbd, br, bi, lam, h0, ewc)
    return vc, hb, hb_last, ewb.reshape(ew.shape[1:])


def _pool(pv, is_ctx, cfg):
    t = pv.shape[0]
    gw = cfg.gw
    row = lax.broadcasted_iota(I32, (t, gw), 0)
    pos = jnp.where(is_ctx, row % cfg.n_ctx, row % cfg.grid_w)
    length = jnp.where(is_ctx, cfg.n_ctx, cfg.grid_w)

    def shifted(x, dlt):
        ok = jnp.logical_and(pos + dlt >= 0, pos + dlt < length)
        return jnp.where(ok, pltpu.roll(x, (-dlt) % t, 0), 0.0)

    outs = []
    for g, m in enumerate(POOL_HALF_WINDOWS):
        x = pv[:, g * gw:(g + 1) * gw]
        back, fwd, k = x, x, 1
        while k < m:
            back = back + shifted(back, -k)
            fwd = fwd + shifted(fwd, k)
            k *= 2
        s = shifted(back, -1) + fwd
        cnt = (jnp.minimum(pos + m, length) - jnp.maximum(pos - m, 0)).astype(F32)
        outs.append(s / cnt - x)
    return outs


def _scan_fwd_kernel(vc_ref, h0_ref, wbd_ref, br_ref, bi_ref, lam_ref, ew_ref,
                     hf_ref, hlast_ref, ewb_ref, carry_scr, sa, su, sh, *, cfg):
    i = pl.program_id(0)
    is_ctx, first, _ = _tile_flags(i, cfg)
    ewb_ref[0] = ew_ref[0, 0].astype(BF16)
    _gates(vc_ref[...], wbd_ref, br_ref[...], bi_ref[...], lam_ref[...], sa, su, cfg)

    @pl.when(first)
    def _():
        carry_scr[...] = h0_ref[0]

    h = _scan_rows(hf_ref, carry_scr[...], False, sa, su, sh)
    carry_scr[...] = h

    @pl.when(is_ctx)
    def _():
        hlast_ref[0] = h


def _scan_fwd(vc, h0, wbd, br, bi, lam, ew, layer, cfg):
    t, r = cfg.t, cfg.r
    tile = pl.BlockSpec((t, r), lambda i: (i, 0))
    full = lambda a: pl.BlockSpec(a.shape, lambda i: (0,) * a.ndim)
    ewc, ew_in, ew_out, ew_shape = _cast_chunks(ew, layer, cfg.nt - cfg.ntc,
                                                lambda i: jnp.maximum(i - cfg.ntc, 0))
    hf, hf_last, ewb = pl.pallas_call(
        functools.partial(_scan_fwd_kernel, cfg=cfg),
        grid=(cfg.nt,),
        in_specs=[tile, pl.BlockSpec((1, 1, r), lambda i: (_tab_row(i, cfg), 0, 0)),
                  full(wbd), full(br), full(bi), full(lam), ew_in],
        out_specs=[tile, pl.BlockSpec((1, 1, r), lambda i: (jnp.minimum(i, cfg.ntc - 1), 0, 0)), ew_out],
        out_shape=[jax.ShapeDtypeStruct((cfg.n_tok, r), F32),
                   jax.ShapeDtypeStruct((cfg.n_ctx_req, 1, r), F32), ew_shape],
        scratch_shapes=[pltpu.VMEM((1, r), F32)] + [pltpu.VMEM((t * r // LANES, LANES), F32)] * 3,
        compiler_params=_cparams(),
        name="scan_fwd",
    )(vc, h0, wbd, br, bi, lam, ewc)
    return hf, hf_last, ewb.reshape(ew.shape[1:])


def _mix_kernel(xc_ref, xl_ref, hf_ref, hb_ref, mod_ref, g1_ref, wp_ref, wg_ref, pw_ref, ps_ref, wbp_ref,
                wbl_ref, wo_ref, g2_ref, rwa_ref, rwb_ref, ew_ref, x1_ref, hn2_ref, lg_ref, ewb_ref, *, cfg):
    i = pl.program_id(0)
    ewb_ref[0] = ew_ref[0, 0].astype(BF16)
    d, gw = cfg.d, cfg.gw
    is_ctx = i < cfg.nbc
    m = mod_ref[0]
    ps = ps_ref[...]
    dot = functools.partial(jnp.dot, preferred_element_type=F32)
    halves = [slice(h * cfg.n_ctx, (h + 1) * cfg.n_ctx) for h in range(cfg.tb // cfg.n_ctx)]
    st = [dict() for _ in halves]

    def head(s, rs):
        s['x'] = jnp.where(is_ctx, xc_ref[rs, :], xl_ref[rs, :])
        s['hn'] = _norm_mod(s['x'], g1_ref[...], m[:, d:2 * d], m[:, 0:d]).astype(BF16)

    def project(s, rs):
        s['p'] = dot(s['hn'], wp_ref[...])
        s['gl'] = dot(s['hn'], wg_ref[...])

    def pool(s, rs):
        s['y_lru'] = (hf_ref[rs, :] + hb_ref[rs, :]).astype(BF16)
        s['pooled'] = [q.astype(BF16) for q in _pool(s.pop('p'), is_ctx, cfg)]

    def branches(s, rs):
        y_pool = jnp.concatenate([dot(q, pw_ref[g]) * ps[:, g * gw:(g + 1) * gw]
                                  for g, q in enumerate(s.pop('pooled'))], axis=1).astype(BF16)
        s['bp'] = dot(y_pool, wbp_ref[...])
        s['bl'] = dot(s.pop('y_lru'), wbl_ref[...])

    def merge(s, rs):
        gl = s.pop('gl')
        s['merged'] = (_sigmoid(gl[:, :d]) * s.pop('bp') + _sigmoid(gl[:, d:]) * s.pop('bl')).astype(BF16)

    def out_proj(s, rs):
        s['mix'] = dot(s.pop('merged'), wo_ref[...])

    def tail(s, rs):
        x1 = s.pop('x') + m[:, 2 * d:3 * d] * s.pop('mix')
        x1_ref[rs, :] = x1
        hn2 = _norm_mod(x1, g2_ref[...], m[:, 4 * d:5 * d], m[:, 3 * d:4 * d])
        hi = hn2.astype(BF16)
        hn2_ref[rs, :] = hi
        lo = (hn2 - hi.astype(F32)).astype(BF16)
        nt = functools.partial(lax.dot_general, dimension_numbers=(((1,), (1,)), ((), ())),
                               preferred_element_type=F32)
        lg_ref[:, rs] = nt(rwa_ref[...], hi) + nt(rwb_ref[...], hi) + nt(rwa_ref[...], lo)

    for stage in (head, project, pool, branches, merge, out_proj, tail):
        for s, rs in zip(st, halves):
            stage(s, rs)


def _mix(xc, xl, hf, hb, mod, g1, wp, wg, pw, ps, wbp, wbl, wo, g2, rwa, rwb, ew, layer, cfg):
    tb, d, r = cfg.tb, cfg.d, cfg.r
    n = cfg.n_tok
    nbc = cfg.nbc
    tile = lambda width: pl.BlockSpec((tb, width), lambda i: (i, 0))
    full = lambda a: pl.BlockSpec(a.shape, lambda i: (0,) * a.ndim)
    tab_row = lambda i: jnp.where(i < nbc, 0, 1 + (i - nbc) // (cfg.n_lat // tb))
    ewc, ew_in, ew_out, ew_shape = _cast_chunks(ew, layer, n // tb - nbc, lambda i: jnp.maximum(i - nbc, 0))
    x1, hn2, logits, ewb = pl.pallas_call(
        functools.partial(_mix_kernel, cfg=cfg),
        grid=(n // tb,),
        in_specs=[pl.BlockSpec((tb, d), lambda i: (jnp.minimum(i, nbc - 1), 0)),
                  pl.BlockSpec((tb, d), lambda i: (jnp.maximum(i - nbc, 0), 0)),
                  tile(r), tile(r), pl.BlockSpec((1, 1, N_MOD * d), lambda i: (tab_row(i), 0, 0)),
                  full(g1), full(wp), full(wg), full(pw), full(ps), full(wbp), full(wbl), full(wo),
                  full(g2), full(rwa), full(rwb), ew_in],
        out_specs=[tile(d), tile(d), pl.BlockSpec((cfg.e, tb), lambda i: (0, i)), ew_out],
        out_shape=[jax.ShapeDtypeStruct((n, d), F32),
                   jax.ShapeDtypeStruct((n, d), BF16),
                   jax.ShapeDtypeStruct((cfg.e, n), F32), ew_shape],
        compiler_params=_cparams(),
        name="mix",
    )(xc.reshape(-1, d), xl.reshape(-1, d), hf, hb, mod, g1, wp, wg, pw, ps, wbp, wbl, wo, g2, rwa, rwb, ewc)
    return x1, hn2, logits, ewb.reshape(ew.shape[1:])


def _route_kernel(l_ref, selpos_ref, gate_ref, offs_ref, cnts_ref, *, cap, t):
    lg = l_ref[...]
    nreq, e, n = lg.shape
    rows = nreq * e
    nch = n // t
    ex = jnp.exp(lg - jnp.max(lg, axis=1, keepdims=True))
    aff = (ex / jnp.sum(ex, axis=1, keepdims=True)).reshape(rows, n)
    gate_ref[...] = aff
    keys = pltpu.bitcast(aff, I32)

    def count(mask):
        return jnp.sum(mask.astype(F32), axis=1, keepdims=True)

    thr = jnp.zeros((rows, 1), I32)
    for bit in range(30, -1, -1):
        cand = thr | (1 << bit)
        thr = jnp.where(count(keys >= cand) >= cap, cand, thr)
    gt = keys > thr
    eq = keys == thr
    need = cap - count(gt)

    tri = (lax.broadcasted_iota(I32, (t, t), 0) < lax.broadcasted_iota(I32, (t, t), 1)).astype(BF16)
    tok = lax.broadcasted_iota(I32, (n, LANES), 0)
    chunk = lax.broadcasted_iota(I32, (n, LANES), 1)
    before = (tok < chunk * t).astype(BF16)
    inside = (tok // t == chunk).astype(BF16)

    def prefix(mask):
        mb = mask.astype(BF16)
        offs = jnp.dot(mb, before, preferred_element_type=F32)
        pre = [jnp.dot(mb[:, k * t:(k + 1) * t], tri, preferred_element_type=F32) + offs[:, k:k + 1]
               for k in range(nch)]
        return jnp.concatenate(pre, axis=1) if nch > 1 else pre[0], offs, mb

    eq_rank, _, _ = prefix(eq)
    sel = jnp.logical_or(gt, jnp.logical_and(eq, eq_rank < need))
    pos, offs, selb = prefix(sel)
    selpos_ref[...] = jnp.where(sel, pos.astype(I32), NOT_SELECTED)
    offs_ref[...] = offs.astype(I32)
    cnts_ref[...] = jnp.dot(selb, inside, preferred_element_type=F32).astype(I32)


def _route(logits, cap, cfg):
    nreq, e, n = logits.shape
    rows = nreq * e
    whole = lambda width: pl.BlockSpec((rows, width), lambda b: (0, 0))
    return pl.pallas_call(
        functools.partial(_route_kernel, cap=cap, t=cfg.t),
        grid=(1,),
        in_specs=[pl.BlockSpec((nreq, e, n), lambda b: (0, 0, 0))],
        out_specs=[whole(n), whole(n), whole(LANES), whole(LANES)],
        out_shape=[jax.ShapeDtypeStruct((rows, n), I32),
                   jax.ShapeDtypeStruct((rows, n), F32),
                   jax.ShapeDtypeStruct((rows, LANES), I32),
                   jax.ShapeDtypeStruct((rows, LANES), I32)],
        compiler_params=_cparams(),
        name="route",
    )(logits)


def _slot_block(i, cfg):
    return jnp.where(i < cfg.ntc, i // cfg.rpb, cfg.ntc // cfg.rpb + (i - cfg.ntc) // cfg.tpr)


def _windows(i, toff_ref, tcnt_ref, cfg):
    is_ctx = i < cfg.ntc
    base = jnp.where(is_ctx, (i % cfg.rpb) * cfg.c_ctx, 0)
    starts, npass = [], 0
    for e in range(cfg.e):
        off = base + toff_ref[i * cfg.e + e]
        start = (off // BF16_ROWS) * BF16_ROWS
        starts.append(start)
        npass = jnp.maximum(npass, (off + tcnt_ref[i * cfg.e + e] - start + cfg.w - 1) // cfg.w)
    return base, starts, npass


def _window(start, j, cfg):
    want = start + j * cfg.w
    a = pl.multiple_of(jnp.minimum(want, cfg.sblk - cfg.w), BF16_ROWS)
    return a, want - a


def _dispatch_kernel(toff_ref, tcnt_ref, x_ref, sp_ref, g_ref, xs_ref, gs_ref, p_scr, *, cfg):
    i = pl.program_id(0)
    w = cfg.w
    is_ctx = i < cfg.ntc
    new_block = jnp.where(is_ctx, i % cfg.rpb == 0, jnp.maximum(i - cfg.ntc, 0) % cfg.tpr == 0)

    @pl.when(new_block)
    def _():
        xs_ref[...] = jnp.zeros(xs_ref.shape, xs_ref.dtype)
        gs_ref[...] = jnp.zeros(gs_ref.shape, gs_ref.dtype)

    base, starts, npass = _windows(i, toff_ref, tcnt_ref, cfg)
    slot = lax.broadcasted_iota(I32, (w, cfg.t), 0)

    def one_pass(j, carry):
        firsts = []
        for e in range(cfg.e):
            a, owned = _window(starts[e], j, cfg)
            firsts.append(a)
            rel = sp_ref[e:e + 1, :] - (a - base)
            hit = jnp.logical_and(rel == slot, slot >= owned)
            p_scr[e * w:(e + 1) * w, :] = hit.astype(BF16)
            gs_ref[e, pl.ds(a, w), :] += jnp.sum(jnp.where(hit, g_ref[e:e + 1, :], 0.0), axis=1, keepdims=True)
        rows = jnp.dot(p_scr[...], x_ref[...], preferred_element_type=F32)
        for e in range(cfg.e):
            xs_ref[e, pl.ds(firsts[e], w), :] += rows[e * w:(e + 1) * w].astype(BF16)
        return carry

    lax.fori_loop(0, npass, one_pass, 0)


def _dispatch(toff, tcnt, hn2, selpos, gate, cfg):
    t, d, e = cfg.t, cfg.d, cfg.e
    s = cfg.n_sblk * cfg.sblk
    slots = lambda width: pl.BlockSpec((e, cfg.sblk, width), lambda i, *_: (0, _slot_block(i, cfg), 0))
    return pl.pallas_call(
        functools.partial(_dispatch_kernel, cfg=cfg),
        grid_spec=pltpu.PrefetchScalarGridSpec(
            num_scalar_prefetch=2,
            grid=(cfg.nt,),
            in_specs=[pl.BlockSpec((t, d), lambda i, *_: (i, 0)),
                      pl.BlockSpec((e, t), lambda i, *_: (0, i)),
                      pl.BlockSpec((e, t), lambda i, *_: (0, i))],
            out_specs=[slots(d), slots(1)],
            scratch_shapes=[pltpu.VMEM((e * cfg.w, t), BF16)]),
        out_shape=[jax.ShapeDtypeStruct((e, s, d), BF16), jax.ShapeDtypeStruct((e, s, 1), F32)],
        compiler_params=_cparams(),
        name="dispatch",
    )(toff, tcnt, hn2, selpos, gate)


def _ffn_kernel(x_ref, gs_ref, w1_ref, w3_ref, w2_ref, o_ref, *, fc):
    x = x_ref[0]
    acc = jnp.zeros((x.shape[0], o_ref.shape[2]), F32)
    for c in range(w1_ref.shape[2] // fc):
        sl = slice(c * fc, (c + 1) * fc)
        h1 = jnp.dot(x, w1_ref[0, :, sl], preferred_element_type=F32)
        h3 = jnp.dot(x, w3_ref[0, :, sl], preferred_element_type=F32)
        hid = (h1 * _sigmoid(h1)) * h3
        acc = acc + jnp.dot(hid.astype(BF16), w2_ref[0, sl, :], preferred_element_type=F32)
    o_ref[0] = (acc * gs_ref[0]).astype(BF16)


def _ffn(xs, gs, w1, w3, w2, cfg):
    e, s, d = xs.shape
    f = w1.shape[2]
    tm = 2 * cfg.sblk if cfg.n_sblk % 2 == 0 else cfg.sblk
    return pl.pallas_call(
        functools.partial(_ffn_kernel, fc=min(512, f)),
        grid=(e, s // tm),
        in_specs=[pl.BlockSpec((1, tm, d), lambda k, m: (k, m, 0)),
                  pl.BlockSpec((1, tm, 1), lambda k, m: (k, m, 0)),
                  pl.BlockSpec((1, d, f), lambda k, m: (k, 0, 0)),
                  pl.BlockSpec((1, d, f), lambda k, m: (k, 0, 0)),
                  pl.BlockSpec((1, f, d), lambda k, m: (k, 0, 0))],
        out_specs=pl.BlockSpec((1, tm, d), lambda k, m: (k, m, 0)),
        out_shape=jax.ShapeDtypeStruct((e, s, d), BF16),
        compiler_params=pltpu.CompilerParams(dimension_semantics=("arbitrary", "arbitrary"),
                                             vmem_limit_bytes=VMEM_LIMIT),
        name="ffn",
    )(xs, gs, w1, w3, w2)


def _combine_kernel(toff_ref, tcnt_ref, x1_ref, mod_ref, spt_ref, o_ref, fg_ref, oc_ref, ol_ref, ow_scr,
                    *, cfg, final):
    i = pl.program_id(0)
    w, d = cfg.w, cfg.d
    base, starts, npass = _windows(i, toff_ref, tcnt_ref, cfg)
    ne = cfg.e
    expert = lax.broadcasted_iota(I32, (1, ne), 1)
    expand = (lax.broadcasted_iota(I32, (ne, ne * w), 1) // w
              == lax.broadcasted_iota(I32, (ne, ne * w), 0)).astype(BF16)
    row_in_window = (lax.broadcasted_iota(I32, (cfg.t, ne * w), 1) % w).astype(F32)
    spt = spt_ref[...]

    def one_pass(j):
        shift = jnp.zeros((1, ne), I32)
        owned = jnp.zeros((1, ne), I32)
        for e in range(ne):
            a, own = _window(starts[e], j, cfg)
            ow_scr[e * w:(e + 1) * w, :] = o_ref[e, pl.ds(a, w), :]
            shift = jnp.where(expert == e, a - base, shift)
            owned = jnp.where(expert == e, own, owned)
        rel = spt - shift
        rel = jnp.where(jnp.logical_and(rel >= owned, rel < w), rel, -1)
        wide = jnp.dot(rel.astype(F32).astype(BF16), expand, preferred_element_type=F32)
        hits = (wide == row_in_window).astype(BF16)
        return jnp.dot(hits, ow_scr[...], preferred_element_type=F32)

    is_ctx = i < cfg.ntc

    def finish(y):
        x2 = x1_ref[...] + mod_ref[0][:, 5 * d:6 * d] * y
        if final:
            x2 = (x2 * lax.rsqrt(jnp.mean(x2 * x2, axis=-1, keepdims=True) + EPS)) * fg_ref[...]

        @pl.when(is_ctx)
        def _():
            oc_ref[0] = x2

        @pl.when(jnp.logical_not(is_ctx))
        def _():
            ol_ref[0] = x2

    @pl.when(npass <= 1)
    def _():
        finish(one_pass(0))

    @pl.when(npass > 1)
    def _():
        finish(lax.fori_loop(1, npass, lambda j, y: y + one_pass(j), one_pass(0)))


def _combine(toff, tcnt, x1, mod, selpos_t, o, fg, cfg, final):
    t, d, e = cfg.t, cfg.d, cfg.e
    return pl.pallas_call(
        functools.partial(_combine_kernel, cfg=cfg, final=final),
        grid_spec=pltpu.PrefetchScalarGridSpec(
            num_scalar_prefetch=2,
            grid=(cfg.nt,),
            in_specs=[pl.BlockSpec((t, d), lambda i, *_: (i, 0)),
                      pl.BlockSpec((1, 1, N_MOD * d), lambda i, *_: (_tab_row(i, cfg), 0, 0)),
                      pl.BlockSpec((t, e), lambda i, *_: (i, 0)),
                      pl.BlockSpec((e, cfg.sblk, d), lambda i, *_: (0, _slot_block(i, cfg), 0)),
                      pl.BlockSpec((1, d), lambda i, *_: (0, 0))],
            out_specs=_x_specs(cfg, lambda i: i),
            scratch_shapes=[pltpu.VMEM((e * cfg.w, d), BF16)]),
        out_shape=[jax.ShapeDtypeStruct((cfg.n_ctx_req, cfg.n_ctx, d), F32),
                   jax.ShapeDtypeStruct((cfg.n_lat_req, cfg.n_lat, d), F32)],
        compiler_params=_cparams(),
        name="combine",
    )(toff, tcnt, x1, mod, selpos_t, o, fg)


def _block_diag_gates(wr, wi, cfg):
    hpb = cfg.bd // cfg.dh
    nblk = cfg.r // cfg.bd
    eye = jnp.eye(hpb, dtype=wr.dtype)

    def blocks(w):
        w4 = w.reshape(nblk, hpb, cfg.dh, cfg.dh)
        return jnp.einsum('jhde,hk->jhdke', w4, eye).reshape(nblk, cfg.bd, cfg.bd)

    return (0.5 * jnp.concatenate([blocks(wr), blocks(wi)], axis=2)).astype(BF16)


def _tile_tables(offs_ctx, offs_lat, cfg):
    ctx = offs_ctx[:, 0].reshape(-1, cfg.e)
    lat = jnp.swapaxes(offs_lat[:, :cfg.tpr].reshape(-1, cfg.e, cfg.tpr), 1, 2).reshape(-1, cfg.e)
    return jnp.concatenate([ctx, lat], axis=0).reshape(-1)


def _by_expert(rows_ctx, rows_lat, cfg):
    def flip(a):
        return jnp.swapaxes(a.reshape(-1, cfg.e, a.shape[1]), 0, 1).reshape(cfg.e, -1)
    return jnp.concatenate([flip(rows_ctx), flip(rows_lat)], axis=1)


def _forward(cfg, x_prompt, x_sample, state_lru, c, c_ctx, norm1_g, norm2_g, final_g, w_mod, b_mod, w_in,
             pool_w, pool_scale, conv_w, conv_b, lru_wr, lru_br, lru_wi, lru_bi, lru_lambda,
             w_br_pool, w_br_lru, w_out, router_w, exp_w1, exp_w3, exp_w2):
    d, r, p = cfg.d, cfg.r, cfg.p
    depth = w_in.shape[0]
    assert cfg.n_lat % cfg.t == 0 and cfg.t % cfg.grid_w == 0 and cfg.n_ctx_req % cfg.rpb == 0
    assert cfg.w % BF16_ROWS == 0
    assert (cfg.n_ctx_req * cfg.n_ctx) % cfg.tb == 0 and cfg.n_lat % cfg.tb == 0 and cfg.tb % cfg.grid_w == 0

    n_ctx_tok = cfg.n_ctx_req * cfg.n_ctx
    xc, xl = x_prompt, x_sample

    cvec = jnp.zeros((cfg.n_tab, d), F32).at[0].set(c_ctx).at[1:1 + cfg.n_lat_req].set(c)
    mod = _modulation(cvec, w_mod, b_mod, cfg)

    states = []
    for l in range(depth):
        mod_l = mod[l].reshape(cfg.n_tab, 1, N_MOD * d)
        row = lambda a: a.reshape(1, -1)
        wp = w_in[l][:, :p].astype(BF16)
        wv = w_in[l][:, p:p + r].astype(BF16)
        wg = w_in[l][:, p + r:].astype(BF16)
        rwa = router_w[l].T.astype(BF16)
        rwb = (router_w[l].T - rwa.astype(F32)).astype(BF16)

        def h0(direction):
            tab = jnp.zeros((cfg.n_tab, 1, r), F32)
            return tab.at[1:1 + cfg.n_lat_req, 0].set(state_lru[:, l, direction].astype(F32))

        wbd = [_block_diag_gates(lru_wr[l, z], lru_wi[l, z], cfg) for z in range(2)]
        vc, hb, hb_last, w1b = _scan_bwd(xc, xl, mod_l, row(norm1_g[l]), wv, conv_w[l], row(conv_b[l]), wbd[1],
                                         row(0.5 * lru_br[l, 1]), row(0.5 * lru_bi[l, 1]),
                                         row(lru_lambda[l, 1]), h0(1), exp_w1, l, cfg)
        hf, hf_last, w3b = _scan_fwd(vc, h0(0), wbd[0], row(0.5 * lru_br[l, 0]), row(0.5 * lru_bi[l, 0]),
                                     row(lru_lambda[l, 0]), exp_w3, l, cfg)
        x1, hn2, logits, w2b = _mix(xc, xl, hf, hb, mod_l, row(norm1_g[l]), wp, wg, pool_w[l].astype(BF16),
                                    row(pool_scale[l]), w_br_pool[l].astype(BF16), w_br_lru[l].astype(BF16),
                                    w_out[l].astype(BF16), row(norm2_g[l]), rwa, rwb, exp_w2, l, cfg)
        states.append(jnp.stack([hf_last[:, 0], hb_last[:, 0]], axis=1))

        by_request = lambda a, n: jnp.swapaxes(a.reshape(cfg.e, -1, n), 0, 1)
        sp_c, g_c, off_c, cnt_c = _route(by_request(logits[:, :n_ctx_tok], cfg.n_ctx), cfg.c_ctx, cfg)
        sp_l, g_l, off_l, cnt_l = _route(by_request(logits[:, n_ctx_tok:], cfg.n_lat), cfg.c_lat, cfg)
        selpos = _by_expert(sp_c, sp_l, cfg)
        toff = _tile_tables(off_c, off_l, cfg)
        tcnt = _tile_tables(cnt_c, cnt_l, cfg)

        xs, gs = _dispatch(toff, tcnt, hn2, selpos, _by_expert(g_c, g_l, cfg), cfg)
        o = _ffn(xs, gs, w1b, w3b, w2b, cfg)
        xc, xl = _combine(toff, tcnt, x1, mod_l, selpos.T, o, row(final_g), cfg, l == depth - 1)

    new_state = jnp.stack(states, axis=1).astype(x_prompt.dtype)
    return xc, xl, new_state


def kernel(x_prompt, x_sample, state_lru, c, c_ctx, norm1_g, norm2_g, final_g, w_mod, b_mod, w_in, pool_w,
           pool_scale, conv_w, conv_b, lru_wr, lru_br, lru_wi, lru_bi, lru_lambda, w_br_pool, w_br_lru, w_out,
           router_w, exp_w1, exp_w3, exp_w2):
    cfg = Cfg(d=x_prompt.shape[2], gw=pool_w.shape[2], r=lru_lambda.shape[2], dh=lru_wr.shape[3],
              e=router_w.shape[2], f=exp_w1.shape[3], n_ctx_req=x_prompt.shape[0], n_ctx=x_prompt.shape[1],
              n_lat_req=x_sample.shape[0], n_lat=x_sample.shape[1], grid_w=GRID_W)
    return _forward(cfg, x_prompt, x_sample, state_lru, c, c_ctx, norm1_g, norm2_g, final_g, w_mod, b_mod, w_in,
                    pool_w, pool_scale, conv_w, conv_b, lru_wr, lru_br, lru_wi, lru_bi, lru_lambda,
                    w_br_pool, w_br_lru, w_out, router_w, exp_w1, exp_w3, exp_w2)
```

```python
import functools
from typing import NamedTuple

import jax
import jax.numpy as jnp
from jax import lax
from jax.experimental import pallas as pl
from jax.experimental.pallas import tpu as pltpu

F32 = jnp.float32
BF16 = jnp.bfloat16
I32 = jnp.int32
HIGHEST = lax.Precision.HIGHEST

EPS = 1e-6
LRU_C = 8.0
POOL_HALF_WINDOWS = (1, 2, 4, 8)
N_MOD = 6
CAPACITY_FACTOR = 2
GRID_W = 64
NOT_SELECTED = -(1 << 20)
SUBLANES = 8
BF16_ROWS = 16
LANES = 128
MXU_DIM = 256
VMEM_LIMIT = 56 * 1024 * 1024


class Cfg(NamedTuple):
    d: int
    gw: int
    r: int
    dh: int
    e: int
    f: int
    n_ctx_req: int
    n_ctx: int
    n_lat_req: int
    n_lat: int
    grid_w: int

    @property
    def t(self): return self.n_ctx
    @property
    def tb(self): return 2 * self.n_ctx
    @property
    def nbc(self): return self.n_ctx_req * self.n_ctx // self.tb
    @property
    def p(self): return self.gw * len(POOL_HALF_WINDOWS)
    @property
    def ntc(self): return self.n_ctx_req
    @property
    def tpr(self): return self.n_lat // self.t
    @property
    def nt(self): return self.ntc + self.n_lat_req * self.tpr
    @property
    def n_tok(self): return self.nt * self.t
    @property
    def c_ctx(self): return CAPACITY_FACTOR * self.n_ctx // self.e
    @property
    def c_lat(self): return CAPACITY_FACTOR * self.n_lat // self.e
    @property
    def sblk(self): return self.c_lat
    @property
    def rpb(self): return self.c_lat // self.c_ctx
    @property
    def n_sblk(self): return self.n_ctx_req // self.rpb + self.n_lat_req
    @property
    def w(self): return min(64, self.sblk)
    @property
    def bd(self): return min(MXU_DIM, self.r)
    @property
    def n_tab(self): return 16


def _cparams():
    return pltpu.CompilerParams(dimension_semantics=("arbitrary",), vmem_limit_bytes=VMEM_LIMIT)


def _tab_row(i, cfg):
    return jnp.where(i < cfg.ntc, 0, 1 + (i - cfg.ntc) // cfg.tpr)


def _sigmoid(x):
    return 0.5 * jnp.tanh(0.5 * x) + 0.5


def _lat_pos(i, cfg):
    j = jnp.maximum(i - cfg.ntc, 0)
    return j // cfg.tpr, j % cfg.tpr


def _x_specs(cfg, tile_of):
    def ctx_map(s, *_):
        return (jnp.minimum(tile_of(s), cfg.ntc - 1), 0, 0)

    def lat_map(s, *_):
        b, k = _lat_pos(tile_of(s), cfg)
        return (b, k, 0)

    return [pl.BlockSpec((1, cfg.t, cfg.d), ctx_map), pl.BlockSpec((1, cfg.t, cfg.d), lat_map)]


def _norm_mod(x, g, scale, shift):
    y = x * lax.rsqrt(jnp.mean(x * x, axis=-1, keepdims=True) + EPS)
    return y * (g * (1.0 + scale)) + shift


def _mod_kernel(c_ref, w_ref, b_ref, o_ref):
    c = c_ref[...]
    s = c * _sigmoid(c)
    o_ref[0] = jnp.dot(s, w_ref[0], precision=HIGHEST, preferred_element_type=F32) + b_ref[0]


def _modulation(cvec, w_mod, b_mod, cfg):
    depth, d, n6 = w_mod.shape
    nc = n6 // 4
    return pl.pallas_call(
        _mod_kernel,
        grid=(depth, n6 // nc),
        in_specs=[pl.BlockSpec((cfg.n_tab, d), lambda l, j: (0, 0)),
                  pl.BlockSpec((1, d, nc), lambda l, j: (l, 0, j)),
                  pl.BlockSpec((1, 1, nc), lambda l, j: (l, 0, j))],
        out_specs=pl.BlockSpec((1, cfg.n_tab, nc), lambda l, j: (l, 0, j)),
        out_shape=jax.ShapeDtypeStruct((depth, cfg.n_tab, n6), F32),
        compiler_params=pltpu.CompilerParams(dimension_semantics=("arbitrary", "arbitrary"),
                                             vmem_limit_bytes=VMEM_LIMIT),
        name="modulation",
    )(cvec, w_mod, b_mod.reshape(depth, 1, n6))


def _conv(v, prev, nxt, w, b):
    t = v.shape[0]
    row = lax.broadcasted_iota(I32, v.shape, 0)
    vm1 = jnp.where(row == 0, prev[7:8], pltpu.roll(v, 1, 0))
    vm2 = jnp.where(row == 0, prev[6:7], jnp.where(row == 1, prev[7:8], pltpu.roll(v, 2, 0)))
    vp1 = jnp.where(row == t - 1, nxt[0:1], pltpu.roll(v, t - 1, 0))
    return b + vm2 * w[0:1] + vm1 * w[1:2] + v * w[2:3] + vp1 * w[3:4]


def _to_slab(slab, val, col0):
    nj = slab.shape[0] // val.shape[0]
    for g in range(val.shape[0] // SUBLANES):
        for j in range(val.shape[1] // LANES):
            dst = pl.ds((g * nj + col0 // LANES + j) * SUBLANES, SUBLANES)
            slab[dst, :] = val[g * SUBLANES:(g + 1) * SUBLANES, j * LANES:(j + 1) * LANES]


def _gates(vc, wbd_ref, br, bi, lam, sa, su, cfg):
    bd = cfg.bd
    vcb = vc.astype(BF16)
    nl = -lam
    half_rate = (0.5 * LRU_C) * (jnp.maximum(nl, 0.0) + jnp.log1p(jnp.exp(-jnp.abs(nl))))
    for j in range(cfg.r // bd):
        sl = slice(j * bd, (j + 1) * bd)
        z = jnp.dot(vcb[:, sl], wbd_ref[j], preferred_element_type=F32)
        ig = 0.5 * jnp.tanh(z[:, bd:] + bi[:, sl]) + 0.5
        m = jnp.tanh(z[:, :bd] + br[:, sl]) * half_rate[:, sl] + half_rate[:, sl]
        a = jnp.exp(-m)
        _to_slab(sa, a, j * bd)
        _to_slab(su, jnp.sqrt(jnp.tanh(m) * (a * a + 1.0)) * ig * vc[:, sl], j * bd)


def _scan_rows(out_ref, h_in, reverse, sa, su, sh):
    t, r = out_ref.shape
    nj = r // LANES
    groups = t // SUBLANES
    for j in range(nj):
        sh[j:j + 1, :] = h_in[:, j * LANES:(j + 1) * LANES]
    h0 = sh[0:nj, :]

    def body(k, h):
        g = groups - 1 - k if reverse else k
        base = g * (nj * SUBLANES)
        for row in (range(SUBLANES - 1, -1, -1) if reverse else range(SUBLANES)):
            src = pl.ds(base + row, nj, stride=SUBLANES)
            h = sa[src, :] * h + su[src, :]
            sh[src, :] = h
        return h

    h = lax.fori_loop(0, groups, body, h0, unroll=2)
    for g in range(t // SUBLANES):
        for j in range(nj):
            out_ref[g * SUBLANES:(g + 1) * SUBLANES, j * LANES:(j + 1) * LANES] = \
                sh[pl.ds((g * nj + j) * SUBLANES, SUBLANES), :]
    sh[0:nj, :] = h
    return jnp.concatenate([sh[j:j + 1, :] for j in range(nj)], axis=1)


def _tile_flags(i, cfg):
    is_ctx = i < cfg.ntc
    k = jnp.maximum(i - cfg.ntc, 0) % cfg.tpr
    first = jnp.logical_or(is_ctx, k == 0)
    last = jnp.logical_or(is_ctx, k == cfg.tpr - 1)
    return is_ctx, first, last


def _cast_chunks(w, layer, n_chunks, chunk_of):
    depth, e, rows, cols = w.shape
    assert (e * rows) % (n_chunks * BF16_ROWS) == 0
    cr = e * rows // n_chunks
    return (w.reshape(depth, n_chunks, cr, cols),
            pl.BlockSpec((1, 1, cr, cols), lambda s: (layer, chunk_of(s), 0, 0)),
            pl.BlockSpec((1, cr, cols), lambda s: (chunk_of(s), 0, 0)),
            jax.ShapeDtypeStruct((n_chunks, cr, cols), BF16))


def _scan_bwd_kernel(xc_ref, xl_ref, xprev_ref, xnext_ref, mod_ref, g1_ref, wv_ref, cw_ref, cb_ref, wbd_ref,
                     br_ref, bi_ref, lam_ref, h0_ref, ew_ref, vc_ref, hb_ref, hlast_ref, ewb_ref,
                     carry_scr, sa, su, sh, *, cfg):
    i = cfg.nt - 1 - pl.program_id(0)
    t, d = cfg.t, cfg.d
    is_ctx, first, last = _tile_flags(i, cfg)
    ewb_ref[0] = ew_ref[0, 0].astype(BF16)
    x_ext = jnp.concatenate([xprev_ref[0], jnp.where(is_ctx, xc_ref[0], xl_ref[0]), xnext_ref[0]], axis=0)
    m = mod_ref[0]
    hn = _norm_mod(x_ext, g1_ref[...], m[:, d:2 * d], m[:, 0:d]).astype(BF16)
    v_ext = jnp.dot(hn, wv_ref[...], preferred_element_type=F32)
    prev = jnp.where(first, 0.0, v_ext[0:SUBLANES])
    nxt = jnp.where(last, 0.0, v_ext[t + SUBLANES:t + 2 * SUBLANES])
    vc = _conv(v_ext[SUBLANES:t + SUBLANES], prev, nxt, cw_ref[...], cb_ref[...])
    vc_ref[...] = vc
    _gates(vc, wbd_ref, br_ref[...], bi_ref[...], lam_ref[...], sa, su, cfg)

    @pl.when(last)
    def _():
        carry_scr[...] = h0_ref[0]

    h = _scan_rows(hb_ref, carry_scr[...], True, sa, su, sh)
    carry_scr[...] = h

    @pl.when(is_ctx)
    def _():
        hlast_ref[0] = h


def _scan_bwd(xc, xl, mod, g1, wv, cw, cb, wbd, br, bi, lam, h0, ew, layer, cfg):
    t, d, r, nt = cfg.t, cfg.d, cfg.r, cfg.nt
    n = cfg.n_tok
    tb = t // SUBLANES
    rev = lambda s: nt - 1 - s
    full = lambda a: pl.BlockSpec(a.shape, lambda s: (0,) * a.ndim)
    tile = pl.BlockSpec((t, r), lambda s: (rev(s), 0))
    tab = lambda width: pl.BlockSpec((1, 1, width), lambda s: (_tab_row(rev(s), cfg), 0, 0))

    def prev_map(s):
        b, k = _lat_pos(rev(s), cfg)
        return (b, jnp.maximum(k * tb - 1, 0), 0)

    def next_map(s):
        b, k = _lat_pos(rev(s), cfg)
        return (b, jnp.minimum((k + 1) * tb, cfg.n_lat // SUBLANES - 1), 0)

    ewc, ew_in, ew_out, ew_shape = _cast_chunks(ew, layer, nt - cfg.ntc,
                                                lambda s: jnp.maximum(rev(s) - cfg.ntc, 0))
    vc, hb, hb_last, ewb = pl.pallas_call(
        functools.partial(_scan_bwd_kernel, cfg=cfg),
        grid=(nt,),
        in_specs=_x_specs(cfg, rev) + [
            pl.BlockSpec((1, SUBLANES, d), prev_map), pl.BlockSpec((1, SUBLANES, d), next_map),
            tab(N_MOD * d), full(g1), full(wv), full(cw), full(cb), full(wbd), full(br), full(bi), full(lam),
            tab(r), ew_in],
        out_specs=[tile, tile,
                   pl.BlockSpec((1, 1, r), lambda s: (jnp.minimum(rev(s), cfg.ntc - 1), 0, 0)), ew_out],
        out_shape=[jax.ShapeDtypeStruct((n, r), F32),
                   jax.ShapeDtypeStruct((n, r), F32),
                   jax.ShapeDtypeStruct((cfg.n_ctx_req, 1, r), F32), ew_shape],
        scratch_shapes=[pltpu.VMEM((1, r), F32)] + [pltpu.VMEM((t * r // LANES, LANES), F32)] * 3,
        compiler_params=_cparams(),
        name="scan_bwd",
    )(xc, xl, xl, xl, mod, g1, wv, cw, cb, wbd, br, bi, lam, h0, ewc)
    return vc, hb, hb_last, ewb.reshape(ew.shape[1:])


def _pool(pv, is_ctx, cfg):
    t = pv.shape[0]
    gw = cfg.gw
    row = lax.broadcasted_iota(I32, (t, gw), 0)
    pos = jnp.where(is_ctx, row % cfg.n_ctx, row % cfg.grid_w)
    length = jnp.where(is_ctx, cfg.n_ctx, cfg.grid_w)

    def shifted(x, dlt):
        ok = jnp.logical_and(pos + dlt >= 0, pos + dlt < length)
        return jnp.where(ok, pltpu.roll(x, (-dlt) % t, 0), 0.0)

    outs = []
    for g, m in enumerate(POOL_HALF_WINDOWS):
        x = pv[:, g * gw:(g + 1) * gw]
        back, fwd, k = x, x, 1
        while k < m:
            back = back + shifted(back, -k)
            fwd = fwd + shifted(fwd, k)
            k *= 2
        s = shifted(back, -1) + fwd
        cnt = (jnp.minimum(pos + m, length) - jnp.maximum(pos - m, 0)).astype(F32)
        outs.append(s / cnt - x)
    return outs


def _scan_fwd_kernel(vc_ref, h0_ref, wbd_ref, br_ref, bi_ref, lam_ref, ew_ref,
                     hf_ref, hlast_ref, ewb_ref, carry_scr, sa, su, sh, *, cfg):
    i = pl.program_id(0)
    is_ctx, first, _ = _tile_flags(i, cfg)
    ewb_ref[0] = ew_ref[0, 0].astype(BF16)
    _gates(vc_ref[...], wbd_ref, br_ref[...], bi_ref[...], lam_ref[...], sa, su, cfg)

    @pl.when(first)
    def _():
        carry_scr[...] = h0_ref[0]

    h = _scan_rows(hf_ref, carry_scr[...], False, sa, su, sh)
    carry_scr[...] = h

    @pl.when(is_ctx)
    def _():
        hlast_ref[0] = h


def _scan_fwd(vc, h0, wbd, br, bi, lam, ew, layer, cfg):
    t, r = cfg.t, cfg.r
    tile = pl.BlockSpec((t, r), lambda i: (i, 0))
    full = lambda a: pl.BlockSpec(a.shape, lambda i: (0,) * a.ndim)
    ewc, ew_in, ew_out, ew_shape = _cast_chunks(ew, layer, cfg.nt - cfg.ntc,
                                                lambda i: jnp.maximum(i - cfg.ntc, 0))
    hf, hf_last, ewb = pl.pallas_call(
        functools.partial(_scan_fwd_kernel, cfg=cfg),
        grid=(cfg.nt,),
        in_specs=[tile, pl.BlockSpec((1, 1, r), lambda i: (_tab_row(i, cfg), 0, 0)),
                  full(wbd), full(br), full(bi), full(lam), ew_in],
        out_specs=[tile, pl.BlockSpec((1, 1, r), lambda i: (jnp.minimum(i, cfg.ntc - 1), 0, 0)), ew_out],
        out_shape=[jax.ShapeDtypeStruct((cfg.n_tok, r), F32),
                   jax.ShapeDtypeStruct((cfg.n_ctx_req, 1, r), F32), ew_shape],
        scratch_shapes=[pltpu.VMEM((1, r), F32)] + [pltpu.VMEM((t * r // LANES, LANES), F32)] * 3,
        compiler_params=_cparams(),
        name="scan_fwd",
    )(vc, h0, wbd, br, bi, lam, ewc)
    return hf, hf_last, ewb.reshape(ew.shape[1:])


def _mix_kernel(xc_ref, xl_ref, hf_ref, hb_ref, mod_ref, g1_ref, wp_ref, wg_ref, pw_ref, ps_ref, wbp_ref,
                wbl_ref, wo_ref, g2_ref, rwa_ref, rwb_ref, ew_ref, x1_ref, hn2_ref, lg_ref, ewb_ref, *, cfg):
    i = pl.program_id(0)
    ewb_ref[0] = ew_ref[0, 0].astype(BF16)
    d, gw = cfg.d, cfg.gw
    is_ctx = i < cfg.nbc
    m = mod_ref[0]
    ps = ps_ref[...]
    dot = functools.partial(jnp.dot, preferred_element_type=F32)
    halves = [slice(h * cfg.n_ctx, (h + 1) * cfg.n_ctx) for h in range(cfg.tb // cfg.n_ctx)]
    st = [dict() for _ in halves]

    def head(s, rs):
        s['x'] = jnp.where(is_ctx, xc_ref[rs, :], xl_ref[rs, :])
        s['hn'] = _norm_mod(s['x'], g1_ref[...], m[:, d:2 * d], m[:, 0:d]).astype(BF16)

    def project(s, rs):
        s['p'] = dot(s['hn'], wp_ref[...])
        s['gl'] = dot(s['hn'], wg_ref[...])

    def pool(s, rs):
        s['y_lru'] = (hf_ref[rs, :] + hb_ref[rs, :]).astype(BF16)
        s['pooled'] = [q.astype(BF16) for q in _pool(s.pop('p'), is_ctx, cfg)]

    def branches(s, rs):
        y_pool = jnp.concatenate([dot(q, pw_ref[g]) * ps[:, g * gw:(g + 1) * gw]
                                  for g, q in enumerate(s.pop('pooled'))], axis=1).astype(BF16)
        s['bp'] = dot(y_pool, wbp_ref[...])
        s['bl'] = dot(s.pop('y_lru'), wbl_ref[...])

    def merge(s, rs):
        gl = s.pop('gl')
        s['merged'] = (_sigmoid(gl[:, :d]) * s.pop('bp') + _sigmoid(gl[:, d:]) * s.pop('bl')).astype(BF16)

    def out_proj(s, rs):
        s['mix'] = dot(s.pop('merged'), wo_ref[...])

    def tail(s, rs):
        x1 = s.pop('x') + m[:, 2 * d:3 * d] * s.pop('mix')
        x1_ref[rs, :] = x1
        hn2 = _norm_mod(x1, g2_ref[...], m[:, 4 * d:5 * d], m[:, 3 * d:4 * d])
        hi = hn2.astype(BF16)
        hn2_ref[rs, :] = hi
        lo = (hn2 - hi.astype(F32)).astype(BF16)
        nt = functools.partial(lax.dot_general, dimension_numbers=(((1,), (1,)), ((), ())),
                               preferred_element_type=F32)
        lg_ref[:, rs] = nt(rwa_ref[...], hi) + nt(rwb_ref[...], hi) + nt(rwa_ref[...], lo)

    for stage in (head, project, pool, branches, merge, out_proj, tail):
        for s, rs in zip(st, halves):
            stage(s, rs)


def _mix(xc, xl, hf, hb, mod, g1, wp, wg, pw, ps, wbp, wbl, wo, g2, rwa, rwb, ew, layer, cfg):
    tb, d, r = cfg.tb, cfg.d, cfg.r
    n = cfg.n_tok
    nbc = cfg.nbc
    tile = lambda width: pl.BlockSpec((tb, width), lambda i: (i, 0))
    full = lambda a: pl.BlockSpec(a.shape, lambda i: (0,) * a.ndim)
    tab_row = lambda i: jnp.where(i < nbc, 0, 1 + (i - nbc) // (cfg.n_lat // tb))
    ewc, ew_in, ew_out, ew_shape = _cast_chunks(ew, layer, n // tb - nbc, lambda i: jnp.maximum(i - nbc, 0))
    x1, hn2, logits, ewb = pl.pallas_call(
        functools.partial(_mix_kernel, cfg=cfg),
        grid=(n // tb,),
        in_specs=[pl.BlockSpec((tb, d), lambda i: (jnp.minimum(i, nbc - 1), 0)),
                  pl.BlockSpec((tb, d), lambda i: (jnp.maximum(i - nbc, 0), 0)),
                  tile(r), tile(r), pl.BlockSpec((1, 1, N_MOD * d), lambda i: (tab_row(i), 0, 0)),
                  full(g1), full(wp), full(wg), full(pw), full(ps), full(wbp), full(wbl), full(wo),
                  full(g2), full(rwa), full(rwb), ew_in],
        out_specs=[tile(d), tile(d), pl.BlockSpec((cfg.e, tb), lambda i: (0, i)), ew_out],
        out_shape=[jax.ShapeDtypeStruct((n, d), F32),
                   jax.ShapeDtypeStruct((n, d), BF16),
                   jax.ShapeDtypeStruct((cfg.e, n), F32), ew_shape],
        compiler_params=_cparams(),
        name="mix",
    )(xc.reshape(-1, d), xl.reshape(-1, d), hf, hb, mod, g1, wp, wg, pw, ps, wbp, wbl, wo, g2, rwa, rwb, ewc)
    return x1, hn2, logits, ewb.reshape(ew.shape[1:])


def _route_kernel(l_ref, selpos_ref, gate_ref, offs_ref, cnts_ref, *, cap, t):
    lg = l_ref[...]
    nreq, e, n = lg.shape
    rows = nreq * e
    nch = n // t
    ex = jnp.exp(lg - jnp.max(lg, axis=1, keepdims=True))
    aff = (ex / jnp.sum(ex, axis=1, keepdims=True)).reshape(rows, n)
    gate_ref[...] = aff
    keys = pltpu.bitcast(aff, I32)

    def count(mask):
        return jnp.sum(mask.astype(F32), axis=1, keepdims=True)

    thr = jnp.zeros((rows, 1), I32)
    for bit in range(30, -1, -1):
        cand = thr | (1 << bit)
        thr = jnp.where(count(keys >= cand) >= cap, cand, thr)
    gt = keys > thr
    eq = keys == thr
    need = cap - count(gt)

    tri = (lax.broadcasted_iota(I32, (t, t), 0) < lax.broadcasted_iota(I32, (t, t), 1)).astype(BF16)
    tok = lax.broadcasted_iota(I32, (n, LANES), 0)
    chunk = lax.broadcasted_iota(I32, (n, LANES), 1)
    before = (tok < chunk * t).astype(BF16)
    inside = (tok // t == chunk).astype(BF16)

    def prefix(mask):
        mb = mask.astype(BF16)
        offs = jnp.dot(mb, before, preferred_element_type=F32)
        pre = [jnp.dot(mb[:, k * t:(k + 1) * t], tri, preferred_element_type=F32) + offs[:, k:k + 1]
               for k in range(nch)]
        return jnp.concatenate(pre, axis=1) if nch > 1 else pre[0], offs, mb

    eq_rank, _, _ = prefix(eq)
    sel = jnp.logical_or(gt, jnp.logical_and(eq, eq_rank < need))
    pos, offs, selb = prefix(sel)
    selpos_ref[...] = jnp.where(sel, pos.astype(I32), NOT_SELECTED)
    offs_ref[...] = offs.astype(I32)
    cnts_ref[...] = jnp.dot(selb, inside, preferred_element_type=F32).astype(I32)


def _route(logits, cap, cfg):
    nreq, e, n = logits.shape
    rows = nreq * e
    whole = lambda width: pl.BlockSpec((rows, width), lambda b: (0, 0))
    return pl.pallas_call(
        functools.partial(_route_kernel, cap=cap, t=cfg.t),
        grid=(1,),
        in_specs=[pl.BlockSpec((nreq, e, n), lambda b: (0, 0, 0))],
        out_specs=[whole(n), whole(n), whole(LANES), whole(LANES)],
        out_shape=[jax.ShapeDtypeStruct((rows, n), I32),
                   jax.ShapeDtypeStruct((rows, n), F32),
                   jax.ShapeDtypeStruct((rows, LANES), I32),
                   jax.ShapeDtypeStruct((rows, LANES), I32)],
        compiler_params=_cparams(),
        name="route",
    )(logits)


def _slot_block(i, cfg):
    return jnp.where(i < cfg.ntc, i // cfg.rpb, cfg.ntc // cfg.rpb + (i - cfg.ntc) // cfg.tpr)


def _window_tables(toff, tcnt, cfg):
    tile = jnp.arange(cfg.nt, dtype=I32)
    base = jnp.where(tile < cfg.ntc, (tile % cfg.rpb) * cfg.c_ctx, 0)[:, None]
    off = base + toff.reshape(cfg.nt, cfg.e)
    start = (off // BF16_ROWS) * BF16_ROWS
    npass = jnp.max((off + tcnt.reshape(cfg.nt, cfg.e) - start + cfg.w - 1) // cfg.w, axis=1)
    return start.reshape(-1).astype(I32), npass.astype(I32)


def _windows(i, tstart_ref, tnpass_ref, cfg):
    base = jnp.where(i < cfg.ntc, (i % cfg.rpb) * cfg.c_ctx, 0)
    return base, [tstart_ref[i * cfg.e + e] for e in range(cfg.e)], tnpass_ref[i]


def _window(start, j, cfg):
    want = start + j * cfg.w
    a = pl.multiple_of(jnp.minimum(want, cfg.sblk - cfg.w), BF16_ROWS)
    return a, want - a


def _dispatch_kernel(toff_ref, tcnt_ref, x_ref, sp_ref, g_ref, xs_ref, gs_ref, p_scr, *, cfg):
    i = pl.program_id(0)
    w = cfg.w
    is_ctx = i < cfg.ntc
    new_block = jnp.where(is_ctx, i % cfg.rpb == 0, jnp.maximum(i - cfg.ntc, 0) % cfg.tpr == 0)

    @pl.when(new_block)
    def _():
        xs_ref[...] = jnp.zeros(xs_ref.shape, xs_ref.dtype)
        gs_ref[...] = jnp.zeros(gs_ref.shape, gs_ref.dtype)

    base, starts, npass = _windows(i, toff_ref, tcnt_ref, cfg)
    slot = lax.broadcasted_iota(I32, (w, cfg.t), 0)

    def one_pass(j, carry):
        firsts = []
        for e in range(cfg.e):
            a, owned = _window(starts[e], j, cfg)
            firsts.append(a)
            rel = sp_ref[e:e + 1, :] - (a - base)
            hit = jnp.logical_and(rel == slot, slot >= owned)
            p_scr[e * w:(e + 1) * w, :] = hit.astype(BF16)
            gs_ref[e, pl.ds(a, w), :] += jnp.sum(jnp.where(hit, g_ref[e:e + 1, :], 0.0), axis=1, keepdims=True)
        rows = jnp.dot(p_scr[...], x_ref[...], preferred_element_type=F32)
        for e in range(cfg.e):
            xs_ref[e, pl.ds(firsts[e], w), :] += rows[e * w:(e + 1) * w].astype(BF16)
        return carry

    lax.fori_loop(0, npass, one_pass, 0)


def _dispatch(toff, tcnt, hn2, selpos, gate, cfg):
    t, d, e = cfg.t, cfg.d, cfg.e
    s = cfg.n_sblk * cfg.sblk
    slots = lambda width: pl.BlockSpec((e, cfg.sblk, width), lambda i, *_: (0, _slot_block(i, cfg), 0))
    return pl.pallas_call(
        functools.partial(_dispatch_kernel, cfg=cfg),
        grid_spec=pltpu.PrefetchScalarGridSpec(
            num_scalar_prefetch=2,
            grid=(cfg.nt,),
            in_specs=[pl.BlockSpec((t, d), lambda i, *_: (i, 0)),
                      pl.BlockSpec((e, t), lambda i, *_: (0, i)),
                      pl.BlockSpec((e, t), lambda i, *_: (0, i))],
            out_specs=[slots(d), slots(1)],
            scratch_shapes=[pltpu.VMEM((e * cfg.w, t), BF16)]),
        out_shape=[jax.ShapeDtypeStruct((e, s, d), BF16), jax.ShapeDtypeStruct((e, s, 1), F32)],
        compiler_params=_cparams(),
        name="dispatch",
    )(toff, tcnt, hn2, selpos, gate)


def _ffn_kernel(x_ref, gs_ref, w1_ref, w3_ref, w2_ref, o_ref, *, fc):
    x = x_ref[0]
    acc = jnp.zeros((x.shape[0], o_ref.shape[2]), F32)
    for c in range(w1_ref.shape[2] // fc):
        sl = slice(c * fc, (c + 1) * fc)
        h1 = jnp.dot(x, w1_ref[0, :, sl], preferred_element_type=F32)
        h3 = jnp.dot(x, w3_ref[0, :, sl], preferred_element_type=F32)
        hid = (h1 * _sigmoid(h1)) * h3
        acc = acc + jnp.dot(hid.astype(BF16), w2_ref[0, sl, :], preferred_element_type=F32)
    o_ref[0] = (acc * gs_ref[0]).astype(BF16)


def _ffn(xs, gs, w1, w3, w2, cfg):
    e, s, d = xs.shape
    f = w1.shape[2]
    tm = 2 * cfg.sblk if cfg.n_sblk % 2 == 0 else cfg.sblk
    return pl.pallas_call(
        functools.partial(_ffn_kernel, fc=min(512, f)),
        grid=(e, s // tm),
        in_specs=[pl.BlockSpec((1, tm, d), lambda k, m: (k, m, 0)),
                  pl.BlockSpec((1, tm, 1), lambda k, m: (k, m, 0)),
                  pl.BlockSpec((1, d, f), lambda k, m: (k, 0, 0)),
                  pl.BlockSpec((1, d, f), lambda k, m: (k, 0, 0)),
                  pl.BlockSpec((1, f, d), lambda k, m: (k, 0, 0))],
        out_specs=pl.BlockSpec((1, tm, d), lambda k, m: (k, m, 0)),
        out_shape=jax.ShapeDtypeStruct((e, s, d), BF16),
        compiler_params=pltpu.CompilerParams(dimension_semantics=("arbitrary", "arbitrary"),
                                             vmem_limit_bytes=VMEM_LIMIT),
        name="ffn",
    )(xs, gs, w1, w3, w2)


def _combine_kernel(toff_ref, tcnt_ref, x1_ref, mod_ref, spt_ref, o_ref, fg_ref, oc_ref, ol_ref, ow_scr,
                    *, cfg, final):
    i = pl.program_id(0)
    w, d = cfg.w, cfg.d
    base, starts, npass = _windows(i, toff_ref, tcnt_ref, cfg)
    ne = cfg.e
    expert = lax.broadcasted_iota(I32, (1, ne), 1)
    expand = (lax.broadcasted_iota(I32, (ne, ne * w), 1) // w
              == lax.broadcasted_iota(I32, (ne, ne * w), 0)).astype(BF16)
    row_in_window = (lax.broadcasted_iota(I32, (cfg.t, ne * w), 1) % w).astype(F32)
    spt = spt_ref[...]

    def one_pass(j):
        shift = jnp.zeros((1, ne), I32)
        owned = jnp.zeros((1, ne), I32)
        for e in range(ne):
            a, own = _window(starts[e], j, cfg)
            ow_scr[e * w:(e + 1) * w, :] = o_ref[e, pl.ds(a, w), :]
            shift = jnp.where(expert == e, a - base, shift)
            owned = jnp.where(expert == e, own, owned)
        rel = spt - shift
        rel = jnp.where(jnp.logical_and(rel >= owned, rel < w), rel, -1)
        wide = jnp.dot(rel.astype(F32).astype(BF16), expand, preferred_element_type=F32)
        hits = (wide == row_in_window).astype(BF16)
        return jnp.dot(hits, ow_scr[...], preferred_element_type=F32)

    is_ctx = i < cfg.ntc

    def finish(y):
        x2 = x1_ref[...] + mod_ref[0][:, 5 * d:6 * d] * y
        if final:
            x2 = (x2 * lax.rsqrt(jnp.mean(x2 * x2, axis=-1, keepdims=True) + EPS)) * fg_ref[...]

        @pl.when(is_ctx)
        def _():
            oc_ref[0] = x2

        @pl.when(jnp.logical_not(is_ctx))
        def _():
            ol_ref[0] = x2

    @pl.when(npass <= 1)
    def _():
        finish(one_pass(0))

    @pl.when(npass > 1)
    def _():
        finish(lax.fori_loop(1, npass, lambda j, y: y + one_pass(j), one_pass(0)))


def _combine(toff, tcnt, x1, mod, selpos_t, o, fg, cfg, final):
    t, d, e = cfg.t, cfg.d, cfg.e
    return pl.pallas_call(
        functools.partial(_combine_kernel, cfg=cfg, final=final),
        grid_spec=pltpu.PrefetchScalarGridSpec(
            num_scalar_prefetch=2,
            grid=(cfg.nt,),
            in_specs=[pl.BlockSpec((t, d), lambda i, *_: (i, 0)),
                      pl.BlockSpec((1, 1, N_MOD * d), lambda i, *_: (_tab_row(i, cfg), 0, 0)),
                      pl.BlockSpec((t, e), lambda i, *_: (i, 0)),
                      pl.BlockSpec((e, cfg.sblk, d), lambda i, *_: (0, _slot_block(i, cfg), 0)),
                      pl.BlockSpec((1, d), lambda i, *_: (0, 0))],
            out_specs=_x_specs(cfg, lambda i: i),
            scratch_shapes=[pltpu.VMEM((e * cfg.w, d), BF16)]),
        out_shape=[jax.ShapeDtypeStruct((cfg.n_ctx_req, cfg.n_ctx, d), F32),
                   jax.ShapeDtypeStruct((cfg.n_lat_req, cfg.n_lat, d), F32)],
        compiler_params=_cparams(),
        name="combine",
    )(toff, tcnt, x1, mod, selpos_t, o, fg)


def _block_diag_gates(wr, wi, cfg):
    hpb = cfg.bd // cfg.dh
    nblk = cfg.r // cfg.bd
    eye = jnp.eye(hpb, dtype=wr.dtype)

    def blocks(w):
        w4 = w.reshape(nblk, hpb, cfg.dh, cfg.dh)
        return jnp.einsum('jhde,hk->jhdke', w4, eye).reshape(nblk, cfg.bd, cfg.bd)

    return (0.5 * jnp.concatenate([blocks(wr), blocks(wi)], axis=2)).astype(BF16)


def _tile_tables(offs_ctx, offs_lat, cfg):
    ctx = offs_ctx[:, 0].reshape(-1, cfg.e)
    lat = jnp.swapaxes(offs_lat[:, :cfg.tpr].reshape(-1, cfg.e, cfg.tpr), 1, 2).reshape(-1, cfg.e)
    return jnp.concatenate([ctx, lat], axis=0).reshape(-1)


def _by_expert(rows_ctx, rows_lat, cfg):
    def flip(a):
        return jnp.swapaxes(a.reshape(-1, cfg.e, a.shape[1]), 0, 1).reshape(cfg.e, -1)
    return jnp.concatenate([flip(rows_ctx), flip(rows_lat)], axis=1)


def _forward(cfg, x_prompt, x_sample, state_lru, c, c_ctx, norm1_g, norm2_g, final_g, w_mod, b_mod, w_in,
             pool_w, pool_scale, conv_w, conv_b, lru_wr, lru_br, lru_wi, lru_bi, lru_lambda,
             w_br_pool, w_br_lru, w_out, router_w, exp_w1, exp_w3, exp_w2):
    d, r, p = cfg.d, cfg.r, cfg.p
    depth = w_in.shape[0]
    assert cfg.n_lat % cfg.t == 0 and cfg.t % cfg.grid_w == 0 and cfg.n_ctx_req % cfg.rpb == 0
    assert cfg.w % BF16_ROWS == 0
    assert (cfg.n_ctx_req * cfg.n_ctx) % cfg.tb == 0 and cfg.n_lat % cfg.tb == 0 and cfg.tb % cfg.grid_w == 0

    n_ctx_tok = cfg.n_ctx_req * cfg.n_ctx
    xc, xl = x_prompt, x_sample

    cvec = jnp.zeros((cfg.n_tab, d), F32).at[0].set(c_ctx).at[1:1 + cfg.n_lat_req].set(c)
    mod = _modulation(cvec, w_mod, b_mod, cfg)

    states = []
    for l in range(depth):
        mod_l = mod[l].reshape(cfg.n_tab, 1, N_MOD * d)
        row = lambda a: a.reshape(1, -1)
        wp = w_in[l][:, :p].astype(BF16)
        wv = w_in[l][:, p:p + r].astype(BF16)
        wg = w_in[l][:, p + r:].astype(BF16)
        rwa = router_w[l].T.astype(BF16)
        rwb = (router_w[l].T - rwa.astype(F32)).astype(BF16)

        def h0(direction):
            tab = jnp.zeros((cfg.n_tab, 1, r), F32)
            return tab.at[1:1 + cfg.n_lat_req, 0].set(state_lru[:, l, direction].astype(F32))

        wbd = [_block_diag_gates(lru_wr[l, z], lru_wi[l, z], cfg) for z in range(2)]
        vc, hb, hb_last, w1b = _scan_bwd(xc, xl, mod_l, row(norm1_g[l]), wv, conv_w[l], row(conv_b[l]), wbd[1],
                                         row(0.5 * lru_br[l, 1]), row(0.5 * lru_bi[l, 1]),
                                         row(lru_lambda[l, 1]), h0(1), exp_w1, l, cfg)
        hf, hf_last, w3b = _scan_fwd(vc, h0(0), wbd[0], row(0.5 * lru_br[l, 0]), row(0.5 * lru_bi[l, 0]),
                                     row(lru_lambda[l, 0]), exp_w3, l, cfg)
        x1, hn2, logits, w2b = _mix(xc, xl, hf, hb, mod_l, row(norm1_g[l]), wp, wg, pool_w[l].astype(BF16),
                                    row(pool_scale[l]), w_br_pool[l].astype(BF16), w_br_lru[l].astype(BF16),
                                    w_out[l].astype(BF16), row(norm2_g[l]), rwa, rwb, exp_w2, l, cfg)
        states.append(jnp.stack([hf_last[:, 0], hb_last[:, 0]], axis=1))

        by_request = lambda a, n: jnp.swapaxes(a.reshape(cfg.e, -1, n), 0, 1)
        sp_c, g_c, off_c, cnt_c = _route(by_request(logits[:, :n_ctx_tok], cfg.n_ctx), cfg.c_ctx, cfg)
        sp_l, g_l, off_l, cnt_l = _route(by_request(logits[:, n_ctx_tok:], cfg.n_lat), cfg.c_lat, cfg)
        selpos = _by_expert(sp_c, sp_l, cfg)
        toff, tcnt = _window_tables(_tile_tables(off_c, off_l, cfg), _tile_tables(cnt_c, cnt_l, cfg), cfg)

        xs, gs = _dispatch(toff, tcnt, hn2, selpos, _by_expert(g_c, g_l, cfg), cfg)
        o = _ffn(xs, gs, w1b, w3b, w2b, cfg)
        xc, xl = _combine(toff, tcnt, x1, mod_l, selpos.T, o, row(final_g), cfg, l == depth - 1)

    new_state = jnp.stack(states, axis=1).astype(x_prompt.dtype)
    return xc, xl, new_state


def kernel(x_prompt, x_sample, state_lru, c, c_ctx, norm1_g, norm2_g, final_g, w_mod, b_mod, w_in, pool_w,
           pool_scale, conv_w, conv_b, lru_wr, lru_br, lru_wi, lru_bi, lru_lambda, w_br_pool, w_br_lru, w_out,
           router_w, exp_w1, exp_w3, exp_w2):
    cfg = Cfg(d=x_prompt.shape[2], gw=pool_w.shape[2], r=lru_lambda.shape[2], dh=lru_wr.shape[3],
              e=router_w.shape[2], f=exp_w1.shape[3], n_ctx_req=x_prompt.shape[0], n_ctx=x_prompt.shape[1],
              n_lat_req=x_sample.shape[0], n_lat=x_sample.shape[1], grid_w=GRID_W)
    return _forward(cfg, x_prompt, x_sample, state_lru, c, c_ctx, norm1_g, norm2_g, final_g, w_mod, b_mod, w_in,
                    pool_w, pool_scale, conv_w, conv_b, lru_wr, lru_br, lru_wi, lru_bi, lru_lambda,
                    w_br_pool, w_br_lru, w_out, router_w, exp_w1, exp_w3, exp_w2)
```
